```python
import jax, jax.numpy as jnp
from jax import lax
import numpy as np

D_MODEL = 1024
BATCH = 2
SEQ = 8192
DEPTH = 1

HEAD_DIM = 64
D_MIX = D_MODEL
N_HEADS_A = (D_MIX // 2) // HEAD_DIM
N_KV_A = 2
N_HEADS_B = (D_MIX // 2) // HEAD_DIM
DILATED_CONFIGS = ((128, 1), (512, 4), (2048, 16))
Q_BLOCK = 128
GRID_W = 64
ROPE_THETA = 10000.0
D_FF = 2816
EPS = 1e-6
NEG_INF = -1e30

A_Q = N_HEADS_A * HEAD_DIM
A_KV = N_KV_A * HEAD_DIM
B_QKV = N_HEADS_B * HEAD_DIM
QKV_COLS = A_Q + 2 * A_KV + 3 * B_QKV
QKV_SPLITS = (A_Q, A_Q + A_KV, A_Q + 2 * A_KV, A_Q + 2 * A_KV + B_QKV, A_Q + 2 * A_KV + 2 * B_QKV)

kernel_name = "hybrid_gqa_axial_dilated_macaron_block"


def _rmsnorm(x, g):
    xf = x.astype(jnp.float32)
    r = lax.rsqrt(jnp.mean(xf * xf, axis=-1, keepdims=True) + EPS)
    return (xf * r * g.astype(jnp.float32)).astype(x.dtype)


def _rope_angles(pos, dim):
    freqs = ROPE_THETA ** (-jnp.arange(0, dim, 2, dtype=jnp.float32) / dim)
    return pos.astype(jnp.float32)[:, None] * freqs[None, :]


def _apply_rope(x, ang):
    xf = x.astype(jnp.float32)
    x1, x2 = jnp.split(xf, 2, axis=-1)
    cos = jnp.cos(ang)[None, :, None, :]
    sin = jnp.sin(ang)[None, :, None, :]
    return jnp.concatenate([x1 * cos - x2 * sin, x2 * cos + x1 * sin], axis=-1).astype(x.dtype)


def _apply_axial_rope(x, ang_row, ang_col):
    xr, xc = jnp.split(x, 2, axis=-1)
    return jnp.concatenate([_apply_rope(xr, ang_row), _apply_rope(xc, ang_col)], axis=-1)


def _swiglu(x, w_gate, w_up, w_down):
    return (jax.nn.silu(x @ w_gate) * (x @ w_up)) @ w_down


def _gqa_dense_blocked(q, k, v):
    B, S, Hq, D = q.shape
    Hkv = k.shape[2]
    G = Hq // Hkv
    nq = S // Q_BLOCK
    scale = D ** -0.5
    qb = q.reshape(B, nq, Q_BLOCK, Hkv, G, D).transpose(1, 0, 2, 3, 4, 5)

    def block(qi):
        s = jnp.einsum('bqkgd,bskd->bkgqs', qi, k, preferred_element_type=jnp.float32) * scale
        p = jax.nn.softmax(s, axis=-1)
        o = jnp.einsum('bkgqs,bskd->bqkgd', p.astype(v.dtype), v, preferred_element_type=jnp.float32)
        return o.astype(v.dtype)

    o = lax.map(block, qb)
    return o.transpose(1, 0, 2, 3, 4, 5).reshape(B, S, Hq, D)


def _dilated_window_partial(q, k, v, window, dilation):
    B, S, H, D = q.shape
    span = (window // 2) // dilation
    L = S // dilation
    blk = span
    nb = -(-L // blk)
    Lp = nb * blk
    scale = D ** -0.5

    def strided(x):
        return x.reshape(B, L, dilation, H, D).transpose(0, 2, 1, 3, 4)

    qs = jnp.pad(strided(q), ((0, 0), (0, 0), (0, Lp - L), (0, 0), (0, 0)))
    kv_pad = ((0, 0), (0, 0), (blk, Lp - L + blk), (0, 0), (0, 0))
    ks = jnp.pad(strided(k), kv_pad).reshape(B, dilation, nb + 2, blk, H, D)
    vs = jnp.pad(strided(v), kv_pad).reshape(B, dilation, nb + 2, blk, H, D)
    qb = qs.reshape(B, dilation, nb, blk, H, D)
    kw = jnp.concatenate([ks[:, :, :-2], ks[:, :, 1:-1], ks[:, :, 2:]], axis=3)
    vw = jnp.concatenate([vs[:, :, :-2], vs[:, :, 1:-1], vs[:, :, 2:]], axis=3)

    qpos = jnp.arange(nb)[:, None] * blk + jnp.arange(blk)[None, :]
    kpos = jnp.arange(nb)[:, None] * blk + jnp.arange(3 * blk)[None, :] - blk
    diff = kpos[:, None, :] - qpos[:, :, None]
    valid = (jnp.abs(diff) <= span) & (kpos[:, None, :] >= 0) & (kpos[:, None, :] < L)

    s = jnp.einsum('brcihe,brcjhe->brchij', qb, kw, preferred_element_type=jnp.float32) * scale
    s = jnp.where(valid[None, None, :, None], s, NEG_INF)
    m = jnp.max(s, axis=-1)
    p = jnp.exp(s - m[..., None])
    l = jnp.sum(p, axis=-1)
    o = jnp.einsum('brchij,brcjhe->brcihe', p.astype(vw.dtype), vw, preferred_element_type=jnp.float32)

    o = o.reshape(B, dilation, Lp, H, D)[:, :, :L].transpose(0, 2, 1, 3, 4).reshape(B, S, H, D)

    def unstride_stat(a):
        a = a.transpose(0, 1, 2, 4, 3).reshape(B, dilation, Lp, H)[:, :, :L]
        return a.transpose(0, 2, 1, 3).reshape(B, S, H)

    return o, unstride_stat(m), unstride_stat(l)


def _dilated_mixture(q, k, v):
    parts = [_dilated_window_partial(q, k, v, w, d) for (w, d) in DILATED_CONFIGS]
    m_all = jnp.max(jnp.stack([p[1] for p in parts], axis=0), axis=0)
    num = jnp.zeros(q.shape, jnp.float32)
    den = jnp.zeros(q.shape[:3], jnp.float32)
    for o_i, m_i, l_i in parts:
        w_i = jnp.exp(m_i - m_all)
        num = num + w_i[..., None] * o_i
        den = den + w_i * l_i
    return (num / den[..., None]).astype(q.dtype)


def setup_inputs(seed: int = 0) -> dict:
    key = jax.random.key(seed)
    ks = jax.random.split(key, 18)
    f32 = jnp.float32

    def w(k, shape, fan_in):
        return jax.random.normal(k, shape, f32) * (fan_in ** -0.5)

    def gain(k, dim):
        return 1.0 + 0.02 * jax.random.normal(k, (DEPTH, dim), f32)

    return {
        "x": jax.random.normal(ks[0], (BATCH, SEQ, D_MODEL), f32),
        "ffn1_pre_g": gain(ks[1], D_MODEL),
        "ffn1_post_g": gain(ks[2], D_MODEL),
        "ffn1_w_gate": w(ks[3], (DEPTH, D_MODEL, D_FF), D_MODEL),
        "ffn1_w_up": w(ks[4], (DEPTH, D_MODEL, D_FF), D_MODEL),
        "ffn1_w_down": w(ks[5], (DEPTH, D_FF, D_MODEL), D_FF),
        "mix_pre_g": gain(ks[6], D_MODEL),
        "mix_post_g": gain(ks[7], D_MODEL),
        "w_qkv": w(ks[8], (DEPTH, D_MODEL, QKV_COLS), D_MODEL),
        "q_norm_g": gain(ks[9], HEAD_DIM),
        "k_norm_g": gain(ks[10], HEAD_DIM),
        "w_out": w(ks[11], (DEPTH, D_MIX, D_MODEL), D_MIX),
        "ffn2_pre_g": gain(ks[12], D_MODEL),
        "ffn2_post_g": gain(ks[13], D_MODEL),
        "ffn2_w_gate": w(ks[14], (DEPTH, D_MODEL, D_FF), D_MODEL),
        "ffn2_w_up": w(ks[15], (DEPTH, D_MODEL, D_FF), D_MODEL),
        "ffn2_w_down": w(ks[16], (DEPTH, D_FF, D_MODEL), D_FF),
    }


def reference(x, ffn1_pre_g, ffn1_post_g, ffn1_w_gate, ffn1_w_up, ffn1_w_down,
              mix_pre_g, mix_post_g, w_qkv, q_norm_g, k_norm_g, w_out,
              ffn2_pre_g, ffn2_post_g, ffn2_w_gate, ffn2_w_up, ffn2_w_down):
    B, S, _ = x.shape
    ROWS = S // GRID_W
    pos = jnp.arange(S, dtype=jnp.int32)
    row = jnp.repeat(jnp.arange(ROWS, dtype=jnp.int32), GRID_W)
    col = jnp.tile(jnp.arange(GRID_W, dtype=jnp.int32), ROWS)
    ang_row = _rope_angles(row, HEAD_DIM // 2)
    ang_col = _rope_angles(col, HEAD_DIM // 2)
    ang_1d = _rope_angles(pos, HEAD_DIM)

    for l in range(DEPTH):
        f = _swiglu(_rmsnorm(x, ffn1_pre_g[l]), ffn1_w_gate[l], ffn1_w_up[l], ffn1_w_down[l])
        x = x + 0.5 * _rmsnorm(f, ffn1_post_g[l])

        h = _rmsnorm(x, mix_pre_g[l])
        qkv = h @ w_qkv[l]
        qa, ka, va, qb, kb, vb = jnp.split(qkv, QKV_SPLITS, axis=-1)

        qa = _rmsnorm(qa.reshape(B, S, N_HEADS_A, HEAD_DIM), q_norm_g[l])
        ka = _rmsnorm(ka.reshape(B, S, N_KV_A, HEAD_DIM), k_norm_g[l])
        va = va.reshape(B, S, N_KV_A, HEAD_DIM)
        qa = _apply_axial_rope(qa, ang_row, ang_col)
        ka = _apply_axial_rope(ka, ang_row, ang_col)
        out_a = _gqa_dense_blocked(qa, ka, va)

        qb = _apply_rope(qb.reshape(B, S, N_HEADS_B, HEAD_DIM), ang_1d)
        kb = _apply_rope(kb.reshape(B, S, N_HEADS_B, HEAD_DIM), ang_1d)
        vb = vb.reshape(B, S, N_HEADS_B, HEAD_DIM)
        out_b = _dilated_mixture(qb, kb, vb)

        heads = jnp.concatenate([out_a.reshape(B, S, A_Q), out_b.reshape(B, S, B_QKV)], axis=-1)
        x = x + _rmsnorm(heads @ w_out[l], mix_post_g[l])

        f = _swiglu(_rmsnorm(x, ffn2_pre_g[l]), ffn2_w_gate[l], ffn2_w_up[l], ffn2_w_down[l])
        x = x + 0.5 * _rmsnorm(f, ffn2_post_g[l])
    return x
```

```python
import functools

import jax
import jax.numpy as jnp
from jax import lax
from jax.experimental import pallas as pl
from jax.experimental.pallas import tpu as pltpu

D_MODEL = 1024
HEAD_DIM = 64
N_HEADS_A = 8
N_KV_A = 2
N_HEADS_B = 8
DILATED_CONFIGS = ((128, 1), (512, 4), (2048, 16))
GRID_W = 64
ROPE_THETA = 10000.0
D_FF = 2816
EPS = 1e-6
NEG_INF = -1e30

A_Q = N_HEADS_A * HEAD_DIM
A_KV = N_KV_A * HEAD_DIM
B_QKV = N_HEADS_B * HEAD_DIM
QKV_COLS = A_Q + 2 * A_KV + 3 * B_QKV
SCALE = HEAD_DIM ** -0.5

LANES = 128
FF_CHUNKS = ((0, 1024), (1024, 2048), (2048, 2816))
TM = 512
TQ_A = 256
TK_A = 512
TQ_B = 512
SUB_B = 128
SPAN_B = 64
VMEM_LIMIT = 52 * 1024 * 1024

BF16 = jnp.bfloat16
F32 = jnp.float32


def _dot(a, b):
    return jnp.dot(a, b, preferred_element_type=F32)


def _dot_nt(a, b):
    return lax.dot_general(a, b, (((1,), (1,)), ((), ())), preferred_element_type=F32)


def _rms(x, g):
    ms = jnp.mean(x * x, axis=-1, keepdims=True)
    return x * lax.rsqrt(ms + EPS) * g


def _swiglu_half_step(x, pre_g, post_g, wg_ref, wu_ref, wd_ref):
    h = _rms(x, pre_g).astype(BF16)
    f = None
    for lo, hi in FF_CHUNKS:
        g = _dot(h, wg_ref[:, lo:hi])
        u = _dot(h, wu_ref[:, lo:hi])
        a = (g / (1.0 + jnp.exp(-g)) * u).astype(BF16)
        part = _dot(a, wd_ref[lo:hi, :])
        f = part if f is None else f + part
    return x + 0.5 * _rms(f, post_g)


def _ffn_kernel(x_ref, pre_ref, post_ref, wg_ref, wu_ref, wd_ref, o_ref):
    o_ref[...] = _swiglu_half_step(x_ref[...], pre_ref[...], post_ref[...],
                                   wg_ref, wu_ref, wd_ref)


def _resident(shape):
    nd = len(shape)
    return pl.BlockSpec(shape, lambda *_: (0,) * nd, pipeline_mode=pl.Buffered(1))


def _ffn_call(x2d, pre_g, post_g, wg, wu, wd):
    n = x2d.shape[0]
    row = pl.BlockSpec((TM, D_MODEL), lambda i: (i, 0))
    return pl.pallas_call(
        _ffn_kernel,
        grid=(n // TM,),
        in_specs=[row, _resident((1, D_MODEL)), _resident((1, D_MODEL)),
                  _resident(wg.shape), _resident(wu.shape), _resident(wd.shape)],
        out_specs=row,
        out_shape=jax.ShapeDtypeStruct(x2d.shape, F32),
        compiler_params=pltpu.CompilerParams(
            dimension_semantics=("arbitrary",), vmem_limit_bytes=VMEM_LIMIT),
        name="ffn1",
    )(x2d, pre_g, post_g, wg, wu, wd)


def _rope(c, cos, sin_signed, half):
    lane = lax.broadcasted_iota(jnp.int32, c.shape, 1)
    first = (lane % (2 * half)) < half
    partner = jnp.where(first, pltpu.roll(c, LANES - half, 1), pltpu.roll(c, half, 1))
    return c * cos + partner * sin_signed


def _head_rms(c, seg_ref, g):
    sq = c * c
    hi = sq.astype(BF16)
    lo = (sq - hi.astype(F32)).astype(BF16)
    ms = _dot(hi, seg_ref[...]) + _dot(lo, seg_ref[...])
    return c * lax.rsqrt(ms + EPS) * g


def _dup_halves(c):
    lane = lax.broadcasted_iota(jnp.int32, c.shape, 1)
    low = lane < HEAD_DIM
    r = pltpu.roll(c, HEAD_DIM, 1)
    return jnp.where(low, c, r), jnp.where(low, r, c)


def _qkv_kernel(x_ref, pre_ref, w_ref, seg_ref, qg_ref, kg_ref,
                cosa_ref, sina_ref, cosb_ref, sinb_ref,
                qa_ref, ka_ref, va_ref, qb_ref, kb_ref, vb_ref):
    h = _rms(x_ref[...], pre_ref[...]).astype(BF16)
    cosa, sina = cosa_ref[...], sina_ref[...]
    cosb, sinb = cosb_ref[...], sinb_ref[...]
    qg, kg = qg_ref[...], kg_ref[...]

    for c in range(A_Q // LANES):
        q = _dot(h, w_ref[:, c * LANES:(c + 1) * LANES])
        q = _rope(_head_rms(q, seg_ref, qg), cosa, sina, HEAD_DIM // 4) * SCALE
        qa_ref[:, c * LANES:(c + 1) * LANES] = q.astype(BF16)

    k = _dot(h, w_ref[:, A_Q:A_Q + A_KV])
    k = _rope(_head_rms(k, seg_ref, kg), cosa, sina, HEAD_DIM // 4)
    k0, k1 = _dup_halves(k)
    ka_ref[0] = k0.astype(BF16)
    ka_ref[1] = k1.astype(BF16)

    v = _dot(h, w_ref[:, A_Q + A_KV:A_Q + 2 * A_KV])
    v0, v1 = _dup_halves(v)
    va_ref[0] = v0.astype(BF16)
    va_ref[1] = v1.astype(BF16)

    base = A_Q + 2 * A_KV
    for c in range(B_QKV // LANES):
        sl = slice(c * LANES, (c + 1) * LANES)
        q = _dot(h, w_ref[:, base + c * LANES:base + (c + 1) * LANES])
        qb_ref[:, sl] = (_rope(q, cosb, sinb, HEAD_DIM // 2) * SCALE).astype(BF16)
        k = _dot(h, w_ref[:, base + B_QKV + c * LANES:base + B_QKV + (c + 1) * LANES])
        kb_ref[:, sl] = _rope(k, cosb, sinb, HEAD_DIM // 2).astype(BF16)
        v = _dot(h, w_ref[:, base + 2 * B_QKV + c * LANES:base + 2 * B_QKV + (c + 1) * LANES])
        vb_ref[:, sl] = v.astype(BF16)


def _qkv_call(x1, pre_g, w, seg, qg, kg, cosa, sina, cosb, sinb):
    B, S, _ = x1.shape
    tab = pl.BlockSpec((TM, LANES), lambda i, b: (i, 0))
    wide = pl.BlockSpec((None, TM, B_QKV), lambda i, b: (b, i, 0))
    dup = pl.BlockSpec((None, N_KV_A, TM, LANES), lambda i, b: (b, 0, i, 0))
    wide_shape = jax.ShapeDtypeStruct((B, S, B_QKV), BF16)
    dup_shape = jax.ShapeDtypeStruct((B, N_KV_A, S, LANES), BF16)
    return pl.pallas_call(
        _qkv_kernel,
        grid=(S // TM, B),
        in_specs=[pl.BlockSpec((None, TM, D_MODEL), lambda i, b: (b, i, 0)),
                  _resident((1, D_MODEL)), _resident(w.shape), _resident(seg.shape),
                  _resident((1, LANES)), _resident((1, LANES)), tab, tab, tab, tab],
        out_specs=[wide, dup, dup, wide, wide, wide],
        out_shape=[wide_shape, dup_shape, dup_shape, wide_shape, wide_shape, wide_shape],
        compiler_params=pltpu.CompilerParams(
            dimension_semantics=("arbitrary", "arbitrary"), vmem_limit_bytes=VMEM_LIMIT),
        name="qkv",
    )(x1, pre_g, w, seg, qg, kg, cosa, sina, cosb, sinb)


def _attn_a_kernel(q_ref, k_ref, v_ref, o_ref, m_sc, l_sc, acc_sc):
    tq = q_ref.shape[0]
    n_k = k_ref.shape[0] // TK_A
    low = lax.broadcasted_iota(jnp.int32, (tq, LANES), 1) < HEAD_DIM
    q = q_ref[...]
    parts = []
    for c in range(2):
        qc = q[:, c * LANES:(c + 1) * LANES]
        parts.append(jnp.where(low, qc, jnp.zeros_like(qc)))
        parts.append(jnp.where(low, jnp.zeros_like(qc), qc))
    qs = jnp.concatenate(parts, axis=0)

    m_sc[...] = jnp.full(m_sc.shape, NEG_INF, F32)
    l_sc[...] = jnp.zeros(l_sc.shape, F32)
    acc_sc[...] = jnp.zeros(acc_sc.shape, F32)

    def body(kb, carry):
        rows = pl.ds(pl.multiple_of(kb * TK_A, TK_A), TK_A)
        s = _dot_nt(qs, k_ref[rows, :])
        m_old = m_sc[...]
        m_new = jnp.maximum(m_old, jnp.max(s, axis=1, keepdims=True))
        alpha = jnp.exp(m_old - m_new)
        p = jnp.exp(s - m_new)
        l_sc[...] = alpha * l_sc[...] + jnp.sum(p, axis=1, keepdims=True)
        acc_sc[...] = alpha * acc_sc[...] + _dot(p.astype(BF16), v_ref[rows, :])
        m_sc[...] = m_new
        return carry

    lax.fori_loop(0, n_k, body, 0)
    o = acc_sc[...] / l_sc[...]
    for c in range(2):
        pair = jnp.where(low, o[(2 * c) * tq:(2 * c + 1) * tq], o[(2 * c + 1) * tq:(2 * c + 2) * tq])
        o_ref[:, c * LANES:(c + 1) * LANES] = pair.astype(o_ref.dtype)


def _attn_a_call(qa, ka, va):
    B, S, _ = qa.shape
    qcols = A_Q // N_KV_A
    qspec = pl.BlockSpec((None, TQ_A, qcols), lambda b, g, i: (b, i, g))
    kvspec = pl.BlockSpec((None, None, S, LANES), lambda b, g, i: (b, g, 0, 0))
    rows = 4 * TQ_A
    return pl.pallas_call(
        _attn_a_kernel,
        grid=(B, N_KV_A, S // TQ_A),
        in_specs=[qspec, kvspec, kvspec],
        out_specs=qspec,
        out_shape=jax.ShapeDtypeStruct((B, S, A_Q), BF16),
        scratch_shapes=[pltpu.VMEM((rows, 1), F32), pltpu.VMEM((rows, 1), F32),
                        pltpu.VMEM((rows, LANES), F32)],
        compiler_params=pltpu.CompilerParams(
            dimension_semantics=("arbitrary", "arbitrary", "arbitrary"),
            vmem_limit_bytes=VMEM_LIMIT),
        name="attn_a",
    )(qa, ka, va)


def _attn_b_kernel(q_ref, kp_ref, kc_ref, kn_ref, vp_ref, vc_ref, vn_ref,
                   o_ref, st_ref, kbuf, vbuf, *, seq_len):
    j = pl.program_id(2)
    kbuf[0:SPAN_B, :] = kp_ref[...]
    kbuf[SPAN_B:SPAN_B + TQ_B, :] = kc_ref[...]
    kbuf[SPAN_B + TQ_B:, :] = kn_ref[...]
    vbuf[0:SPAN_B, :] = vp_ref[...]
    vbuf[SPAN_B:SPAN_B + TQ_B, :] = vc_ref[...]
    vbuf[SPAN_B + TQ_B:, :] = vn_ref[...]

    n_keys = SUB_B + 2 * SPAN_B
    a_idx = lax.broadcasted_iota(jnp.int32, (SUB_B, n_keys), 0)
    c_idx = lax.broadcasted_iota(jnp.int32, (SUB_B, n_keys), 1)
    band = jnp.abs(c_idx - SPAN_B - a_idx) <= SPAN_B
    lane = lax.broadcasted_iota(jnp.int32, (SUB_B, LANES), 1)
    low = lane < HEAD_DIM

    def body(i, carry):
        q0 = pl.multiple_of(i * SUB_B, SUB_B)
        kpos = j * TQ_B + q0 - SPAN_B + c_idx
        valid = band & (kpos >= 0) & (kpos < seq_len)
        bias = jnp.where(valid, 0.0, NEG_INF).astype(F32)
        st = jnp.zeros((SUB_B, LANES), F32)
        for hp in range(N_HEADS_B // 2):
            cols = slice(hp * LANES, (hp + 1) * LANES)
            qp = q_ref[pl.ds(q0, SUB_B), cols]
            kw = kbuf[pl.ds(q0, n_keys), cols]
            vw = vbuf[pl.ds(q0, n_keys), cols]
            outs = []
            for half in range(2):
                keep = low if half == 0 else jnp.logical_not(low)
                qh = jnp.where(keep, qp, jnp.zeros_like(qp))
                s = _dot_nt(qh, kw) + bias
                m = jnp.max(s, axis=1, keepdims=True)
                p = jnp.exp(s - m)
                l = jnp.sum(p, axis=1, keepdims=True)
                outs.append(_dot(p.astype(BF16), vw))
                head = 2 * hp + half
                st = jnp.where(lane == head, m, st)
                st = jnp.where(lane == N_HEADS_B + head, l, st)
            o_ref[pl.ds(q0, SUB_B), cols] = jnp.where(low, outs[0], outs[1]).astype(o_ref.dtype)
        st_ref[pl.ds(q0, SUB_B), :] = st
        return carry

    lax.fori_loop(0, TQ_B // SUB_B, body, 0)


def _attn_b_call(qb, kb, vb, dilation):
    B, S, C = qb.shape
    L = S // dilation
    nh = L // SPAN_B
    per = TQ_B // SPAN_B
    view = lambda a: a.reshape(B, L, dilation * C)
    cur = pl.BlockSpec((None, TQ_B, C), lambda b, r, j: (b, j, r))
    prev = pl.BlockSpec((None, SPAN_B, C), lambda b, r, j: (b, jnp.maximum(j * per - 1, 0), r))
    nxt = pl.BlockSpec((None, SPAN_B, C), lambda b, r, j: (b, jnp.minimum((j + 1) * per, nh - 1), r))
    st_spec = pl.BlockSpec((None, TQ_B, LANES), lambda b, r, j: (b, j, r))
    o, st = pl.pallas_call(
        functools.partial(_attn_b_kernel, seq_len=L),
        grid=(B, dilation, L // TQ_B),
        in_specs=[cur, prev, cur, nxt, prev, cur, nxt],
        out_specs=[cur, st_spec],
        out_shape=[jax.ShapeDtypeStruct((B, L, dilation * C), BF16),
                   jax.ShapeDtypeStruct((B, L, dilation * LANES), F32)],
        scratch_shapes=[pltpu.VMEM((TQ_B + 2 * SPAN_B, C), BF16),
                        pltpu.VMEM((TQ_B + 2 * SPAN_B, C), BF16)],
        compiler_params=pltpu.CompilerParams(
            dimension_semantics=("arbitrary", "arbitrary", "arbitrary"),
            vmem_limit_bytes=VMEM_LIMIT),
        name=f"attn_b_d{dilation}",
    )(view(qb), view(kb), view(kb), view(kb), view(vb), view(vb), view(vb))
    return o.reshape(B, S, C), st.reshape(B, S, LANES)


def _out_ffn_kernel(x_ref, ha_ref, o1_ref, o2_ref, o3_ref, s1_ref, s2_ref, s3_ref,
                    wo_ref, mixg_ref, pre_ref, post_ref, wg_ref, wu_ref, wd_ref, y_ref):
    low = lax.broadcasted_iota(jnp.int32, (TM, LANES), 1) < HEAD_DIM
    stats = (s1_ref[...], s2_ref[...], s3_ref[...])
    o_refs = (o1_ref, o2_ref, o3_ref)

    def head_weights(h):
        ms = [s[:, h:h + 1] for s in stats]
        ls = [s[:, N_HEADS_B + h:N_HEADS_B + h + 1] for s in stats]
        m_all = jnp.maximum(jnp.maximum(ms[0], ms[1]), ms[2])
        es = [jnp.exp(m - m_all) for m in ms]
        den = es[0] * ls[0] + es[1] * ls[1] + es[2] * ls[2]
        return [e / den for e in es]

    pairs = []
    for hp in range(N_HEADS_B // 2):
        cols = slice(hp * LANES, (hp + 1) * LANES)
        w_lo = head_weights(2 * hp)
        w_hi = head_weights(2 * hp + 1)
        acc = None
        for d in range(3):
            w = jnp.where(low, w_lo[d], w_hi[d])
            term = w * o_refs[d][:, cols].astype(F32)
            acc = term if acc is None else acc + term
        pairs.append(acc.astype(BF16))
    heads_b = jnp.concatenate(pairs, axis=1)

    mixed = _dot(ha_ref[...], wo_ref[0:A_Q, :]) + _dot(heads_b, wo_ref[A_Q:, :])
    x2 = x_ref[...] + _rms(mixed, mixg_ref[...])
    y_ref[...] = _swiglu_half_step(x2, pre_ref[...], post_ref[...], wg_ref, wu_ref, wd_ref)


def _out_ffn_call(x1, heads_a, parts, wo, mix_g, pre_g, post_g, wg, wu, wd):
    n = x1.shape[0]
    row = lambda c: pl.BlockSpec((TM, c), lambda i: (i, 0))
    (o1, s1), (o2, s2), (o3, s3) = parts
    return pl.pallas_call(
        _out_ffn_kernel,
        grid=(n // TM,),
        in_specs=[row(D_MODEL), row(A_Q), row(B_QKV), row(B_QKV), row(B_QKV),
                  row(LANES), row(LANES), row(LANES),
                  _resident(wo.shape), _resident((1, D_MODEL)), _resident((1, D_MODEL)),
                  _resident((1, D_MODEL)), _resident(wg.shape), _resident(wu.shape),
                  _resident(wd.shape)],
        out_specs=row(D_MODEL),
        out_shape=jax.ShapeDtypeStruct((n, D_MODEL), F32),
        compiler_params=pltpu.CompilerParams(
            dimension_semantics=("arbitrary",), vmem_limit_bytes=VMEM_LIMIT),
        name="out_ffn2",
    )(x1, heads_a, o1, o2, o3, s1, s2, s3, wo, mix_g, pre_g, post_g, wg, wu, wd)


def _rope_tables(seq):
    pos = jnp.arange(seq, dtype=jnp.int32)
    row = (pos // GRID_W).astype(F32)[:, None]
    col = (pos % GRID_W).astype(F32)[:, None]
    dim_a = HEAD_DIM // 2
    fa = ROPE_THETA ** (-jnp.arange(0, dim_a, 2, dtype=F32) / dim_a)
    fb = ROPE_THETA ** (-jnp.arange(0, HEAD_DIM, 2, dtype=F32) / HEAD_DIM)
    ar, ac = row * fa[None, :], col * fa[None, :]
    ab = pos.astype(F32)[:, None] * fb[None, :]
    cos_a = jnp.concatenate([jnp.cos(ar), jnp.cos(ar), jnp.cos(ac), jnp.cos(ac)], axis=-1)
    sin_a = jnp.concatenate([-jnp.sin(ar), jnp.sin(ar), -jnp.sin(ac), jnp.sin(ac)], axis=-1)
    cos_b = jnp.concatenate([jnp.cos(ab), jnp.cos(ab)], axis=-1)
    sin_b = jnp.concatenate([-jnp.sin(ab), jnp.sin(ab)], axis=-1)
    two = lambda t: jnp.tile(t, (1, LANES // HEAD_DIM))
    return two(cos_a), two(sin_a), two(cos_b), two(sin_b)


def _layer(x, ffn1_pre_g, ffn1_post_g, ffn1_w_gate, ffn1_w_up, ffn1_w_down,
           mix_pre_g, mix_post_g, w_qkv, q_norm_g, k_norm_g, w_out,
           ffn2_pre_g, ffn2_post_g, ffn2_w_gate, ffn2_w_up, ffn2_w_down, tables, seg):
    B, S, D = x.shape
    vec = lambda g: g.reshape(1, -1).astype(F32)
    two = lambda g: jnp.tile(vec(g), (1, LANES // HEAD_DIM))
    bf = lambda w: w.astype(BF16)

    x1 = _ffn_call(x.reshape(B * S, D), vec(ffn1_pre_g), vec(ffn1_post_g),
                   bf(ffn1_w_gate), bf(ffn1_w_up), bf(ffn1_w_down))
    qa, ka, va, qb, kb, vb = _qkv_call(x1.reshape(B, S, D), vec(mix_pre_g), bf(w_qkv), seg,
                                       two(q_norm_g), two(k_norm_g), *tables)
    heads_a = _attn_a_call(qa, ka, va)
    parts = [_attn_b_call(qb, kb, vb, d) for (_, d) in DILATED_CONFIGS]
    flat = lambda a: a.reshape(B * S, a.shape[-1])
    y = _out_ffn_call(x1, flat(heads_a), [(flat(o), flat(s)) for o, s in parts],
                      bf(w_out), vec(mix_post_g), vec(ffn2_pre_g), vec(ffn2_post_g),
                      bf(ffn2_w_gate), bf(ffn2_w_up), bf(ffn2_w_down))
    return y.reshape(B, S, D)


def kernel(x, ffn1_pre_g, ffn1_post_g, ffn1_w_gate, ffn1_w_up, ffn1_w_down, mix_pre_g, mix_post_g, w_qkv, q_norm_g, k_norm_g, w_out, ffn2_pre_g, ffn2_post_g, ffn2_w_gate, ffn2_w_up, ffn2_w_down):
    assert all(w // 2 // d == SPAN_B for w, d in DILATED_CONFIGS)
    S = x.shape[1]
    tables = _rope_tables(S)
    head_of_lane = jnp.arange(LANES) // HEAD_DIM
    seg = (head_of_lane[:, None] == head_of_lane[None, :]).astype(BF16) / HEAD_DIM
    params = (ffn1_pre_g, ffn1_post_g, ffn1_w_gate, ffn1_w_up, ffn1_w_down, mix_pre_g, mix_post_g,
              w_qkv, q_norm_g, k_norm_g, w_out, ffn2_pre_g, ffn2_post_g, ffn2_w_gate, ffn2_w_up,
              ffn2_w_down)
    for l in range(ffn1_pre_g.shape[0]):
        x = _layer(x, *(p[l] for p in params), tables, seg)
    return x
```

```python
import functools

import jax
import jax.numpy as jnp
from jax import lax
from jax.experimental import pallas as pl
from jax.experimental.pallas import tpu as pltpu

D_MODEL = 1024
HEAD_DIM = 64
N_HEADS_A = 8
N_KV_A = 2
N_HEADS_B = 8
DILATED_CONFIGS = ((128, 1), (512, 4), (2048, 16))
GRID_W = 64
ROPE_THETA = 10000.0
D_FF = 2816
EPS = 1e-6
NEG_INF = -1e30

A_Q = N_HEADS_A * HEAD_DIM
A_KV = N_KV_A * HEAD_DIM
B_QKV = N_HEADS_B * HEAD_DIM
QKV_COLS = A_Q + 2 * A_KV + 3 * B_QKV
SCALE = HEAD_DIM ** -0.5
LOG2E = 1.4426950408889634

LANES = 128
FF_CHUNKS = ((0, 1024), (1024, 2048), (2048, 2816))
TM = 512
TQ_A = 256
TK_A = 512
TQ_B = 512
SUB_B = 128
SPAN_B = 64
VMEM_LIMIT = 52 * 1024 * 1024

BF16 = jnp.bfloat16
F32 = jnp.float32


def _dot(a, b):
    return jnp.dot(a, b, preferred_element_type=F32)


def _dot_nt(a, b):
    return lax.dot_general(a, b, (((1,), (1,)), ((), ())), preferred_element_type=F32)


def _rms(x, g):
    ms = jnp.mean(x * x, axis=-1, keepdims=True)
    return x * lax.rsqrt(ms + EPS) * g


def _swiglu_half_step(x, pre_g, post_g, wg_ref, wu_ref, wd_ref):
    h = _rms(x, pre_g).astype(BF16)
    f = None
    for lo, hi in FF_CHUNKS:
        g = _dot(h, wg_ref[:, lo:hi])
        u = _dot(h, wu_ref[:, lo:hi])
        a = (g / (1.0 + jnp.exp(-g)) * u).astype(BF16)
        part = _dot(a, wd_ref[lo:hi, :])
        f = part if f is None else f + part
    return x + 0.5 * _rms(f, post_g)


def _ffn_kernel(x_ref, pre_ref, post_ref, wg_ref, wu_ref, wd_ref, o_ref):
    o_ref[...] = _swiglu_half_step(x_ref[...], pre_ref[...], post_ref[...],
                                   wg_ref, wu_ref, wd_ref)


def _resident(shape):
    nd = len(shape)
    return pl.BlockSpec(shape, lambda *_: (0,) * nd, pipeline_mode=pl.Buffered(1))


def _ffn_call(x2d, pre_g, post_g, wg, wu, wd):
    n = x2d.shape[0]
    row = pl.BlockSpec((TM, D_MODEL), lambda i: (i, 0))
    return pl.pallas_call(
        _ffn_kernel,
        grid=(n // TM,),
        in_specs=[row, _resident((1, D_MODEL)), _resident((1, D_MODEL)),
                  _resident(wg.shape), _resident(wu.shape), _resident(wd.shape)],
        out_specs=row,
        out_shape=jax.ShapeDtypeStruct(x2d.shape, F32),
        compiler_params=pltpu.CompilerParams(
            dimension_semantics=("arbitrary",), vmem_limit_bytes=VMEM_LIMIT),
        name="ffn1",
    )(x2d, pre_g, post_g, wg, wu, wd)


def _rope(c, cos, sin_signed, half):
    lane = lax.broadcasted_iota(jnp.int32, c.shape, 1)
    first = (lane % (2 * half)) < half
    partner = jnp.where(first, pltpu.roll(c, LANES - half, 1), pltpu.roll(c, half, 1))
    return c * cos + partner * sin_signed


def _head_rms(c, seg_ref, g):
    sq = c * c
    hi = sq.astype(BF16)
    lo = (sq - hi.astype(F32)).astype(BF16)
    ms = _dot(hi, seg_ref[...]) + _dot(lo, seg_ref[...])
    return c * lax.rsqrt(ms + EPS) * g


def _dup_halves(c):
    lane = lax.broadcasted_iota(jnp.int32, c.shape, 1)
    low = lane < HEAD_DIM
    r = pltpu.roll(c, HEAD_DIM, 1)
    return jnp.where(low, c, r), jnp.where(low, r, c)


def _qkv_kernel(x_ref, pre_ref, w_ref, seg_ref, qg_ref, kg_ref,
                cosa_ref, sina_ref, cosb_ref, sinb_ref,
                qa_ref, ka_ref, va_ref, qb_ref, kb_ref, vb_ref):
    h = _rms(x_ref[...], pre_ref[...]).astype(BF16)
    cosa, sina = cosa_ref[...], sina_ref[...]
    cosb, sinb = cosb_ref[...], sinb_ref[...]
    qg, kg = qg_ref[...], kg_ref[...]

    for c in range(A_Q // LANES):
        q = _dot(h, w_ref[:, c * LANES:(c + 1) * LANES])
        q = _rope(_head_rms(q, seg_ref, qg), cosa, sina, HEAD_DIM // 4) * (SCALE * LOG2E)
        qa_ref[:, c * LANES:(c + 1) * LANES] = q.astype(BF16)

    k = _dot(h, w_ref[:, A_Q:A_Q + A_KV])
    k = _rope(_head_rms(k, seg_ref, kg), cosa, sina, HEAD_DIM // 4)
    k0, k1 = _dup_halves(k)
    ka_ref[0] = k0.astype(BF16)
    ka_ref[1] = k1.astype(BF16)

    v = _dot(h, w_ref[:, A_Q + A_KV:A_Q + 2 * A_KV])
    vt = v.T.astype(BF16)
    va_ref[0] = vt[0:HEAD_DIM, :]
    va_ref[1] = vt[HEAD_DIM:, :]

    base = A_Q + 2 * A_KV
    for c in range(B_QKV // LANES):
        sl = slice(c * LANES, (c + 1) * LANES)
        q = _dot(h, w_ref[:, base + c * LANES:base + (c + 1) * LANES])
        qb_ref[:, sl] = (_rope(q, cosb, sinb, HEAD_DIM // 2) * SCALE).astype(BF16)
        k = _dot(h, w_ref[:, base + B_QKV + c * LANES:base + B_QKV + (c + 1) * LANES])
        kb_ref[:, sl] = _rope(k, cosb, sinb, HEAD_DIM // 2).astype(BF16)
        v = _dot(h, w_ref[:, base + 2 * B_QKV + c * LANES:base + 2 * B_QKV + (c + 1) * LANES])
        vb_ref[:, sl] = v.astype(BF16)


def _qkv_call(x1, pre_g, w, seg, qg, kg, cosa, sina, cosb, sinb):
    B, S, _ = x1.shape
    tab = pl.BlockSpec((TM, LANES), lambda i, b: (i, 0))
    wide = pl.BlockSpec((None, TM, B_QKV), lambda i, b: (b, i, 0))
    dup = pl.BlockSpec((None, N_KV_A, TM, LANES), lambda i, b: (b, 0, i, 0))
    vt = pl.BlockSpec((None, N_KV_A, None, HEAD_DIM, TM), lambda i, b: (b, 0, i, 0, 0))
    wide_shape = jax.ShapeDtypeStruct((B, S, B_QKV), BF16)
    dup_shape = jax.ShapeDtypeStruct((B, N_KV_A, S, LANES), BF16)
    vt_shape = jax.ShapeDtypeStruct((B, N_KV_A, S // TM, HEAD_DIM, TM), BF16)
    return pl.pallas_call(
        _qkv_kernel,
        grid=(S // TM, B),
        in_specs=[pl.BlockSpec((None, TM, D_MODEL), lambda i, b: (b, i, 0)),
                  _resident((1, D_MODEL)), _resident(w.shape), _resident(seg.shape),
                  _resident((1, LANES)), _resident((1, LANES)), tab, tab, tab, tab],
        out_specs=[wide, dup, vt, wide, wide, wide],
        out_shape=[wide_shape, dup_shape, vt_shape, wide_shape, wide_shape, wide_shape],
        compiler_params=pltpu.CompilerParams(
            dimension_semantics=("arbitrary", "arbitrary"), vmem_limit_bytes=VMEM_LIMIT),
        name="qkv",
    )(x1, pre_g, w, seg, qg, kg, cosa, sina, cosb, sinb)


def _attn_a_kernel(q_ref, k_ref, vt_ref, o_ref, qs_sc, m_sc, l_sc, acc_sc):
    tq = q_ref.shape[0]
    n_heads = qs_sc.shape[0]
    low = lax.broadcasted_iota(jnp.int32, (tq, LANES), 1) < HEAD_DIM
    for h in range(n_heads):
        qc = q_ref[:, (h // 2) * LANES:(h // 2 + 1) * LANES]
        keep = low if h % 2 == 0 else jnp.logical_not(low)
        qs_sc[h] = jnp.where(keep, qc, jnp.zeros_like(qc))

    m_sc[...] = jnp.full(m_sc.shape, NEG_INF, F32)
    l_sc[...] = jnp.zeros(l_sc.shape, F32)
    acc_sc[...] = jnp.zeros(acc_sc.shape, F32)

    def body(kb, carry):
        k = k_ref[pl.ds(pl.multiple_of(kb * TK_A, TK_A), TK_A), :]
        vt = vt_ref[kb]
        for h in range(n_heads):
            s = _dot_nt(k, qs_sc[h])
            m_old = m_sc[h]
            m_new = jnp.maximum(m_old, jnp.max(s, axis=0, keepdims=True))
            alpha = jnp.exp2(m_old - m_new)
            p = jnp.exp2(s - m_new)
            l_sc[h] = alpha * l_sc[h] + jnp.sum(p, axis=0, keepdims=True)
            acc_sc[h] = alpha * acc_sc[h] + _dot(vt, p.astype(BF16))
            m_sc[h] = m_new
        return carry

    lax.fori_loop(0, vt_ref.shape[0], body, 0)
    for c in range(n_heads // 2):
        pair = jnp.concatenate([acc_sc[2 * c] / l_sc[2 * c],
                                acc_sc[2 * c + 1] / l_sc[2 * c + 1]], axis=0)
        o_ref[:, c * LANES:(c + 1) * LANES] = pair.T.astype(o_ref.dtype)


def _attn_a_call(qa, ka, vta):
    B, S, _ = qa.shape
    qcols = A_Q // N_KV_A
    group = N_HEADS_A // N_KV_A
    qspec = pl.BlockSpec((None, TQ_A, qcols), lambda b, g, i: (b, i, g))
    kspec = pl.BlockSpec((None, None, S, LANES), lambda b, g, i: (b, g, 0, 0))
    vspec = pl.BlockSpec((None, None, S // TK_A, HEAD_DIM, TK_A), lambda b, g, i: (b, g, 0, 0, 0))
    return pl.pallas_call(
        _attn_a_kernel,
        grid=(B, N_KV_A, S // TQ_A),
        in_specs=[qspec, kspec, vspec],
        out_specs=qspec,
        out_shape=jax.ShapeDtypeStruct((B, S, A_Q), BF16),
        scratch_shapes=[pltpu.VMEM((group, TQ_A, LANES), BF16),
                        pltpu.VMEM((group, 1, TQ_A), F32), pltpu.VMEM((group, 1, TQ_A), F32),
                        pltpu.VMEM((group, HEAD_DIM, TQ_A), F32)],
        compiler_params=pltpu.CompilerParams(
            dimension_semantics=("arbitrary", "arbitrary", "arbitrary"),
            vmem_limit_bytes=VMEM_LIMIT),
        name="attn_a",
    )(qa, ka, vta)


def _attn_b_kernel(q_ref, kp_ref, kc_ref, kn_ref, vp_ref, vc_ref, vn_ref,
                   o_ref, st_ref, kbuf, vbuf, *, seq_len):
    j = pl.program_id(2)
    kbuf[0:SPAN_B, :] = kp_ref[...]
    kbuf[SPAN_B:SPAN_B + TQ_B, :] = kc_ref[...]
    kbuf[SPAN_B + TQ_B:, :] = kn_ref[...]
    vbuf[0:SPAN_B, :] = vp_ref[...]
    vbuf[SPAN_B:SPAN_B + TQ_B, :] = vc_ref[...]
    vbuf[SPAN_B + TQ_B:, :] = vn_ref[...]

    n_keys = SUB_B + 2 * SPAN_B
    a_idx = lax.broadcasted_iota(jnp.int32, (SUB_B, n_keys), 0)
    c_idx = lax.broadcasted_iota(jnp.int32, (SUB_B, n_keys), 1)
    band = jnp.abs(c_idx - SPAN_B - a_idx) <= SPAN_B
    lane = lax.broadcasted_iota(jnp.int32, (SUB_B, LANES), 1)
    low = lane < HEAD_DIM

    def body(i, carry):
        q0 = pl.multiple_of(i * SUB_B, SUB_B)
        kpos = j * TQ_B + q0 - SPAN_B + c_idx
        valid = band & (kpos >= 0) & (kpos < seq_len)
        bias = jnp.where(valid, 0.0, NEG_INF).astype(F32)
        st = jnp.zeros((SUB_B, LANES), F32)
        for hp in range(N_HEADS_B // 2):
            cols = slice(hp * LANES, (hp + 1) * LANES)
            qp = q_ref[pl.ds(q0, SUB_B), cols]
            kw = kbuf[pl.ds(q0, n_keys), cols]
            vw = vbuf[pl.ds(q0, n_keys), cols]
            outs = []
            for half in range(2):
                keep = low if half == 0 else jnp.logical_not(low)
                qh = jnp.where(keep, qp, jnp.zeros_like(qp))
                s = _dot_nt(qh, kw) + bias
                m = jnp.max(s, axis=1, keepdims=True)
                p = jnp.exp(s - m)
                l = jnp.sum(p, axis=1, keepdims=True)
                outs.append(_dot(p.astype(BF16), vw))
                head = 2 * hp + half
                st = jnp.where(lane == head, m, st)
                st = jnp.where(lane == N_HEADS_B + head, l, st)
            o_ref[pl.ds(q0, SUB_B), cols] = jnp.where(low, outs[0], outs[1]).astype(o_ref.dtype)
        st_ref[pl.ds(q0, SUB_B), :] = st
        return carry

    lax.fori_loop(0, TQ_B // SUB_B, body, 0)


def _attn_b_call(qb, kb, vb, dilation):
    B, S, C = qb.shape
    L = S // dilation
    nh = L // SPAN_B
    per = TQ_B // SPAN_B
    view = lambda a: a.reshape(B, L, dilation * C)
    cur = pl.BlockSpec((None, TQ_B, C), lambda b, r, j: (b, j, r))
    prev = pl.BlockSpec((None, SPAN_B, C), lambda b, r, j: (b, jnp.maximum(j * per - 1, 0), r))
    nxt = pl.BlockSpec((None, SPAN_B, C), lambda b, r, j: (b, jnp.minimum((j + 1) * per, nh - 1), r))
    st_spec = pl.BlockSpec((None, TQ_B, LANES), lambda b, r, j: (b, j, r))
    o, st = pl.pallas_call(
        functools.partial(_attn_b_kernel, seq_len=L),
        grid=(B, dilation, L // TQ_B),
        in_specs=[cur, prev, cur, nxt, prev, cur, nxt],
        out_specs=[cur, st_spec],
        out_shape=[jax.ShapeDtypeStruct((B, L, dilation * C), BF16),
                   jax.ShapeDtypeStruct((B, L, dilation * LANES), F32)],
        scratch_shapes=[pltpu.VMEM((TQ_B + 2 * SPAN_B, C), BF16),
                        pltpu.VMEM((TQ_B + 2 * SPAN_B, C), BF16)],
        compiler_params=pltpu.CompilerParams(
            dimension_semantics=("arbitrary", "arbitrary", "arbitrary"),
            vmem_limit_bytes=VMEM_LIMIT),
        name=f"attn_b_d{dilation}",
    )(view(qb), view(kb), view(kb), view(kb), view(vb), view(vb), view(vb))
    return o.reshape(B, S, C), st.reshape(B, S, LANES)


def _out_ffn_kernel(x_ref, ha_ref, o1_ref, o2_ref, o3_ref, s1_ref, s2_ref, s3_ref,
                    wo_ref, mixg_ref, pre_ref, post_ref, wg_ref, wu_ref, wd_ref, y_ref):
    low = lax.broadcasted_iota(jnp.int32, (TM, LANES), 1) < HEAD_DIM
    stats = (s1_ref[...], s2_ref[...], s3_ref[...])
    o_refs = (o1_ref, o2_ref, o3_ref)

    def head_weights(h):
        ms = [s[:, h:h + 1] for s in stats]
        ls = [s[:, N_HEADS_B + h:N_HEADS_B + h + 1] for s in stats]
        m_all = jnp.maximum(jnp.maximum(ms[0], ms[1]), ms[2])
        es = [jnp.exp(m - m_all) for m in ms]
        den = es[0] * ls[0] + es[1] * ls[1] + es[2] * ls[2]
        return [e / den for e in es]

    pairs = []
    for hp in range(N_HEADS_B // 2):
        cols = slice(hp * LANES, (hp + 1) * LANES)
        w_lo = head_weights(2 * hp)
        w_hi = head_weights(2 * hp + 1)
        acc = None
        for d in range(3):
            w = jnp.where(low, w_lo[d], w_hi[d])
            term = w * o_refs[d][:, cols].astype(F32)
            acc = term if acc is None else acc + term
        pairs.append(acc.astype(BF16))
    heads_b = jnp.concatenate(pairs, axis=1)

    mixed = _dot(ha_ref[...], wo_ref[0:A_Q, :]) + _dot(heads_b, wo_ref[A_Q:, :])
    x2 = x_ref[...] + _rms(mixed, mixg_ref[...])
    y_ref[...] = _swiglu_half_step(x2, pre_ref[...], post_ref[...], wg_ref, wu_ref, wd_ref)


def _out_ffn_call(x1, heads_a, parts, wo, mix_g, pre_g, post_g, wg, wu, wd):
    n = x1.shape[0]
    row = lambda c: pl.BlockSpec((TM, c), lambda i: (i, 0))
    (o1, s1), (o2, s2), (o3, s3) = parts
    return pl.pallas_call(
        _out_ffn_kernel,
        grid=(n // TM,),
        in_specs=[row(D_MODEL), row(A_Q), row(B_QKV), row(B_QKV), row(B_QKV),
                  row(LANES), row(LANES), row(LANES),
                  _resident(wo.shape), _resident((1, D_MODEL)), _resident((1, D_MODEL)),
                  _resident((1, D_MODEL)), _resident(wg.shape), _resident(wu.shape),
                  _resident(wd.shape)],
        out_specs=row(D_MODEL),
        out_shape=jax.ShapeDtypeStruct((n, D_MODEL), F32),
        compiler_params=pltpu.CompilerParams(
            dimension_semantics=("arbitrary",), vmem_limit_bytes=VMEM_LIMIT),
        name="out_ffn2",
    )(x1, heads_a, o1, o2, o3, s1, s2, s3, wo, mix_g, pre_g, post_g, wg, wu, wd)


def _rope_tables(seq):
    pos = jnp.arange(seq, dtype=jnp.int32)
    row = (pos // GRID_W).astype(F32)[:, None]
    col = (pos % GRID_W).astype(F32)[:, None]
    dim_a = HEAD_DIM // 2
    fa = ROPE_THETA ** (-jnp.arange(0, dim_a, 2, dtype=F32) / dim_a)
    fb = ROPE_THETA ** (-jnp.arange(0, HEAD_DIM, 2, dtype=F32) / HEAD_DIM)
    ar, ac = row * fa[None, :], col * fa[None, :]
    ab = pos.astype(F32)[:, None] * fb[None, :]
    cos_a = jnp.concatenate([jnp.cos(ar), jnp.cos(ar), jnp.cos(ac), jnp.cos(ac)], axis=-1)
    sin_a = jnp.concatenate([-jnp.sin(ar), jnp.sin(ar), -jnp.sin(ac), jnp.sin(ac)], axis=-1)
    cos_b = jnp.concatenate([jnp.cos(ab), jnp.cos(ab)], axis=-1)
    sin_b = jnp.concatenate([-jnp.sin(ab), jnp.sin(ab)], axis=-1)
    two = lambda t: jnp.tile(t, (1, LANES // HEAD_DIM))
    return two(cos_a), two(sin_a), two(cos_b), two(sin_b)


def _layer(x, ffn1_pre_g, ffn1_post_g, ffn1_w_gate, ffn1_w_up, ffn1_w_down,
           mix_pre_g, mix_post_g, w_qkv, q_norm_g, k_norm_g, w_out,
           ffn2_pre_g, ffn2_post_g, ffn2_w_gate, ffn2_w_up, ffn2_w_down, tables, seg):
    B, S, D = x.shape
    vec = lambda g: g.reshape(1, -1).astype(F32)
    two = lambda g: jnp.tile(vec(g), (1, LANES // HEAD_DIM))
    bf = lambda w: w.astype(BF16)

    x1 = _ffn_call(x.reshape(B * S, D), vec(ffn1_pre_g), vec(ffn1_post_g),
                   bf(ffn1_w_gate), bf(ffn1_w_up), bf(ffn1_w_down))
    qa, ka, va, qb, kb, vb = _qkv_call(x1.reshape(B, S, D), vec(mix_pre_g), bf(w_qkv), seg,
                                       two(q_norm_g), two(k_norm_g), *tables)
    heads_a = _attn_a_call(qa, ka, va)
    parts = [_attn_b_call(qb, kb, vb, d) for (_, d) in DILATED_CONFIGS]
    flat = lambda a: a.reshape(B * S, a.shape[-1])
    y = _out_ffn_call(x1, flat(heads_a), [(flat(o), flat(s)) for o, s in parts],
                      bf(w_out), vec(mix_post_g), vec(ffn2_pre_g), vec(ffn2_post_g),
                      bf(ffn2_w_gate), bf(ffn2_w_up), bf(ffn2_w_down))
    return y.reshape(B, S, D)


def kernel(x, ffn1_pre_g, ffn1_post_g, ffn1_w_gate, ffn1_w_up, ffn1_w_down, mix_pre_g, mix_post_g, w_qkv, q_norm_g, k_norm_g, w_out, ffn2_pre_g, ffn2_post_g, ffn2_w_gate, ffn2_w_up, ffn2_w_down):
    assert all(w // 2 // d == SPAN_B for w, d in DILATED_CONFIGS)
    S = x.shape[1]
    tables = _rope_tables(S)
    head_of_lane = jnp.arange(LANES) // HEAD_DIM
    seg = (head_of_lane[:, None] == head_of_lane[None, :]).astype(BF16) / HEAD_DIM
    params = (ffn1_pre_g, ffn1_post_g, ffn1_w_gate, ffn1_w_up, ffn1_w_down, mix_pre_g, mix_post_g,
              w_qkv, q_norm_g, k_norm_g, w_out, ffn2_pre_g, ffn2_post_g, ffn2_w_gate, ffn2_w_up,
              ffn2_w_down)
    for l in range(ffn1_pre_g.shape[0]):
        x = _layer(x, *(p[l] for p in params), tables, seg)
    return x
```

```python
import functools

import jax
import jax.numpy as jnp
from jax import lax
from jax.experimental import pallas as pl
from jax.experimental.pallas import tpu as pltpu

D_MODEL = 1024
HEAD_DIM = 64
N_HEADS_A = 8
N_KV_A = 2
N_HEADS_B = 8
DILATED_CONFIGS = ((128, 1), (512, 4), (2048, 16))
GRID_W = 64
ROPE_THETA = 10000.0
D_FF = 2816
EPS = 1e-6
NEG_INF = -1e30

A_Q = N_HEADS_A * HEAD_DIM
A_KV = N_KV_A * HEAD_DIM
B_QKV = N_HEADS_B * HEAD_DIM
QKV_COLS = A_Q + 2 * A_KV + 3 * B_QKV
SCALE = HEAD_DIM ** -0.5
LOG2E = 1.4426950408889634

LANES = 128
FF_CHUNKS = ((0, 1024), (1024, 2048), (2048, 2816))
TM = 512
TQ_A = 256
TK_A = 1024
SUB_A = 256
VT_ROWS = HEAD_DIM + 16
TQ_B = 512
SUB_B = 128
SPAN_B = 64
VMEM_LIMIT = 52 * 1024 * 1024

BF16 = jnp.bfloat16
F32 = jnp.float32


def _dot(a, b):
    return jnp.dot(a, b, preferred_element_type=F32)


def _dot_nt(a, b):
    return lax.dot_general(a, b, (((1,), (1,)), ((), ())), preferred_element_type=F32)


def _rms(x, g):
    ms = jnp.mean(x * x, axis=-1, keepdims=True)
    return x * lax.rsqrt(ms + EPS) * g


def _swiglu_half_step(x, pre_g, post_g, wg_ref, wu_ref, wd_ref):
    h = _rms(x, pre_g).astype(BF16)
    f = None
    for lo, hi in FF_CHUNKS:
        g = _dot(h, wg_ref[:, lo:hi])
        u = _dot(h, wu_ref[:, lo:hi])
        a = (g / (1.0 + jnp.exp(-g)) * u).astype(BF16)
        part = _dot(a, wd_ref[lo:hi, :])
        f = part if f is None else f + part
    return x + 0.5 * _rms(f, post_g)


def _ffn_kernel(x_ref, pre_ref, post_ref, wg_ref, wu_ref, wd_ref, o_ref):
    o_ref[...] = _swiglu_half_step(x_ref[...], pre_ref[...], post_ref[...],
                                   wg_ref, wu_ref, wd_ref)


def _resident(shape):
    nd = len(shape)
    return pl.BlockSpec(shape, lambda *_: (0,) * nd, pipeline_mode=pl.Buffered(1))


def _ffn_call(x2d, pre_g, post_g, wg, wu, wd):
    n = x2d.shape[0]
    row = pl.BlockSpec((TM, D_MODEL), lambda i: (i, 0))
    return pl.pallas_call(
        _ffn_kernel,
        grid=(n // TM,),
        in_specs=[row, _resident((1, D_MODEL)), _resident((1, D_MODEL)),
                  _resident(wg.shape), _resident(wu.shape), _resident(wd.shape)],
        out_specs=row,
        out_shape=jax.ShapeDtypeStruct(x2d.shape, F32),
        compiler_params=pltpu.CompilerParams(
            dimension_semantics=("arbitrary",), vmem_limit_bytes=VMEM_LIMIT),
        name="ffn1",
    )(x2d, pre_g, post_g, wg, wu, wd)


def _rope(c, cos, sin_signed, half):
    lane = lax.broadcasted_iota(jnp.int32, c.shape, 1)
    first = (lane % (2 * half)) < half
    partner = jnp.where(first, pltpu.roll(c, LANES - half, 1), pltpu.roll(c, half, 1))
    return c * cos + partner * sin_signed


def _head_rms(c, seg_ref, g):
    sq = c * c
    hi = sq.astype(BF16)
    lo = (sq - hi.astype(F32)).astype(BF16)
    ms = _dot(hi, seg_ref[...]) + _dot(lo, seg_ref[...])
    return c * lax.rsqrt(ms + EPS) * g


def _dup_halves(c):
    lane = lax.broadcasted_iota(jnp.int32, c.shape, 1)
    low = lane < HEAD_DIM
    r = pltpu.roll(c, HEAD_DIM, 1)
    return jnp.where(low, c, r), jnp.where(low, r, c)


def _qkv_kernel(x_ref, pre_ref, w_ref, seg_ref, qg_ref, kg_ref,
                cosa_ref, sina_ref, cosb_ref, sinb_ref,
                qa_ref, ka_ref, va_ref, qb_ref, kb_ref, vb_ref):
    h = _rms(x_ref[...], pre_ref[...]).astype(BF16)
    cosa, sina = cosa_ref[...], sina_ref[...]
    cosb, sinb = cosb_ref[...], sinb_ref[...]
    qg, kg = qg_ref[...], kg_ref[...]

    for c in range(A_Q // LANES):
        q = _dot(h, w_ref[:, c * LANES:(c + 1) * LANES])
        q = _rope(_head_rms(q, seg_ref, qg), cosa, sina, HEAD_DIM // 4) * (SCALE * LOG2E)
        qa_ref[:, c * LANES:(c + 1) * LANES] = q.astype(BF16)

    k = _dot(h, w_ref[:, A_Q:A_Q + A_KV])
    k = _rope(_head_rms(k, seg_ref, kg), cosa, sina, HEAD_DIM // 4)
    k0, k1 = _dup_halves(k)
    ka_ref[0] = k0.astype(BF16)
    ka_ref[1] = k1.astype(BF16)

    v = _dot(h, w_ref[:, A_Q + A_KV:A_Q + 2 * A_KV])
    vt = v.T.astype(BF16)
    ones = jnp.ones((VT_ROWS - HEAD_DIM, TM), BF16)
    for g in range(N_KV_A):
        va_ref[g, 0:HEAD_DIM, :] = vt[g * HEAD_DIM:(g + 1) * HEAD_DIM, :]
        va_ref[g, HEAD_DIM:, :] = ones

    base = A_Q + 2 * A_KV
    for c in range(B_QKV // LANES):
        sl = slice(c * LANES, (c + 1) * LANES)
        q = _dot(h, w_ref[:, base + c * LANES:base + (c + 1) * LANES])
        qb_ref[:, sl] = (_rope(q, cosb, sinb, HEAD_DIM // 2) * SCALE).astype(BF16)
        k = _dot(h, w_ref[:, base + B_QKV + c * LANES:base + B_QKV + (c + 1) * LANES])
        kb_ref[:, sl] = _rope(k, cosb, sinb, HEAD_DIM // 2).astype(BF16)
        v = _dot(h, w_ref[:, base + 2 * B_QKV + c * LANES:base + 2 * B_QKV + (c + 1) * LANES])
        vb_ref[:, sl] = v.astype(BF16)


def _qkv_call(x1, pre_g, w, seg, qg, kg, cosa, sina, cosb, sinb):
    B, S, _ = x1.shape
    tab = pl.BlockSpec((TM, LANES), lambda i, b: (i, 0))
    wide = pl.BlockSpec((None, TM, B_QKV), lambda i, b: (b, i, 0))
    dup = pl.BlockSpec((None, N_KV_A, TM, LANES), lambda i, b: (b, 0, i, 0))
    per = TK_A // TM
    vt = pl.BlockSpec((None, N_KV_A, None, VT_ROWS, TM), lambda i, b: (b, 0, i // per, 0, i % per))
    wide_shape = jax.ShapeDtypeStruct((B, S, B_QKV), BF16)
    dup_shape = jax.ShapeDtypeStruct((B, N_KV_A, S, LANES), BF16)
    vt_shape = jax.ShapeDtypeStruct((B, N_KV_A, S // TK_A, VT_ROWS, TK_A), BF16)
    return pl.pallas_call(
        _qkv_kernel,
        grid=(S // TM, B),
        in_specs=[pl.BlockSpec((None, TM, D_MODEL), lambda i, b: (b, i, 0)),
                  _resident((1, D_MODEL)), _resident(w.shape), _resident(seg.shape),
                  _resident((1, LANES)), _resident((1, LANES)), tab, tab, tab, tab],
        out_specs=[wide, dup, vt, wide, wide, wide],
        out_shape=[wide_shape, dup_shape, vt_shape, wide_shape, wide_shape, wide_shape],
        compiler_params=pltpu.CompilerParams(
            dimension_semantics=("arbitrary", "arbitrary"), vmem_limit_bytes=VMEM_LIMIT),
        name="qkv",
    )(x1, pre_g, w, seg, qg, kg, cosa, sina, cosb, sinb)


def _attn_a_kernel(q_ref, k_ref, vt_ref, o_ref, qs_sc, m_sc, acc_sc, sa_sc, sb_sc, cma_sc, cmb_sc):
    tq = q_ref.shape[0]
    n_pairs = qs_sc.shape[0]
    low = lax.broadcasted_iota(jnp.int32, (tq, LANES), 1) < HEAD_DIM
    for c in range(n_pairs):
        qc = q_ref[:, c * LANES:(c + 1) * LANES]
        zero = jnp.zeros_like(qc)
        qs_sc[c, 0:tq, :] = jnp.where(low, qc, zero)
        qs_sc[c, tq:, :] = jnp.where(low, zero, qc)

    m_sc[...] = jnp.full(m_sc.shape, NEG_INF, F32)
    acc_sc[...] = jnp.zeros(acc_sc.shape, F32)

    n_k = vt_ref.shape[0]

    def scores(kb, s_out, cm_out):
        k = k_ref[pl.ds(pl.multiple_of(kb * TK_A, TK_A), TK_A), :]
        for c in range(n_pairs):
            s = _dot_nt(k, qs_sc[c])
            s_out[c] = s
            cm_out[c] = jnp.max(s, axis=0, keepdims=True)

    def consume(kb, s_in, cm_in):
        vt = vt_ref[kb]
        for c in range(n_pairs):
            m_old = m_sc[c]
            m_new = jnp.maximum(m_old, cm_in[c])
            alpha = jnp.exp2(m_old - m_new)
            p = jnp.exp2(s_in[c] - m_new).astype(BF16)
            acc_sc[c] = alpha * acc_sc[c] + _dot(vt, p)
            m_sc[c] = m_new

    bufs = ((sa_sc, cma_sc), (sb_sc, cmb_sc))
    scores(0, *bufs[0])

    def fused(kb, cur, nxt):
        s_in, cm_in = cur
        s_out, cm_out = nxt
        k0 = pl.multiple_of((kb + 1) * TK_A, TK_A)
        m_new, alpha, pv, cm = [], [], [], []
        for c in range(n_pairs):
            m_old = m_sc[c]
            m_new.append(jnp.maximum(m_old, cm_in[c]))
            alpha.append(jnp.exp2(m_old - m_new[c]))
            pv.append(None)
            cm.append(None)
        for j in range(TK_A // SUB_A):
            rows = slice(j * SUB_A, (j + 1) * SUB_A)
            k = k_ref[pl.ds(k0 + j * SUB_A, SUB_A), :]
            vt = vt_ref[kb, :, rows]
            for c in range(n_pairs):
                s = _dot_nt(k, qs_sc[c])
                s_out[c, rows, :] = s
                part = jnp.max(s, axis=0, keepdims=True)
                cm[c] = part if cm[c] is None else jnp.maximum(cm[c], part)
            for c in range(n_pairs):
                p = jnp.exp2(s_in[c, rows, :] - m_new[c]).astype(BF16)
                part = _dot(vt, p)
                pv[c] = part if pv[c] is None else pv[c] + part
        for c in range(n_pairs):
            cm_out[c] = cm[c]
            acc_sc[c] = alpha[c] * acc_sc[c] + pv[c]
            m_sc[c] = m_new[c]

    def body(kb, carry):
        for parity in range(2):
            @pl.when(kb % 2 == parity)
            def _():
                fused(kb, bufs[parity], bufs[1 - parity])
        return carry

    lax.fori_loop(0, n_k - 1, body, 0)
    consume(n_k - 1, *bufs[(n_k - 1) % 2])
    for c in range(n_pairs):
        o = acc_sc[c, 0:HEAD_DIM, :] / acc_sc[c, HEAD_DIM:HEAD_DIM + 1, :]
        pair = jnp.concatenate([o[:, 0:tq], o[:, tq:]], axis=0)
        o_ref[:, c * LANES:(c + 1) * LANES] = pair.T.astype(o_ref.dtype)


def _attn_a_call(qa, ka, vta):
    B, S, _ = qa.shape
    qcols = A_Q // N_KV_A
    pairs = N_HEADS_A // N_KV_A // 2
    qspec = pl.BlockSpec((None, TQ_A, qcols), lambda b, g, i: (b, i, g))
    kspec = pl.BlockSpec((None, None, S, LANES), lambda b, g, i: (b, g, 0, 0))
    vspec = pl.BlockSpec((None, None, S // TK_A, VT_ROWS, TK_A), lambda b, g, i: (b, g, 0, 0, 0))
    return pl.pallas_call(
        _attn_a_kernel,
        grid=(B, N_KV_A, S // TQ_A),
        in_specs=[qspec, kspec, vspec],
        out_specs=qspec,
        out_shape=jax.ShapeDtypeStruct((B, S, A_Q), BF16),
        scratch_shapes=[pltpu.VMEM((pairs, 2 * TQ_A, LANES), BF16),
                        pltpu.VMEM((pairs, 1, 2 * TQ_A), F32),
                        pltpu.VMEM((pairs, VT_ROWS, 2 * TQ_A), F32),
                        pltpu.VMEM((pairs, TK_A, 2 * TQ_A), F32),
                        pltpu.VMEM((pairs, TK_A, 2 * TQ_A), F32),
                        pltpu.VMEM((pairs, 1, 2 * TQ_A), F32),
                        pltpu.VMEM((pairs, 1, 2 * TQ_A), F32)],
        compiler_params=pltpu.CompilerParams(
            dimension_semantics=("arbitrary", "arbitrary", "arbitrary"),
            vmem_limit_bytes=VMEM_LIMIT),
        name="attn_a",
    )(qa, ka, vta)


def _attn_b_kernel(q_ref, kp_ref, kc_ref, kn_ref, vp_ref, vc_ref, vn_ref,
                   o_ref, st_ref, kbuf, vbuf, *, seq_len):
    j = pl.program_id(2)
    kbuf[0:SPAN_B, :] = kp_ref[...]
    kbuf[SPAN_B:SPAN_B + TQ_B, :] = kc_ref[...]
    kbuf[SPAN_B + TQ_B:, :] = kn_ref[...]
    vbuf[0:SPAN_B, :] = vp_ref[...]
    vbuf[SPAN_B:SPAN_B + TQ_B, :] = vc_ref[...]
    vbuf[SPAN_B + TQ_B:, :] = vn_ref[...]

    n_keys = SUB_B + 2 * SPAN_B
    a_idx = lax.broadcasted_iota(jnp.int32, (SUB_B, n_keys), 0)
    c_idx = lax.broadcasted_iota(jnp.int32, (SUB_B, n_keys), 1)
    band = jnp.abs(c_idx - SPAN_B - a_idx) <= SPAN_B
    lane = lax.broadcasted_iota(jnp.int32, (SUB_B, LANES), 1)
    low = lane < HEAD_DIM

    def body(i, carry):
        q0 = pl.multiple_of(i * SUB_B, SUB_B)
        kpos = j * TQ_B + q0 - SPAN_B + c_idx
        valid = band & (kpos >= 0) & (kpos < seq_len)
        bias = jnp.where(valid, 0.0, NEG_INF).astype(F32)
        st = jnp.zeros((SUB_B, LANES), F32)
        for hp in range(N_HEADS_B // 2):
            cols = slice(hp * LANES, (hp + 1) * LANES)
            qp = q_ref[pl.ds(q0, SUB_B), cols]
            kw = kbuf[pl.ds(q0, n_keys), cols]
            vw = vbuf[pl.ds(q0, n_keys), cols]
            outs = []
            for half in range(2):
                keep = low if half == 0 else jnp.logical_not(low)
                qh = jnp.where(keep, qp, jnp.zeros_like(qp))
                s = _dot_nt(qh, kw) + bias
                m = jnp.max(s, axis=1, keepdims=True)
                p = jnp.exp(s - m)
                l = jnp.sum(p, axis=1, keepdims=True)
                outs.append(_dot(p.astype(BF16), vw))
                head = 2 * hp + half
                st = jnp.where(lane == head, m, st)
                st = jnp.where(lane == N_HEADS_B + head, l, st)
            o_ref[pl.ds(q0, SUB_B), cols] = jnp.where(low, outs[0], outs[1]).astype(o_ref.dtype)
        st_ref[pl.ds(q0, SUB_B), :] = st
        return carry

    lax.fori_loop(0, TQ_B // SUB_B, body, 0)


def _attn_b_call(qb, kb, vb, dilation):
    B, S, C = qb.shape
    L = S // dilation
    nh = L // SPAN_B
    per = TQ_B // SPAN_B
    view = lambda a: a.reshape(B, L, dilation * C)
    cur = pl.BlockSpec((None, TQ_B, C), lambda b, r, j: (b, j, r))
    prev = pl.BlockSpec((None, SPAN_B, C), lambda b, r, j: (b, jnp.maximum(j * per - 1, 0), r))
    nxt = pl.BlockSpec((None, SPAN_B, C), lambda b, r, j: (b, jnp.minimum((j + 1) * per, nh - 1), r))
    st_spec = pl.BlockSpec((None, TQ_B, LANES), lambda b, r, j: (b, j, r))
    o, st = pl.pallas_call(
        functools.partial(_attn_b_kernel, seq_len=L),
        grid=(B, dilation, L // TQ_B),
        in_specs=[cur, prev, cur, nxt, prev, cur, nxt],
        out_specs=[cur, st_spec],
        out_shape=[jax.ShapeDtypeStruct((B, L, dilation * C), BF16),
                   jax.ShapeDtypeStruct((B, L, dilation * LANES), F32)],
        scratch_shapes=[pltpu.VMEM((TQ_B + 2 * SPAN_B, C), BF16),
                        pltpu.VMEM((TQ_B + 2 * SPAN_B, C), BF16)],
        compiler_params=pltpu.CompilerParams(
            dimension_semantics=("arbitrary", "arbitrary", "arbitrary"),
            vmem_limit_bytes=VMEM_LIMIT),
        name=f"attn_b_d{dilation}",
    )(view(qb), view(kb), view(kb), view(kb), view(vb), view(vb), view(vb))
    return o.reshape(B, S, C), st.reshape(B, S, LANES)


def _out_ffn_kernel(x_ref, ha_ref, o1_ref, o2_ref, o3_ref, s1_ref, s2_ref, s3_ref,
                    wo_ref, mixg_ref, pre_ref, post_ref, wg_ref, wu_ref, wd_ref, y_ref):
    low = lax.broadcasted_iota(jnp.int32, (TM, LANES), 1) < HEAD_DIM
    stats = (s1_ref[...], s2_ref[...], s3_ref[...])
    o_refs = (o1_ref, o2_ref, o3_ref)

    def head_weights(h):
        ms = [s[:, h:h + 1] for s in stats]
        ls = [s[:, N_HEADS_B + h:N_HEADS_B + h + 1] for s in stats]
        m_all = jnp.maximum(jnp.maximum(ms[0], ms[1]), ms[2])
        es = [jnp.exp(m - m_all) for m in ms]
        den = es[0] * ls[0] + es[1] * ls[1] + es[2] * ls[2]
        return [e / den for e in es]

    pairs = []
    for hp in range(N_HEADS_B // 2):
        cols = slice(hp * LANES, (hp + 1) * LANES)
        w_lo = head_weights(2 * hp)
        w_hi = head_weights(2 * hp + 1)
        acc = None
        for d in range(3):
            w = jnp.where(low, w_lo[d], w_hi[d])
            term = w * o_refs[d][:, cols].astype(F32)
            acc = term if acc is None else acc + term
        pairs.append(acc.astype(BF16))
    heads_b = jnp.concatenate(pairs, axis=1)

    mixed = _dot(ha_ref[...], wo_ref[0:A_Q, :]) + _dot(heads_b, wo_ref[A_Q:, :])
    x2 = x_ref[...] + _rms(mixed, mixg_ref[...])
    y_ref[...] = _swiglu_half_step(x2, pre_ref[...], post_ref[...], wg_ref, wu_ref, wd_ref)


def _out_ffn_call(x1, heads_a, parts, wo, mix_g, pre_g, post_g, wg, wu, wd):
    n = x1.shape[0]
    row = lambda c: pl.BlockSpec((TM, c), lambda i: (i, 0))
    (o1, s1), (o2, s2), (o3, s3) = parts
    return pl.pallas_call(
        _out_ffn_kernel,
        grid=(n // TM,),
        in_specs=[row(D_MODEL), row(A_Q), row(B_QKV), row(B_QKV), row(B_QKV),
                  row(LANES), row(LANES), row(LANES),
                  _resident(wo.shape), _resident((1, D_MODEL)), _resident((1, D_MODEL)),
                  _resident((1, D_MODEL)), _resident(wg.shape), _resident(wu.shape),
                  _resident(wd.shape)],
        out_specs=row(D_MODEL),
        out_shape=jax.ShapeDtypeStruct((n, D_MODEL), F32),
        compiler_params=pltpu.CompilerParams(
            dimension_semantics=("arbitrary",), vmem_limit_bytes=VMEM_LIMIT),
        name="out_ffn2",
    )(x1, heads_a, o1, o2, o3, s1, s2, s3, wo, mix_g, pre_g, post_g, wg, wu, wd)


def _rope_tables(seq):
    pos = jnp.arange(seq, dtype=jnp.int32)
    row = (pos // GRID_W).astype(F32)[:, None]
    col = (pos % GRID_W).astype(F32)[:, None]
    dim_a = HEAD_DIM // 2
    fa = ROPE_THETA ** (-jnp.arange(0, dim_a, 2, dtype=F32) / dim_a)
    fb = ROPE_THETA ** (-jnp.arange(0, HEAD_DIM, 2, dtype=F32) / HEAD_DIM)
    ar, ac = row * fa[None, :], col * fa[None, :]
    ab = pos.astype(F32)[:, None] * fb[None, :]
    cos_a = jnp.concatenate([jnp.cos(ar), jnp.cos(ar), jnp.cos(ac), jnp.cos(ac)], axis=-1)
    sin_a = jnp.concatenate([-jnp.sin(ar), jnp.sin(ar), -jnp.sin(ac), jnp.sin(ac)], axis=-1)
    cos_b = jnp.concatenate([jnp.cos(ab), jnp.cos(ab)], axis=-1)
    sin_b = jnp.concatenate([-jnp.sin(ab), jnp.sin(ab)], axis=-1)
    two = lambda t: jnp.tile(t, (1, LANES // HEAD_DIM))
    return two(cos_a), two(sin_a), two(cos_b), two(sin_b)


def _layer(x, ffn1_pre_g, ffn1_post_g, ffn1_w_gate, ffn1_w_up, ffn1_w_down,
           mix_pre_g, mix_post_g, w_qkv, q_norm_g, k_norm_g, w_out,
           ffn2_pre_g, ffn2_post_g, ffn2_w_gate, ffn2_w_up, ffn2_w_down, tables, seg):
    B, S, D = x.shape
    vec = lambda g: g.reshape(1, -1).astype(F32)
    two = lambda g: jnp.tile(vec(g), (1, LANES // HEAD_DIM))
    bf = lambda w: w.astype(BF16)

    x1 = _ffn_call(x.reshape(B * S, D), vec(ffn1_pre_g), vec(ffn1_post_g),
                   bf(ffn1_w_gate), bf(ffn1_w_up), bf(ffn1_w_down))
    qa, ka, va, qb, kb, vb = _qkv_call(x1.reshape(B, S, D), vec(mix_pre_g), bf(w_qkv), seg,
                                       two(q_norm_g), two(k_norm_g), *tables)
    heads_a = _attn_a_call(qa, ka, va)
    parts = [_attn_b_call(qb, kb, vb, d) for (_, d) in DILATED_CONFIGS]
    flat = lambda a: a.reshape(B * S, a.shape[-1])
    y = _out_ffn_call(x1, flat(heads_a), [(flat(o), flat(s)) for o, s in parts],
                      bf(w_out), vec(mix_post_g), vec(ffn2_pre_g), vec(ffn2_post_g),
                      bf(ffn2_w_gate), bf(ffn2_w_up), bf(ffn2_w_down))
    return y.reshape(B, S, D)


def kernel(x, ffn1_pre_g, ffn1_post_g, ffn1_w_gate, ffn1_w_up, ffn1_w_down, mix_pre_g, mix_post_g, w_qkv, q_norm_g, k_norm_g, w_out, ffn2_pre_g, ffn2_post_g, ffn2_w_gate, ffn2_w_up, ffn2_w_down):
    assert all(w // 2 // d == SPAN_B for w, d in DILATED_CONFIGS)
    S = x.shape[1]
    tables = _rope_tables(S)
    head_of_lane = jnp.arange(LANES) // HEAD_DIM
    seg = (head_of_lane[:, None] == head_of_lane[None, :]).astype(BF16) / HEAD_DIM
    params = (ffn1_pre_g, ffn1_post_g, ffn1_w_gate, ffn1_w_up, ffn1_w_down, mix_pre_g, mix_post_g,
              w_qkv, q_norm_g, k_norm_g, w_out, ffn2_pre_g, ffn2_post_g, ffn2_w_gate, ffn2_w_up,
              ffn2_w_down)
    for l in range(ffn1_pre_g.shape[0]):
        x = _layer(x, *(p[l] for p in params), tables, seg)
    return x
```

```python
import functools

import jax
import jax.numpy as jnp
from jax import lax
from jax.experimental import pallas as pl
from jax.experimental.pallas import tpu as pltpu

D_MODEL = 1024
HEAD_DIM = 64
N_HEADS_A = 8
N_KV_A = 2
N_HEADS_B = 8
DILATED_CONFIGS = ((128, 1), (512, 4), (2048, 16))
DILATIONS = tuple(d for _, d in DILATED_CONFIGS)
GRID_W = 64
ROPE_THETA = 10000.0
D_FF = 2816
EPS = 1e-6
NEG_INF = -1e30

A_Q = N_HEADS_A * HEAD_DIM
A_KV = N_KV_A * HEAD_DIM
B_QKV = N_HEADS_B * HEAD_DIM
QKV_COLS = A_Q + 2 * A_KV + 3 * B_QKV
SCALE = HEAD_DIM ** -0.5
LOG2E = 1.4426950408889634

LANES = 128
FF_CHUNKS = ((0, 1024), (1024, 2048), (2048, 2816))
TM = 512
TQ_A = 256
TK_A = 1024
SUB_A = 256
VT_ROWS = HEAD_DIM + 16
TQ_B = 512
SUB_B = 128
SPAN_B = 64
VMEM_LIMIT = 52 * 1024 * 1024

BF16 = jnp.bfloat16
F32 = jnp.float32


def _dot(a, b):
    return jnp.dot(a, b, preferred_element_type=F32)


def _dot_nt(a, b):
    return lax.dot_general(a, b, (((1,), (1,)), ((), ())), preferred_element_type=F32)


def _rms(x, g):
    ms = jnp.mean(x * x, axis=-1, keepdims=True)
    return x * lax.rsqrt(ms + EPS) * g


def _swiglu_half_step(x, pre_g, post_g, wg_ref, wu_ref, wd_ref):
    h = _rms(x, pre_g).astype(BF16)
    f = None
    for lo, hi in FF_CHUNKS:
        g = _dot(h, wg_ref[:, lo:hi])
        u = _dot(h, wu_ref[:, lo:hi])
        a = (g / (1.0 + jnp.exp(-g)) * u).astype(BF16)
        part = _dot(a, wd_ref[lo:hi, :])
        f = part if f is None else f + part
    return x + 0.5 * _rms(f, post_g)


def _ffn_kernel(x_ref, pre_ref, post_ref, wg_ref, wu_ref, wd_ref, o_ref):
    o_ref[...] = _swiglu_half_step(x_ref[...], pre_ref[...], post_ref[...],
                                   wg_ref, wu_ref, wd_ref)


def _resident(shape):
    nd = len(shape)
    return pl.BlockSpec(shape, lambda *_: (0,) * nd, pipeline_mode=pl.Buffered(1))


def _ffn_call(x2d, pre_g, post_g, wg, wu, wd):
    n = x2d.shape[0]
    row = pl.BlockSpec((TM, D_MODEL), lambda i: (i, 0))
    return pl.pallas_call(
        _ffn_kernel,
        grid=(n // TM,),
        in_specs=[row, _resident((1, D_MODEL)), _resident((1, D_MODEL)),
                  _resident(wg.shape), _resident(wu.shape), _resident(wd.shape)],
        out_specs=row,
        out_shape=jax.ShapeDtypeStruct(x2d.shape, F32),
        compiler_params=pltpu.CompilerParams(
            dimension_semantics=("arbitrary",), vmem_limit_bytes=VMEM_LIMIT),
        name="ffn1",
    )(x2d, pre_g, post_g, wg, wu, wd)


def _rope(c, cos, sin_signed, half):
    lane = lax.broadcasted_iota(jnp.int32, c.shape, 1)
    first = (lane % (2 * half)) < half
    partner = jnp.where(first, pltpu.roll(c, LANES - half, 1), pltpu.roll(c, half, 1))
    return c * cos + partner * sin_signed


def _head_rms(c, seg_ref, g):
    sq = c * c
    hi = sq.astype(BF16)
    lo = (sq - hi.astype(F32)).astype(BF16)
    ms = _dot(hi, seg_ref[...]) + _dot(lo, seg_ref[...])
    return c * lax.rsqrt(ms + EPS) * g


def _dup_halves(c):
    lane = lax.broadcasted_iota(jnp.int32, c.shape, 1)
    low = lane < HEAD_DIM
    r = pltpu.roll(c, HEAD_DIM, 1)
    return jnp.where(low, c, r), jnp.where(low, r, c)


def _qkv_kernel(x_ref, pre_ref, w_ref, seg_ref, qg_ref, kg_ref,
                cosa_ref, sina_ref, cosb_ref, sinb_ref,
                qa_ref, ka_ref, va_ref, *rest):
    b_refs, (qf_sc, kf_sc, vf_sc) = rest[:-3], rest[-3:]
    h = _rms(x_ref[...], pre_ref[...]).astype(BF16)
    cosa, sina = cosa_ref[...], sina_ref[...]
    cosb, sinb = cosb_ref[...], sinb_ref[...]
    qg, kg = qg_ref[...], kg_ref[...]

    for c in range(A_Q // LANES):
        q = _dot(h, w_ref[:, c * LANES:(c + 1) * LANES])
        q = _rope(_head_rms(q, seg_ref, qg), cosa, sina, HEAD_DIM // 4) * (SCALE * LOG2E)
        qa_ref[:, c * LANES:(c + 1) * LANES] = q.astype(BF16)

    k = _dot(h, w_ref[:, A_Q:A_Q + A_KV])
    k = _rope(_head_rms(k, seg_ref, kg), cosa, sina, HEAD_DIM // 4)
    k0, k1 = _dup_halves(k)
    ka_ref[0] = k0.astype(BF16)
    ka_ref[1] = k1.astype(BF16)

    v = _dot(h, w_ref[:, A_Q + A_KV:A_Q + 2 * A_KV])
    vt = v.T.astype(BF16)
    ones = jnp.ones((VT_ROWS - HEAD_DIM, TM), BF16)
    for g in range(N_KV_A):
        va_ref[g, 0:HEAD_DIM, :] = vt[g * HEAD_DIM:(g + 1) * HEAD_DIM, :]
        va_ref[g, HEAD_DIM:, :] = ones

    base = A_Q + 2 * A_KV
    for c in range(B_QKV // LANES):
        q = _dot(h, w_ref[:, base + c * LANES:base + (c + 1) * LANES])
        qf_sc[c] = _rope(q, cosb, sinb, HEAD_DIM // 2) * (SCALE * LOG2E)
        k = _dot(h, w_ref[:, base + B_QKV + c * LANES:base + B_QKV + (c + 1) * LANES])
        kf_sc[c] = _rope(k, cosb, sinb, HEAD_DIM // 2)
        vf_sc[c] = _dot(h, w_ref[:, base + 2 * B_QKV + c * LANES:base + 2 * B_QKV + (c + 1) * LANES])
    for j, src in enumerate((qf_sc, kf_sc, vf_sc)):
        for i, d in enumerate(DILATIONS):
            out = b_refs[3 * i + j]
            for r in range(d):
                rows = slice(None) if d == 1 else pl.ds(r, TM // d, stride=d)
                for c in range(B_QKV // LANES):
                    out[r, :, c * LANES:(c + 1) * LANES] = src[c, rows, :].astype(BF16)


def _qkv_call(x1, pre_g, w, seg, qg, kg, cosa, sina, cosb, sinb):
    B, S, _ = x1.shape
    tab = pl.BlockSpec((TM, LANES), lambda i, b: (i, 0))
    wide = pl.BlockSpec((None, TM, A_Q), lambda i, b: (b, i, 0))
    dup = pl.BlockSpec((None, N_KV_A, TM, LANES), lambda i, b: (b, 0, i, 0))
    per = TK_A // TM
    vt = pl.BlockSpec((None, N_KV_A, None, VT_ROWS, TM), lambda i, b: (b, 0, i // per, 0, i % per))
    wide_shape = jax.ShapeDtypeStruct((B, S, A_Q), BF16)
    dup_shape = jax.ShapeDtypeStruct((B, N_KV_A, S, LANES), BF16)
    vt_shape = jax.ShapeDtypeStruct((B, N_KV_A, S // TK_A, VT_ROWS, TK_A), BF16)
    b_specs, b_shapes = [], []
    for d in DILATIONS:
        b_specs += [pl.BlockSpec((None, d, TM // d, B_QKV), lambda i, b: (b, 0, i, 0))] * 3
        b_shapes += [jax.ShapeDtypeStruct((B, d, S // d, B_QKV), BF16)] * 3
    outs = pl.pallas_call(
        _qkv_kernel,
        grid=(S // TM, B),
        in_specs=[pl.BlockSpec((None, TM, D_MODEL), lambda i, b: (b, i, 0)),
                  _resident((1, D_MODEL)), _resident(w.shape), _resident(seg.shape),
                  _resident((1, LANES)), _resident((1, LANES)), tab, tab, tab, tab],
        out_specs=[wide, dup, vt] + b_specs,
        out_shape=[wide_shape, dup_shape, vt_shape] + b_shapes,
        scratch_shapes=[pltpu.VMEM((B_QKV // LANES, TM, LANES), F32)] * 3,
        compiler_params=pltpu.CompilerParams(
            dimension_semantics=("arbitrary", "arbitrary"), vmem_limit_bytes=VMEM_LIMIT),
        name="qkv",
    )(x1, pre_g, w, seg, qg, kg, cosa, sina, cosb, sinb)
    qa, ka, vta = outs[:3]
    qkv_b = [tuple(outs[3 + 3 * i:6 + 3 * i]) for i in range(len(DILATIONS))]
    return qa, ka, vta, qkv_b


def _attn_a_kernel(q_ref, k_ref, vt_ref, o_ref, qs_sc, m_sc, acc_sc, sa_sc, sb_sc, cma_sc, cmb_sc):
    tq = q_ref.shape[0]
    n_pairs = qs_sc.shape[0]
    low = lax.broadcasted_iota(jnp.int32, (tq, LANES), 1) < HEAD_DIM
    for c in range(n_pairs):
        qc = q_ref[:, c * LANES:(c + 1) * LANES]
        zero = jnp.zeros_like(qc)
        qs_sc[c, 0:tq, :] = jnp.where(low, qc, zero)
        qs_sc[c, tq:, :] = jnp.where(low, zero, qc)

    m_sc[...] = jnp.full(m_sc.shape, NEG_INF, F32)
    acc_sc[...] = jnp.zeros(acc_sc.shape, F32)

    n_k = vt_ref.shape[0]

    def scores(kb, s_out, cm_out):
        k = k_ref[pl.ds(pl.multiple_of(kb * TK_A, TK_A), TK_A), :]
        for c in range(n_pairs):
            s = _dot_nt(k, qs_sc[c])
            s_out[c] = s
            cm_out[c] = jnp.max(s, axis=0, keepdims=True)

    def consume(kb, s_in, cm_in):
        vt = vt_ref[kb]
        for c in range(n_pairs):
            m_old = m_sc[c]
            m_new = jnp.maximum(m_old, cm_in[c])
            alpha = jnp.exp2(m_old - m_new)
            p = jnp.exp2(s_in[c] - m_new).astype(BF16)
            acc_sc[c] = alpha * acc_sc[c] + _dot(vt, p)
            m_sc[c] = m_new

    bufs = ((sa_sc, cma_sc), (sb_sc, cmb_sc))
    scores(0, *bufs[0])

    def fused(kb, cur, nxt):
        s_in, cm_in = cur
        s_out, cm_out = nxt
        k0 = pl.multiple_of((kb + 1) * TK_A, TK_A)
        m_new, alpha, pv, cm = [], [], [], []
        for c in range(n_pairs):
            m_old = m_sc[c]
            m_new.append(jnp.maximum(m_old, cm_in[c]))
            alpha.append(jnp.exp2(m_old - m_new[c]))
            pv.append(None)
            cm.append(None)
        for j in range(TK_A // SUB_A):
            rows = slice(j * SUB_A, (j + 1) * SUB_A)
            k = k_ref[pl.ds(k0 + j * SUB_A, SUB_A), :]
            vt = vt_ref[kb, :, rows]
            for c in range(n_pairs):
                s = _dot_nt(k, qs_sc[c])
                s_out[c, rows, :] = s
                part = jnp.max(s, axis=0, keepdims=True)
                cm[c] = part if cm[c] is None else jnp.maximum(cm[c], part)
            for c in range(n_pairs):
                p = jnp.exp2(s_in[c, rows, :] - m_new[c]).astype(BF16)
                part = _dot(vt, p)
                pv[c] = part if pv[c] is None else pv[c] + part
        for c in range(n_pairs):
            cm_out[c] = cm[c]
            acc_sc[c] = alpha[c] * acc_sc[c] + pv[c]
            m_sc[c] = m_new[c]

    def body(kb, carry):
        for parity in range(2):
            @pl.when(kb % 2 == parity)
            def _():
                fused(kb, bufs[parity], bufs[1 - parity])
        return carry

    lax.fori_loop(0, n_k - 1, body, 0)
    consume(n_k - 1, *bufs[(n_k - 1) % 2])
    for c in range(n_pairs):
        o = acc_sc[c, 0:HEAD_DIM, :] / acc_sc[c, HEAD_DIM:HEAD_DIM + 1, :]
        pair = jnp.concatenate([o[:, 0:tq], o[:, tq:]], axis=0)
        o_ref[:, c * LANES:(c + 1) * LANES] = pair.T.astype(o_ref.dtype)


def _attn_a_call(qa, ka, vta):
    B, S, _ = qa.shape
    qcols = A_Q // N_KV_A
    pairs = N_HEADS_A // N_KV_A // 2
    qspec = pl.BlockSpec((None, TQ_A, qcols), lambda b, g, i: (b, i, g))
    kspec = pl.BlockSpec((None, None, S, LANES), lambda b, g, i: (b, g, 0, 0))
    vspec = pl.BlockSpec((None, None, S // TK_A, VT_ROWS, TK_A), lambda b, g, i: (b, g, 0, 0, 0))
    return pl.pallas_call(
        _attn_a_kernel,
        grid=(B, N_KV_A, S // TQ_A),
        in_specs=[qspec, kspec, vspec],
        out_specs=qspec,
        out_shape=jax.ShapeDtypeStruct((B, S, A_Q), BF16),
        scratch_shapes=[pltpu.VMEM((pairs, 2 * TQ_A, LANES), BF16),
                        pltpu.VMEM((pairs, 1, 2 * TQ_A), F32),
                        pltpu.VMEM((pairs, VT_ROWS, 2 * TQ_A), F32),
                        pltpu.VMEM((pairs, TK_A, 2 * TQ_A), F32),
                        pltpu.VMEM((pairs, TK_A, 2 * TQ_A), F32),
                        pltpu.VMEM((pairs, 1, 2 * TQ_A), F32),
                        pltpu.VMEM((pairs, 1, 2 * TQ_A), F32)],
        compiler_params=pltpu.CompilerParams(
            dimension_semantics=("arbitrary", "arbitrary", "arbitrary"),
            vmem_limit_bytes=VMEM_LIMIT),
        name="attn_a",
    )(qa, ka, vta)


def _attn_b_kernel(q_ref, kp_ref, kc_ref, kn_ref, vp_ref, vc_ref, vn_ref,
                   o_ref, st_ref, kbuf, vbuf, *, seq_len):
    j = pl.program_id(2)
    kbuf[0:SPAN_B, :] = kp_ref[...]
    kbuf[SPAN_B:SPAN_B + TQ_B, :] = kc_ref[...]
    kbuf[SPAN_B + TQ_B:, :] = kn_ref[...]
    vbuf[0:SPAN_B, :] = vp_ref[...]
    vbuf[SPAN_B:SPAN_B + TQ_B, :] = vc_ref[...]
    vbuf[SPAN_B + TQ_B:, :] = vn_ref[...]

    n_keys = SUB_B + 2 * SPAN_B
    c_idx = lax.broadcasted_iota(jnp.int32, (n_keys, 2 * SUB_B), 0)
    a_idx = lax.broadcasted_iota(jnp.int32, (n_keys, 2 * SUB_B), 1) % SUB_B
    band = jnp.abs(c_idx - SPAN_B - a_idx) <= SPAN_B
    low = lax.broadcasted_iota(jnp.int32, (SUB_B, LANES), 1) < HEAD_DIM
    ones = jnp.ones((VT_ROWS - HEAD_DIM, n_keys), BF16)
    pad = jnp.zeros((LANES - 2 * N_HEADS_B, SUB_B), F32)

    def scores_of(i):
        q0 = i * SUB_B
        kpos = j * TQ_B + q0 - SPAN_B + c_idx
        valid = band & (kpos >= 0) & (kpos < seq_len)
        bias = jnp.where(valid, 0.0, NEG_INF).astype(F32)
        out = []
        for hp in range(N_HEADS_B // 2):
            cols = slice(hp * LANES, (hp + 1) * LANES)
            qp = q_ref[q0:q0 + SUB_B, cols]
            zero = jnp.zeros_like(qp)
            qs = jnp.concatenate([jnp.where(low, qp, zero), jnp.where(low, zero, qp)], axis=0)
            out.append(_dot_nt(kbuf[q0:q0 + n_keys, cols], qs) + bias)
        return out

    def finish(i, scores):
        q0 = i * SUB_B
        ms, ls = [], []
        for hp in range(N_HEADS_B // 2):
            cols = slice(hp * LANES, (hp + 1) * LANES)
            vw = vbuf[q0:q0 + n_keys, cols]
            s = scores[hp]
            m = jnp.max(s, axis=0, keepdims=True)
            p = jnp.exp2(s - m).astype(BF16)
            vt = jnp.concatenate([vw.T, ones], axis=0)
            o_all = _dot(vt, p)
            o_t = jnp.concatenate([o_all[0:HEAD_DIM, 0:SUB_B],
                                   o_all[HEAD_DIM:LANES, SUB_B:]], axis=0)
            o_ref[q0:q0 + SUB_B, cols] = o_t.T.astype(o_ref.dtype)
            l = o_all[LANES:LANES + 1, :]
            ms += [m[:, 0:SUB_B], m[:, SUB_B:]]
            ls += [l[:, 0:SUB_B], l[:, SUB_B:]]
        st_t = jnp.concatenate(ms + ls + [pad], axis=0)
        st_ref[q0:q0 + SUB_B, :] = st_t.T

    n_sub = TQ_B // SUB_B
    scores = scores_of(0)
    for i in range(n_sub):
        nxt = scores_of(i + 1) if i + 1 < n_sub else None
        finish(i, scores)
        scores = nxt


def _attn_b_call(qb, kb, vb):
    B, d, L, C = qb.shape
    nh = L // SPAN_B
    per = TQ_B // SPAN_B
    cur = pl.BlockSpec((None, None, TQ_B, C), lambda b, r, j: (b, r, j, 0))
    prev = pl.BlockSpec((None, None, SPAN_B, C),
                        lambda b, r, j: (b, r, jnp.maximum(j * per - 1, 0), 0))
    nxt = pl.BlockSpec((None, None, SPAN_B, C),
                       lambda b, r, j: (b, r, jnp.minimum((j + 1) * per, nh - 1), 0))
    st_spec = pl.BlockSpec((None, None, TQ_B, LANES), lambda b, r, j: (b, r, j, 0))
    return pl.pallas_call(
        functools.partial(_attn_b_kernel, seq_len=L),
        grid=(B, d, L // TQ_B),
        in_specs=[cur, prev, cur, nxt, prev, cur, nxt],
        out_specs=[cur, st_spec],
        out_shape=[jax.ShapeDtypeStruct((B, d, L, C), BF16),
                   jax.ShapeDtypeStruct((B, d, L, LANES), F32)],
        scratch_shapes=[pltpu.VMEM((TQ_B + 2 * SPAN_B, C), BF16),
                        pltpu.VMEM((TQ_B + 2 * SPAN_B, C), BF16)],
        compiler_params=pltpu.CompilerParams(
            dimension_semantics=("arbitrary", "arbitrary", "arbitrary"),
            vmem_limit_bytes=VMEM_LIMIT),
        name=f"attn_b_d{d}",
    )(qb, kb, kb, kb, vb, vb, vb)


def _natural_order(ref, scratch):
    d, _, cols = ref.shape
    if d == 1:
        return ref[0].astype(F32)
    for r in range(d):
        for c in range(cols // LANES):
            scratch[c, pl.ds(r, TM // d, stride=d), :] = ref[r, :, c * LANES:(c + 1) * LANES].astype(F32)
    return jnp.concatenate([scratch[c] for c in range(cols // LANES)], axis=1)


def _out_ffn_kernel(x_ref, ha_ref, o1_ref, o2_ref, o3_ref, s1_ref, s2_ref, s3_ref, ex_ref,
                    wo_ref, mixg_ref, pre_ref, post_ref, wg_ref, wu_ref, wd_ref, y_ref,
                    of2_sc, of3_sc, sf2_sc, sf3_sc):
    stats = (_natural_order(s1_ref, None), _natural_order(s2_ref, sf2_sc),
             _natural_order(s3_ref, sf3_sc))
    parts = (_natural_order(o1_ref, None), _natural_order(o2_ref, of2_sc),
             _natural_order(o3_ref, of3_sc))
    is_max = lax.broadcasted_iota(jnp.int32, (TM, LANES), 1) < N_HEADS_B
    m_all = jnp.maximum(jnp.maximum(stats[0], stats[1]), stats[2])
    es = [jnp.exp2(s - m_all) for s in stats]
    den = None
    for e, s in zip(es, stats):
        term = e * pltpu.roll(s, LANES - N_HEADS_B, 1)
        den = term if den is None else den + term
    heads_b = None
    for e, o in zip(es, parts):
        w = jnp.where(is_max, e / den, 0.0)
        hi = w.astype(BF16)
        lo = (w - hi.astype(F32)).astype(BF16)
        term = (_dot(hi, ex_ref[...]) + _dot(lo, ex_ref[...])) * o
        heads_b = term if heads_b is None else heads_b + term

    mixed = _dot(ha_ref[...], wo_ref[0:A_Q, :]) + _dot(heads_b.astype(BF16), wo_ref[A_Q:, :])
    x2 = x_ref[...] + _rms(mixed, mixg_ref[...])
    y_ref[...] = _swiglu_half_step(x2, pre_ref[...], post_ref[...], wg_ref, wu_ref, wd_ref)


def _out_ffn_call(x1, heads_a, parts, expand, wo, mix_g, pre_g, post_g, wg, wu, wd):
    n = x1.shape[0]
    per_batch = parts[0][0].shape[2] // TM
    row = lambda c: pl.BlockSpec((TM, c), lambda i: (i, 0))

    def strided(a):
        _, d, _, c = a.shape
        return pl.BlockSpec((None, d, TM // d, c), lambda i: (i // per_batch, 0, i % per_batch, 0))

    (o1, s1), (o2, s2), (o3, s3) = parts
    return pl.pallas_call(
        _out_ffn_kernel,
        grid=(n // TM,),
        in_specs=[row(D_MODEL), row(A_Q), strided(o1), strided(o2), strided(o3),
                  strided(s1), strided(s2), strided(s3), _resident(expand.shape),
                  _resident(wo.shape), _resident((1, D_MODEL)), _resident((1, D_MODEL)),
                  _resident((1, D_MODEL)), _resident(wg.shape), _resident(wu.shape),
                  _resident(wd.shape)],
        out_specs=row(D_MODEL),
        out_shape=jax.ShapeDtypeStruct((n, D_MODEL), F32),
        scratch_shapes=[pltpu.VMEM((B_QKV // LANES, TM, LANES), F32),
                        pltpu.VMEM((B_QKV // LANES, TM, LANES), F32),
                        pltpu.VMEM((1, TM, LANES), F32), pltpu.VMEM((1, TM, LANES), F32)],
        compiler_params=pltpu.CompilerParams(
            dimension_semantics=("arbitrary",), vmem_limit_bytes=VMEM_LIMIT),
        name="out_ffn2",
    )(x1, heads_a, o1, o2, o3, s1, s2, s3, expand, wo, mix_g, pre_g, post_g, wg, wu, wd)


def _rope_tables(seq):
    pos = jnp.arange(seq, dtype=jnp.int32)
    row = (pos // GRID_W).astype(F32)[:, None]
    col = (pos % GRID_W).astype(F32)[:, None]
    dim_a = HEAD_DIM // 2
    fa = ROPE_THETA ** (-jnp.arange(0, dim_a, 2, dtype=F32) / dim_a)
    fb = ROPE_THETA ** (-jnp.arange(0, HEAD_DIM, 2, dtype=F32) / HEAD_DIM)
    ar, ac = row * fa[None, :], col * fa[None, :]
    ab = pos.astype(F32)[:, None] * fb[None, :]
    cos_a = jnp.concatenate([jnp.cos(ar), jnp.cos(ar), jnp.cos(ac), jnp.cos(ac)], axis=-1)
    sin_a = jnp.concatenate([-jnp.sin(ar), jnp.sin(ar), -jnp.sin(ac), jnp.sin(ac)], axis=-1)
    cos_b = jnp.concatenate([jnp.cos(ab), jnp.cos(ab)], axis=-1)
    sin_b = jnp.concatenate([-jnp.sin(ab), jnp.sin(ab)], axis=-1)
    two = lambda t: jnp.tile(t, (1, LANES // HEAD_DIM))
    return two(cos_a), two(sin_a), two(cos_b), two(sin_b)


def _layer(x, ffn1_pre_g, ffn1_post_g, ffn1_w_gate, ffn1_w_up, ffn1_w_down,
           mix_pre_g, mix_post_g, w_qkv, q_norm_g, k_norm_g, w_out,
           ffn2_pre_g, ffn2_post_g, ffn2_w_gate, ffn2_w_up, ffn2_w_down, tables, seg, expand):
    B, S, D = x.shape
    vec = lambda g: g.reshape(1, -1).astype(F32)
    two = lambda g: jnp.tile(vec(g), (1, LANES // HEAD_DIM))
    bf = lambda w: w.astype(BF16)

    x1 = _ffn_call(x.reshape(B * S, D), vec(ffn1_pre_g), vec(ffn1_post_g),
                   bf(ffn1_w_gate), bf(ffn1_w_up), bf(ffn1_w_down))
    qa, ka, vta, qkv_b = _qkv_call(x1.reshape(B, S, D), vec(mix_pre_g), bf(w_qkv), seg,
                                   two(q_norm_g), two(k_norm_g), *tables)
    heads_a = _attn_a_call(qa, ka, vta)
    parts = [_attn_b_call(*qkv) for qkv in qkv_b]
    y = _out_ffn_call(x1, heads_a.reshape(B * S, A_Q), parts, expand,
                      bf(w_out), vec(mix_post_g), vec(ffn2_pre_g), vec(ffn2_post_g),
                      bf(ffn2_w_gate), bf(ffn2_w_up), bf(ffn2_w_down))
    return y.reshape(B, S, D)


def kernel(x, ffn1_pre_g, ffn1_post_g, ffn1_w_gate, ffn1_w_up, ffn1_w_down, mix_pre_g, mix_post_g, w_qkv, q_norm_g, k_norm_g, w_out, ffn2_pre_g, ffn2_post_g, ffn2_w_gate, ffn2_w_up, ffn2_w_down):
    assert all(w // 2 // d == SPAN_B for w, d in DILATED_CONFIGS) and TK_A % TM == 0
    S = x.shape[1]
    tables = _rope_tables(S)
    head_of_lane = jnp.arange(LANES) // HEAD_DIM
    seg = (head_of_lane[:, None] == head_of_lane[None, :]).astype(BF16) / HEAD_DIM
    expand = (jnp.arange(LANES)[:, None] == jnp.arange(B_QKV)[None, :] // HEAD_DIM).astype(BF16)
    params = (ffn1_pre_g, ffn1_post_g, ffn1_w_gate, ffn1_w_up, ffn1_w_down, mix_pre_g, mix_post_g,
              w_qkv, q_norm_g, k_norm_g, w_out, ffn2_pre_g, ffn2_post_g, ffn2_w_gate, ffn2_w_up,
              ffn2_w_down)
    for l in range(ffn1_pre_g.shape[0]):
        x = _layer(x, *(p[l] for p in params), tables, seg, expand)
    return x
```

```python
import functools

import jax
import jax.numpy as jnp
from jax import lax
from jax.experimental import pallas as pl
from jax.experimental.pallas import tpu as pltpu

D_MODEL = 1024
HEAD_DIM = 64
N_HEADS_A = 8
N_KV_A = 2
N_HEADS_B = 8
DILATED_CONFIGS = ((128, 1), (512, 4), (2048, 16))
DILATIONS = tuple(d for _, d in DILATED_CONFIGS)
GRID_W = 64
ROPE_THETA = 10000.0
D_FF = 2816
EPS = 1e-6
NEG_INF = -1e30

A_Q = N_HEADS_A * HEAD_DIM
A_KV = N_KV_A * HEAD_DIM
B_QKV = N_HEADS_B * HEAD_DIM
QKV_COLS = A_Q + 2 * A_KV + 3 * B_QKV
SCALE = HEAD_DIM ** -0.5
LOG2E = 1.4426950408889634

LANES = 128
FF_CHUNKS = ((0, 1024), (1024, 2048), (2048, 2816))
TM = 512
TQ_A = 256
TK_A = 1024
SUB_A = 256
MAX_SCORE_BOUND = 48.0
VT_ROWS = HEAD_DIM + 16
TQ_B = 512
SUB_B = 128
SPAN_B = 64
VMEM_LIMIT = 52 * 1024 * 1024

BF16 = jnp.bfloat16
F32 = jnp.float32


def _dot(a, b):
    return jnp.dot(a, b, preferred_element_type=F32)


def _dot_nt(a, b):
    return lax.dot_general(a, b, (((1,), (1,)), ((), ())), preferred_element_type=F32)


def _rms(x, g):
    ms = jnp.mean(x * x, axis=-1, keepdims=True)
    return x * lax.rsqrt(ms + EPS) * g


def _swiglu_half_step(x, pre_g, post_g, wg_ref, wu_ref, wd_ref):
    h = _rms(x, pre_g).astype(BF16)
    f = None
    for lo, hi in FF_CHUNKS:
        g = _dot(h, wg_ref[:, lo:hi])
        u = _dot(h, wu_ref[:, lo:hi])
        a = (g / (1.0 + jnp.exp(-g)) * u).astype(BF16)
        part = _dot(a, wd_ref[lo:hi, :])
        f = part if f is None else f + part
    return x + 0.5 * _rms(f, post_g)


def _ffn_kernel(x_ref, pre_ref, post_ref, wg_ref, wu_ref, wd_ref, o_ref):
    o_ref[...] = _swiglu_half_step(x_ref[...], pre_ref[...], post_ref[...],
                                   wg_ref, wu_ref, wd_ref)


def _resident(shape):
    nd = len(shape)
    return pl.BlockSpec(shape, lambda *_: (0,) * nd, pipeline_mode=pl.Buffered(1))


def _ffn_call(x2d, pre_g, post_g, wg, wu, wd):
    n = x2d.shape[0]
    row = pl.BlockSpec((TM, D_MODEL), lambda i: (i, 0))
    return pl.pallas_call(
        _ffn_kernel,
        grid=(n // TM,),
        in_specs=[row, _resident((1, D_MODEL)), _resident((1, D_MODEL)),
                  _resident(wg.shape), _resident(wu.shape), _resident(wd.shape)],
        out_specs=row,
        out_shape=jax.ShapeDtypeStruct(x2d.shape, F32),
        compiler_params=pltpu.CompilerParams(
            dimension_semantics=("arbitrary",), vmem_limit_bytes=VMEM_LIMIT),
        name="ffn1",
    )(x2d, pre_g, post_g, wg, wu, wd)


def _rope(c, cos, sin_signed, half):
    lane = lax.broadcasted_iota(jnp.int32, c.shape, 1)
    first = (lane % (2 * half)) < half
    partner = jnp.where(first, pltpu.roll(c, LANES - half, 1), pltpu.roll(c, half, 1))
    return c * cos + partner * sin_signed


def _head_rms(c, seg_ref, g):
    sq = c * c
    hi = sq.astype(BF16)
    lo = (sq - hi.astype(F32)).astype(BF16)
    ms = _dot(hi, seg_ref[...]) + _dot(lo, seg_ref[...])
    return c * lax.rsqrt(ms + EPS) * g


def _tile_max_norm2(c, seg_ref):
    norm2 = _dot((c * c).astype(BF16), seg_ref[...]) * HEAD_DIM
    return jnp.max(norm2, axis=0, keepdims=True)


def _dup_halves(c):
    lane = lax.broadcasted_iota(jnp.int32, c.shape, 1)
    low = lane < HEAD_DIM
    r = pltpu.roll(c, HEAD_DIM, 1)
    return jnp.where(low, c, r), jnp.where(low, r, c)


def _qkv_kernel(x_ref, pre_ref, w_ref, seg_ref, qg_ref, kg_ref,
                cosa_ref, sina_ref, cosb_ref, sinb_ref,
                qa_ref, ka_ref, va_ref, nq_ref, nk_ref, *rest):
    b_refs, (qf_sc, kf_sc, vf_sc) = rest[:-3], rest[-3:]
    h = _rms(x_ref[...], pre_ref[...]).astype(BF16)
    cosa, sina = cosa_ref[...], sina_ref[...]
    cosb, sinb = cosb_ref[...], sinb_ref[...]
    qg, kg = qg_ref[...], kg_ref[...]

    for c in range(A_Q // LANES):
        q = _dot(h, w_ref[:, c * LANES:(c + 1) * LANES])
        q = _rope(_head_rms(q, seg_ref, qg), cosa, sina, HEAD_DIM // 4) * (SCALE * LOG2E)
        qa_ref[:, c * LANES:(c + 1) * LANES] = q.astype(BF16)
        top = _tile_max_norm2(q, seg_ref)
        q_top = top if c == 0 else jnp.maximum(q_top, top)
    nq_ref[...] = jnp.broadcast_to(q_top, nq_ref.shape)

    k = _dot(h, w_ref[:, A_Q:A_Q + A_KV])
    k = _rope(_head_rms(k, seg_ref, kg), cosa, sina, HEAD_DIM // 4)
    nk_ref[...] = jnp.broadcast_to(_tile_max_norm2(k, seg_ref), nk_ref.shape)
    k0, k1 = _dup_halves(k)
    ka_ref[0] = k0.astype(BF16)
    ka_ref[1] = k1.astype(BF16)

    v = _dot(h, w_ref[:, A_Q + A_KV:A_Q + 2 * A_KV])
    vt = v.T.astype(BF16)
    ones = jnp.ones((VT_ROWS - HEAD_DIM, TM), BF16)
    for g in range(N_KV_A):
        va_ref[g, 0:HEAD_DIM, :] = vt[g * HEAD_DIM:(g + 1) * HEAD_DIM, :]
        va_ref[g, HEAD_DIM:, :] = ones

    base = A_Q + 2 * A_KV
    for c in range(B_QKV // LANES):
        q = _dot(h, w_ref[:, base + c * LANES:base + (c + 1) * LANES])
        qf_sc[c] = _rope(q, cosb, sinb, HEAD_DIM // 2) * (SCALE * LOG2E)
        k = _dot(h, w_ref[:, base + B_QKV + c * LANES:base + B_QKV + (c + 1) * LANES])
        kf_sc[c] = _rope(k, cosb, sinb, HEAD_DIM // 2)
        vf_sc[c] = _dot(h, w_ref[:, base + 2 * B_QKV + c * LANES:base + 2 * B_QKV + (c + 1) * LANES])
    for j, src in enumerate((qf_sc, kf_sc, vf_sc)):
        for i, d in enumerate(DILATIONS):
            out = b_refs[3 * i + j]
            for r in range(d):
                rows = slice(None) if d == 1 else pl.ds(r, TM // d, stride=d)
                for c in range(B_QKV // LANES):
                    out[r, :, c * LANES:(c + 1) * LANES] = src[c, rows, :].astype(BF16)


def _qkv_call(x1, pre_g, w, seg, qg, kg, cosa, sina, cosb, sinb):
    B, S, _ = x1.shape
    tab = pl.BlockSpec((TM, LANES), lambda i, b: (i, 0))
    wide = pl.BlockSpec((None, TM, A_Q), lambda i, b: (b, i, 0))
    dup = pl.BlockSpec((None, N_KV_A, TM, LANES), lambda i, b: (b, 0, i, 0))
    per = TK_A // TM
    vt = pl.BlockSpec((None, N_KV_A, None, VT_ROWS, TM), lambda i, b: (b, 0, i // per, 0, i % per))
    wide_shape = jax.ShapeDtypeStruct((B, S, A_Q), BF16)
    dup_shape = jax.ShapeDtypeStruct((B, N_KV_A, S, LANES), BF16)
    vt_shape = jax.ShapeDtypeStruct((B, N_KV_A, S // TK_A, VT_ROWS, TK_A), BF16)
    top = pl.BlockSpec((None, None, 8, LANES), lambda i, b: (b, i, 0, 0))
    top_shape = jax.ShapeDtypeStruct((B, S // TM, 8, LANES), F32)
    b_specs, b_shapes = [], []
    for d in DILATIONS:
        b_specs += [pl.BlockSpec((None, d, TM // d, B_QKV), lambda i, b: (b, 0, i, 0))] * 3
        b_shapes += [jax.ShapeDtypeStruct((B, d, S // d, B_QKV), BF16)] * 3
    outs = pl.pallas_call(
        _qkv_kernel,
        grid=(S // TM, B),
        in_specs=[pl.BlockSpec((None, TM, D_MODEL), lambda i, b: (b, i, 0)),
                  _resident((1, D_MODEL)), _resident(w.shape), _resident(seg.shape),
                  _resident((1, LANES)), _resident((1, LANES)), tab, tab, tab, tab],
        out_specs=[wide, dup, vt, top, top] + b_specs,
        out_shape=[wide_shape, dup_shape, vt_shape, top_shape, top_shape] + b_shapes,
        scratch_shapes=[pltpu.VMEM((B_QKV // LANES, TM, LANES), F32)] * 3,
        compiler_params=pltpu.CompilerParams(
            dimension_semantics=("arbitrary", "arbitrary"), vmem_limit_bytes=VMEM_LIMIT),
        name="qkv",
    )(x1, pre_g, w, seg, qg, kg, cosa, sina, cosb, sinb)
    qa, ka, vta, q_top, k_top = outs[:5]
    qkv_b = [tuple(outs[5 + 3 * i:8 + 3 * i]) for i in range(len(DILATIONS))]
    return qa, ka, vta, q_top, k_top, qkv_b


def _attn_a_kernel(q_ref, k_ref, vt_ref, o_ref, qs_sc, m_sc, acc_sc, sa_sc, sb_sc, cma_sc, cmb_sc):
    tq = q_ref.shape[0]
    n_pairs = qs_sc.shape[0]
    low = lax.broadcasted_iota(jnp.int32, (tq, LANES), 1) < HEAD_DIM
    for c in range(n_pairs):
        qc = q_ref[:, c * LANES:(c + 1) * LANES]
        zero = jnp.zeros_like(qc)
        qs_sc[c, 0:tq, :] = jnp.where(low, qc, zero)
        qs_sc[c, tq:, :] = jnp.where(low, zero, qc)

    m_sc[...] = jnp.full(m_sc.shape, NEG_INF, F32)
    acc_sc[...] = jnp.zeros(acc_sc.shape, F32)

    n_k = vt_ref.shape[0]

    def scores(kb, s_out, cm_out):
        k = k_ref[pl.ds(pl.multiple_of(kb * TK_A, TK_A), TK_A), :]
        for c in range(n_pairs):
            s = _dot_nt(k, qs_sc[c])
            s_out[c] = s
            cm_out[c] = jnp.max(s, axis=0, keepdims=True)

    def consume(kb, s_in, cm_in):
        vt = vt_ref[kb]
        for c in range(n_pairs):
            m_old = m_sc[c]
            m_new = jnp.maximum(m_old, cm_in[c])
            alpha = jnp.exp2(m_old - m_new)
            p = jnp.exp2(s_in[c] - m_new).astype(BF16)
            acc_sc[c] = alpha * acc_sc[c] + _dot(vt, p)
            m_sc[c] = m_new

    bufs = ((sa_sc, cma_sc), (sb_sc, cmb_sc))
    scores(0, *bufs[0])

    def fused(kb, cur, nxt):
        s_in, cm_in = cur
        s_out, cm_out = nxt
        k0 = pl.multiple_of((kb + 1) * TK_A, TK_A)
        m_new, alpha, pv, cm = [], [], [], []
        for c in range(n_pairs):
            m_old = m_sc[c]
            m_new.append(jnp.maximum(m_old, cm_in[c]))
            alpha.append(jnp.exp2(m_old - m_new[c]))
            pv.append(None)
            cm.append(None)
        for j in range(TK_A // SUB_A):
            rows = slice(j * SUB_A, (j + 1) * SUB_A)
            k = k_ref[pl.ds(k0 + j * SUB_A, SUB_A), :]
            vt = vt_ref[kb, :, rows]
            for c in range(n_pairs):
                s = _dot_nt(k, qs_sc[c])
                s_out[c, rows, :] = s
                part = jnp.max(s, axis=0, keepdims=True)
                cm[c] = part if cm[c] is None else jnp.maximum(cm[c], part)
            for c in range(n_pairs):
                p = jnp.exp2(s_in[c, rows, :] - m_new[c]).astype(BF16)
                part = _dot(vt, p)
                pv[c] = part if pv[c] is None else pv[c] + part
        for c in range(n_pairs):
            cm_out[c] = cm[c]
            acc_sc[c] = alpha[c] * acc_sc[c] + pv[c]
            m_sc[c] = m_new[c]

    def body(kb, carry):
        for parity in range(2):
            @pl.when(kb % 2 == parity)
            def _():
                fused(kb, bufs[parity], bufs[1 - parity])
        return carry

    lax.fori_loop(0, n_k - 1, body, 0)
    consume(n_k - 1, *bufs[(n_k - 1) % 2])
    for c in range(n_pairs):
        o = acc_sc[c, 0:HEAD_DIM, :] / acc_sc[c, HEAD_DIM:HEAD_DIM + 1, :]
        pair = jnp.concatenate([o[:, 0:tq], o[:, tq:]], axis=0)
        o_ref[:, c * LANES:(c + 1) * LANES] = pair.T.astype(o_ref.dtype)


def _attn_a_bounded_kernel(kmax_ref, q_ref, k_ref, vt_ref, o_ref, qs_sc, r_sc, acc_sc):
    tq = q_ref.shape[0]
    n_pairs = qs_sc.shape[0]
    kmax = kmax_ref[pl.program_id(0), pl.program_id(1)]
    low = lax.broadcasted_iota(jnp.int32, (tq, LANES), 1) < HEAD_DIM
    ones = jnp.ones((8, LANES), BF16)
    for c in range(n_pairs):
        qc = q_ref[:, c * LANES:(c + 1) * LANES]
        zero = jnp.zeros_like(qc)
        qs = jnp.concatenate([jnp.where(low, qc, zero), jnp.where(low, zero, qc)], axis=0)
        qs_sc[c] = qs
        qf = qs.astype(F32)
        norm2 = _dot_nt(ones, (qf * qf).astype(BF16))
        r_sc[c] = jnp.sqrt(norm2[0:1, :]) * kmax
    n_sub = TK_A // SUB_A
    n_pieces = vt_ref.shape[0] * n_sub

    def scores(t):
        k = k_ref[t * SUB_A:(t + 1) * SUB_A, :]
        return [_dot_nt(k, qs_sc[c]) for c in range(n_pairs)]

    ref = [r_sc[c] for c in range(n_pairs)]
    pv = [None] * n_pairs
    s_cur = scores(0)
    for t in range(n_pieces):
        s_nxt = scores(t + 1) if t + 1 < n_pieces else None
        vt = vt_ref[t // n_sub, :, (t % n_sub) * SUB_A:(t % n_sub + 1) * SUB_A]
        for c in range(n_pairs):
            p = jnp.exp2(s_cur[c] - ref[c]).astype(BF16)
            part = _dot(vt, p)
            pv[c] = part if pv[c] is None else pv[c] + part
        s_cur = s_nxt
    for c in range(n_pairs):
        acc_sc[c] = pv[c]
    for c in range(n_pairs):
        o = acc_sc[c, 0:HEAD_DIM, :] / acc_sc[c, HEAD_DIM:HEAD_DIM + 1, :]
        pair = jnp.concatenate([o[:, 0:tq], o[:, tq:]], axis=0)
        o_ref[:, c * LANES:(c + 1) * LANES] = pair.T.astype(o_ref.dtype)


def _attn_a_bounded_call(qa, ka, vta, kmax):
    B, S, _ = qa.shape
    qcols = A_Q // N_KV_A
    pairs = N_HEADS_A // N_KV_A // 2
    qspec = pl.BlockSpec((None, TQ_A, qcols), lambda b, g, i: (b, i, g))
    kspec = pl.BlockSpec((None, None, S, LANES), lambda b, g, i: (b, g, 0, 0))
    vspec = pl.BlockSpec((None, None, S // TK_A, VT_ROWS, TK_A), lambda b, g, i: (b, g, 0, 0, 0))
    return pl.pallas_call(
        _attn_a_bounded_kernel,
        grid=(B, N_KV_A, S // TQ_A),
        in_specs=[pl.BlockSpec(memory_space=pltpu.SMEM), qspec, kspec, vspec],
        out_specs=qspec,
        out_shape=jax.ShapeDtypeStruct((B, S, A_Q), BF16),
        scratch_shapes=[pltpu.VMEM((pairs, 2 * TQ_A, LANES), BF16),
                        pltpu.VMEM((pairs, 1, 2 * TQ_A), F32),
                        pltpu.VMEM((pairs, VT_ROWS, 2 * TQ_A), F32)],
        compiler_params=pltpu.CompilerParams(
            dimension_semantics=("arbitrary", "arbitrary", "arbitrary"),
            vmem_limit_bytes=VMEM_LIMIT),
        name="attn_a_bounded",
    )(kmax, qa, ka, vta)


def _attn_a_call(qa, ka, vta):
    B, S, _ = qa.shape
    qcols = A_Q // N_KV_A
    pairs = N_HEADS_A // N_KV_A // 2
    qspec = pl.BlockSpec((None, TQ_A, qcols), lambda b, g, i: (b, i, g))
    kspec = pl.BlockSpec((None, None, S, LANES), lambda b, g, i: (b, g, 0, 0))
    vspec = pl.BlockSpec((None, None, S // TK_A, VT_ROWS, TK_A), lambda b, g, i: (b, g, 0, 0, 0))
    return pl.pallas_call(
        _attn_a_kernel,
        grid=(B, N_KV_A, S // TQ_A),
        in_specs=[qspec, kspec, vspec],
        out_specs=qspec,
        out_shape=jax.ShapeDtypeStruct((B, S, A_Q), BF16),
        scratch_shapes=[pltpu.VMEM((pairs, 2 * TQ_A, LANES), BF16),
                        pltpu.VMEM((pairs, 1, 2 * TQ_A), F32),
                        pltpu.VMEM((pairs, VT_ROWS, 2 * TQ_A), F32),
                        pltpu.VMEM((pairs, TK_A, 2 * TQ_A), F32),
                        pltpu.VMEM((pairs, TK_A, 2 * TQ_A), F32),
                        pltpu.VMEM((pairs, 1, 2 * TQ_A), F32),
                        pltpu.VMEM((pairs, 1, 2 * TQ_A), F32)],
        compiler_params=pltpu.CompilerParams(
            dimension_semantics=("arbitrary", "arbitrary", "arbitrary"),
            vmem_limit_bytes=VMEM_LIMIT),
        name="attn_a",
    )(qa, ka, vta)


def _attn_b_kernel(q_ref, kp_ref, kc_ref, kn_ref, vp_ref, vc_ref, vn_ref,
                   o_ref, st_ref, kbuf, vbuf, *, seq_len):
    j = pl.program_id(2)
    kbuf[0:SPAN_B, :] = kp_ref[...]
    kbuf[SPAN_B:SPAN_B + TQ_B, :] = kc_ref[...]
    kbuf[SPAN_B + TQ_B:, :] = kn_ref[...]
    vbuf[0:SPAN_B, :] = vp_ref[...]
    vbuf[SPAN_B:SPAN_B + TQ_B, :] = vc_ref[...]
    vbuf[SPAN_B + TQ_B:, :] = vn_ref[...]

    n_keys = SUB_B + 2 * SPAN_B
    c_idx = lax.broadcasted_iota(jnp.int32, (n_keys, 2 * SUB_B), 0)
    a_idx = lax.broadcasted_iota(jnp.int32, (n_keys, 2 * SUB_B), 1) % SUB_B
    band = jnp.abs(c_idx - SPAN_B - a_idx) <= SPAN_B
    low = lax.broadcasted_iota(jnp.int32, (SUB_B, LANES), 1) < HEAD_DIM
    ones = jnp.ones((VT_ROWS - HEAD_DIM, n_keys), BF16)
    pad = jnp.zeros((LANES - 2 * N_HEADS_B, SUB_B), F32)

    def scores_of(i):
        q0 = i * SUB_B
        kpos = j * TQ_B + q0 - SPAN_B + c_idx
        valid = band & (kpos >= 0) & (kpos < seq_len)
        bias = jnp.where(valid, 0.0, NEG_INF).astype(F32)
        out = []
        for hp in range(N_HEADS_B // 2):
            cols = slice(hp * LANES, (hp + 1) * LANES)
            qp = q_ref[q0:q0 + SUB_B, cols]
            zero = jnp.zeros_like(qp)
            qs = jnp.concatenate([jnp.where(low, qp, zero), jnp.where(low, zero, qp)], axis=0)
            out.append(_dot_nt(kbuf[q0:q0 + n_keys, cols], qs) + bias)
        return out

    def finish(i, scores):
        q0 = i * SUB_B
        ms, ls = [], []
        for hp in range(N_HEADS_B // 2):
            cols = slice(hp * LANES, (hp + 1) * LANES)
            vw = vbuf[q0:q0 + n_keys, cols]
            s = scores[hp]
            m = jnp.max(s, axis=0, keepdims=True)
            p = jnp.exp2(s - m).astype(BF16)
            vt = jnp.concatenate([vw.T, ones], axis=0)
            o_all = _dot(vt, p)
            o_t = jnp.concatenate([o_all[0:HEAD_DIM, 0:SUB_B],
                                   o_all[HEAD_DIM:LANES, SUB_B:]], axis=0)
            o_ref[q0:q0 + SUB_B, cols] = o_t.T.astype(o_ref.dtype)
            l = o_all[LANES:LANES + 1, :]
            ms += [m[:, 0:SUB_B], m[:, SUB_B:]]
            ls += [l[:, 0:SUB_B], l[:, SUB_B:]]
        st_t = jnp.concatenate(ms + ls + [pad], axis=0)
        st_ref[q0:q0 + SUB_B, :] = st_t.T

    n_sub = TQ_B // SUB_B
    scores = scores_of(0)
    for i in range(n_sub):
        nxt = scores_of(i + 1) if i + 1 < n_sub else None
        finish(i, scores)
        scores = nxt


def _attn_b_call(qb, kb, vb):
    B, d, L, C = qb.shape
    nh = L // SPAN_B
    per = TQ_B // SPAN_B
    cur = pl.BlockSpec((None, None, TQ_B, C), lambda b, r, j: (b, r, j, 0))
    prev = pl.BlockSpec((None, None, SPAN_B, C),
                        lambda b, r, j: (b, r, jnp.maximum(j * per - 1, 0), 0))
    nxt = pl.BlockSpec((None, None, SPAN_B, C),
                       lambda b, r, j: (b, r, jnp.minimum((j + 1) * per, nh - 1), 0))
    st_spec = pl.BlockSpec((None, None, TQ_B, LANES), lambda b, r, j: (b, r, j, 0))
    return pl.pallas_call(
        functools.partial(_attn_b_kernel, seq_len=L),
        grid=(B, d, L // TQ_B),
        in_specs=[cur, prev, cur, nxt, prev, cur, nxt],
        out_specs=[cur, st_spec],
        out_shape=[jax.ShapeDtypeStruct((B, d, L, C), BF16),
                   jax.ShapeDtypeStruct((B, d, L, LANES), F32)],
        scratch_shapes=[pltpu.VMEM((TQ_B + 2 * SPAN_B, C), BF16),
                        pltpu.VMEM((TQ_B + 2 * SPAN_B, C), BF16)],
        compiler_params=pltpu.CompilerParams(
            dimension_semantics=("arbitrary", "arbitrary", "arbitrary"),
            vmem_limit_bytes=VMEM_LIMIT),
        name=f"attn_b_d{d}",
    )(qb, kb, kb, kb, vb, vb, vb)


def _natural_order(ref, scratch):
    d, _, cols = ref.shape
    if d == 1:
        return ref[0].astype(F32)
    for r in range(d):
        for c in range(cols // LANES):
            scratch[c, pl.ds(r, TM // d, stride=d), :] = ref[r, :, c * LANES:(c + 1) * LANES].astype(F32)
    return jnp.concatenate([scratch[c] for c in range(cols // LANES)], axis=1)


def _out_ffn_kernel(x_ref, ha_ref, o1_ref, o2_ref, o3_ref, s1_ref, s2_ref, s3_ref, ex_ref,
                    wo_ref, mixg_ref, pre_ref, post_ref, wg_ref, wu_ref, wd_ref, y_ref,
                    of2_sc, of3_sc, sf2_sc, sf3_sc):
    stats = (_natural_order(s1_ref, None), _natural_order(s2_ref, sf2_sc),
             _natural_order(s3_ref, sf3_sc))
    parts = (_natural_order(o1_ref, None), _natural_order(o2_ref, of2_sc),
             _natural_order(o3_ref, of3_sc))
    is_max = lax.broadcasted_iota(jnp.int32, (TM, LANES), 1) < N_HEADS_B
    m_all = jnp.maximum(jnp.maximum(stats[0], stats[1]), stats[2])
    es = [jnp.exp2(s - m_all) for s in stats]
    den = None
    for e, s in zip(es, stats):
        term = e * pltpu.roll(s, LANES - N_HEADS_B, 1)
        den = term if den is None else den + term
    heads_b = None
    for e, o in zip(es, parts):
        w = jnp.where(is_max, e / den, 0.0)
        hi = w.astype(BF16)
        lo = (w - hi.astype(F32)).astype(BF16)
        term = (_dot(hi, ex_ref[...]) + _dot(lo, ex_ref[...])) * o
        heads_b = term if heads_b is None else heads_b + term

    mixed = _dot(ha_ref[...], wo_ref[0:A_Q, :]) + _dot(heads_b.astype(BF16), wo_ref[A_Q:, :])
    x2 = x_ref[...] + _rms(mixed, mixg_ref[...])
    y_ref[...] = _swiglu_half_step(x2, pre_ref[...], post_ref[...], wg_ref, wu_ref, wd_ref)


def _out_ffn_call(x1, heads_a, parts, expand, wo, mix_g, pre_g, post_g, wg, wu, wd):
    n = x1.shape[0]
    per_batch = parts[0][0].shape[2] // TM
    row = lambda c: pl.BlockSpec((TM, c), lambda i: (i, 0))

    def strided(a):
        _, d, _, c = a.shape
        return pl.BlockSpec((None, d, TM // d, c), lambda i: (i // per_batch, 0, i % per_batch, 0))

    (o1, s1), (o2, s2), (o3, s3) = parts
    return pl.pallas_call(
        _out_ffn_kernel,
        grid=(n // TM,),
        in_specs=[row(D_MODEL), row(A_Q), strided(o1), strided(o2), strided(o3),
                  strided(s1), strided(s2), strided(s3), _resident(expand.shape),
                  _resident(wo.shape), _resident((1, D_MODEL)), _resident((1, D_MODEL)),
                  _resident((1, D_MODEL)), _resident(wg.shape), _resident(wu.shape),
                  _resident(wd.shape)],
        out_specs=row(D_MODEL),
        out_shape=jax.ShapeDtypeStruct((n, D_MODEL), F32),
        scratch_shapes=[pltpu.VMEM((B_QKV // LANES, TM, LANES), F32),
                        pltpu.VMEM((B_QKV // LANES, TM, LANES), F32),
                        pltpu.VMEM((1, TM, LANES), F32), pltpu.VMEM((1, TM, LANES), F32)],
        compiler_params=pltpu.CompilerParams(
            dimension_semantics=("arbitrary",), vmem_limit_bytes=VMEM_LIMIT),
        name="out_ffn2",
    )(x1, heads_a, o1, o2, o3, s1, s2, s3, expand, wo, mix_g, pre_g, post_g, wg, wu, wd)


def _rope_tables(seq):
    pos = jnp.arange(seq, dtype=jnp.int32)
    row = (pos // GRID_W).astype(F32)[:, None]
    col = (pos % GRID_W).astype(F32)[:, None]
    dim_a = HEAD_DIM // 2
    fa = ROPE_THETA ** (-jnp.arange(0, dim_a, 2, dtype=F32) / dim_a)
    fb = ROPE_THETA ** (-jnp.arange(0, HEAD_DIM, 2, dtype=F32) / HEAD_DIM)
    ar, ac = row * fa[None, :], col * fa[None, :]
    ab = pos.astype(F32)[:, None] * fb[None, :]
    cos_a = jnp.concatenate([jnp.cos(ar), jnp.cos(ar), jnp.cos(ac), jnp.cos(ac)], axis=-1)
    sin_a = jnp.concatenate([-jnp.sin(ar), jnp.sin(ar), -jnp.sin(ac), jnp.sin(ac)], axis=-1)
    cos_b = jnp.concatenate([jnp.cos(ab), jnp.cos(ab)], axis=-1)
    sin_b = jnp.concatenate([-jnp.sin(ab), jnp.sin(ab)], axis=-1)
    two = lambda t: jnp.tile(t, (1, LANES // HEAD_DIM))
    return two(cos_a), two(sin_a), two(cos_b), two(sin_b)


def _layer(x, ffn1_pre_g, ffn1_post_g, ffn1_w_gate, ffn1_w_up, ffn1_w_down,
           mix_pre_g, mix_post_g, w_qkv, q_norm_g, k_norm_g, w_out,
           ffn2_pre_g, ffn2_post_g, ffn2_w_gate, ffn2_w_up, ffn2_w_down, tables, seg, expand):
    B, S, D = x.shape
    vec = lambda g: g.reshape(1, -1).astype(F32)
    two = lambda g: jnp.tile(vec(g), (1, LANES // HEAD_DIM))
    bf = lambda w: w.astype(BF16)

    x1 = _ffn_call(x.reshape(B * S, D), vec(ffn1_pre_g), vec(ffn1_post_g),
                   bf(ffn1_w_gate), bf(ffn1_w_up), bf(ffn1_w_down))
    qa, ka, vta, q_top, k_top, qkv_b = _qkv_call(x1.reshape(B, S, D), vec(mix_pre_g), bf(w_qkv),
                                                 seg, two(q_norm_g), two(k_norm_g), *tables)
    k_norm = jnp.sqrt(jnp.max(k_top, axis=(1, 2)))
    kmax = k_norm[:, ::HEAD_DIM]
    score_bound = jnp.sqrt(jnp.max(q_top)) * jnp.max(kmax)
    heads_a = lax.cond(score_bound <= MAX_SCORE_BOUND,
                       lambda: _attn_a_bounded_call(qa, ka, vta, kmax),
                       lambda: _attn_a_call(qa, ka, vta))
    parts = [_attn_b_call(*qkv) for qkv in qkv_b]
    y = _out_ffn_call(x1, heads_a.reshape(B * S, A_Q), parts, expand,
                      bf(w_out), vec(mix_post_g), vec(ffn2_pre_g), vec(ffn2_post_g),
                      bf(ffn2_w_gate), bf(ffn2_w_up), bf(ffn2_w_down))
    return y.reshape(B, S, D)


def kernel(x, ffn1_pre_g, ffn1_post_g, ffn1_w_gate, ffn1_w_up, ffn1_w_down, mix_pre_g, mix_post_g, w_qkv, q_norm_g, k_norm_g, w_out, ffn2_pre_g, ffn2_post_g, ffn2_w_gate, ffn2_w_up, ffn2_w_down):
    assert all(w // 2 // d == SPAN_B for w, d in DILATED_CONFIGS) and TK_A % TM == 0
    S = x.shape[1]
    tables = _rope_tables(S)
    head_of_lane = jnp.arange(LANES) // HEAD_DIM
    seg = (head_of_lane[:, None] == head_of_lane[None, :]).astype(BF16) / HEAD_DIM
    expand = (jnp.arange(LANES)[:, None] == jnp.arange(B_QKV)[None, :] // HEAD_DIM).astype(BF16)
    params = (ffn1_pre_g, ffn1_post_g, ffn1_w_gate, ffn1_w_up, ffn1_w_down, mix_pre_g, mix_post_g,
              w_qkv, q_norm_g, k_norm_g, w_out, ffn2_pre_g, ffn2_post_g, ffn2_w_gate, ffn2_w_up,
              ffn2_w_down)
    for l in range(ffn1_pre_g.shape[0]):
        x = _layer(x, *(p[l] for p in params), tables, seg, expand)
    return x
```

```python
import functools

import jax
import jax.numpy as jnp
from jax import lax
from jax.experimental import pallas as pl
from jax.experimental.pallas import tpu as pltpu

D_MODEL = 1024
HEAD_DIM = 64
N_HEADS_A = 8
N_KV_A = 2
N_HEADS_B = 8
DILATED_CONFIGS = ((128, 1), (512, 4), (2048, 16))
DILATIONS = tuple(d for _, d in DILATED_CONFIGS)
GRID_W = 64
ROPE_THETA = 10000.0
D_FF = 2816
EPS = 1e-6
NEG_INF = -1e30

A_Q = N_HEADS_A * HEAD_DIM
A_KV = N_KV_A * HEAD_DIM
B_QKV = N_HEADS_B * HEAD_DIM
QKV_COLS = A_Q + 2 * A_KV + 3 * B_QKV
SCALE = HEAD_DIM ** -0.5
LOG2E = 1.4426950408889634

LANES = 128
FF_CHUNKS = ((0, 1024), (1024, 2048), (2048, 2816))
TM = 512
TQ_A = 256
TK_A = 1024
SUB_A = 256
MAX_SCORE_BOUND = 48.0
VT_ROWS = HEAD_DIM + 16
TQ_B = 512
SUB_B = 128
SPAN_B = 64
VMEM_LIMIT = 52 * 1024 * 1024

BF16 = jnp.bfloat16
F32 = jnp.float32


def _dot(a, b):
    return jnp.dot(a, b, preferred_element_type=F32)


def _dot_nt(a, b):
    return lax.dot_general(a, b, (((1,), (1,)), ((), ())), preferred_element_type=F32)


def _rms(x, g):
    ms = jnp.mean(x * x, axis=-1, keepdims=True)
    return x * lax.rsqrt(ms + EPS) * g


def _swiglu_half_step(x, pre_g, post_g, wg_ref, wu_ref, wd_ref):
    h = _rms(x, pre_g).astype(BF16)
    f = None
    for lo, hi in FF_CHUNKS:
        g = _dot(h, wg_ref[:, lo:hi])
        u = _dot(h, wu_ref[:, lo:hi])
        a = (g / (1.0 + jnp.exp(-g)) * u).astype(BF16)
        part = _dot(a, wd_ref[lo:hi, :])
        f = part if f is None else f + part
    return x + 0.5 * _rms(f, post_g)


def _ffn_kernel(x_ref, pre_ref, post_ref, wg_ref, wu_ref, wd_ref, o_ref):
    o_ref[...] = _swiglu_half_step(x_ref[...], pre_ref[...], post_ref[...],
                                   wg_ref, wu_ref, wd_ref)


def _resident(shape):
    nd = len(shape)
    return pl.BlockSpec(shape, lambda *_: (0,) * nd, pipeline_mode=pl.Buffered(1))


def _ffn_call(x2d, pre_g, post_g, wg, wu, wd):
    n = x2d.shape[0]
    row = pl.BlockSpec((TM, D_MODEL), lambda i: (i, 0))
    return pl.pallas_call(
        _ffn_kernel,
        grid=(n // TM,),
        in_specs=[row, _resident((1, D_MODEL)), _resident((1, D_MODEL)),
                  _resident(wg.shape), _resident(wu.shape), _resident(wd.shape)],
        out_specs=row,
        out_shape=jax.ShapeDtypeStruct(x2d.shape, F32),
        compiler_params=pltpu.CompilerParams(
            dimension_semantics=("arbitrary",), vmem_limit_bytes=VMEM_LIMIT),
        name="ffn1",
    )(x2d, pre_g, post_g, wg, wu, wd)


def _rope(c, cos, sin_signed, half):
    lane = lax.broadcasted_iota(jnp.int32, c.shape, 1)
    first = (lane % (2 * half)) < half
    partner = jnp.where(first, pltpu.roll(c, LANES - half, 1), pltpu.roll(c, half, 1))
    return c * cos + partner * sin_signed


def _head_rms(c, seg_ref, g):
    sq = c * c
    hi = sq.astype(BF16)
    lo = (sq - hi.astype(F32)).astype(BF16)
    ms = _dot(hi, seg_ref[...]) + _dot(lo, seg_ref[...])
    return c * lax.rsqrt(ms + EPS) * g


def _dup_halves(c):
    lane = lax.broadcasted_iota(jnp.int32, c.shape, 1)
    low = lane < HEAD_DIM
    r = pltpu.roll(c, HEAD_DIM, 1)
    return jnp.where(low, c, r), jnp.where(low, r, c)


def _qkv_kernel(x_ref, pre_ref, w_ref, seg_ref, qg_ref, kg_ref,
                cosa_ref, sina_ref, cosb_ref, sinb_ref,
                qa_ref, ka_ref, va_ref, *rest):
    b_refs, (qf_sc, kf_sc, vf_sc) = rest[:-3], rest[-3:]
    h = _rms(x_ref[...], pre_ref[...]).astype(BF16)
    cosa, sina = cosa_ref[...], sina_ref[...]
    cosb, sinb = cosb_ref[...], sinb_ref[...]
    wide = 2 * LANES

    qa = _dot(h, w_ref[:, 0:A_Q])
    for t in range(A_Q // wide):
        qn = _head_rms(qa[:, t * wide:(t + 1) * wide], seg_ref, qg_ref[...])
        for c in range(wide // LANES):
            q = _rope(qn[:, c * LANES:(c + 1) * LANES], cosa, sina, HEAD_DIM // 4) * (SCALE * LOG2E)
            col = t * wide + c * LANES
            qa_ref[:, col:col + LANES] = q.astype(BF16)

    kv = _dot(h, w_ref[:, A_Q:A_Q + 2 * A_KV])
    k = _rope(_head_rms(kv, seg_ref, kg_ref[...])[:, 0:A_KV], cosa, sina, HEAD_DIM // 4)
    k0, k1 = _dup_halves(k)
    ka_ref[0] = k0.astype(BF16)
    ka_ref[1] = k1.astype(BF16)
    vt = kv[:, A_KV:].T.astype(BF16)
    ones = jnp.ones((VT_ROWS - HEAD_DIM, TM), BF16)
    for g in range(N_KV_A):
        va_ref[g, 0:HEAD_DIM, :] = vt[g * HEAD_DIM:(g + 1) * HEAD_DIM, :]
        va_ref[g, HEAD_DIM:, :] = ones

    base = A_Q + 2 * A_KV
    qb = _dot(h, w_ref[:, base:base + B_QKV])
    kb = _dot(h, w_ref[:, base + B_QKV:base + 2 * B_QKV])
    vb = _dot(h, w_ref[:, base + 2 * B_QKV:base + 3 * B_QKV])
    for c in range(B_QKV // LANES):
        sl = slice(c * LANES, (c + 1) * LANES)
        qf_sc[c] = _rope(qb[:, sl], cosb, sinb, HEAD_DIM // 2) * (SCALE * LOG2E)
        kf_sc[c] = _rope(kb[:, sl], cosb, sinb, HEAD_DIM // 2)
        vf_sc[c] = vb[:, sl]
    for j, src in enumerate((qf_sc, kf_sc, vf_sc)):
        for i, d in enumerate(DILATIONS):
            out = b_refs[3 * i + j]
            for r in range(d):
                rows = slice(None) if d == 1 else pl.ds(r, TM // d, stride=d)
                for c in range(B_QKV // LANES):
                    out[r, :, c * LANES:(c + 1) * LANES] = src[c, rows, :].astype(BF16)


def _qkv_call(x1, pre_g, w, seg, qg, kg, cosa, sina, cosb, sinb):
    B, S, _ = x1.shape
    tab = pl.BlockSpec((TM, LANES), lambda i, b: (i, 0))
    wide = pl.BlockSpec((None, TM, A_Q), lambda i, b: (b, i, 0))
    dup = pl.BlockSpec((None, N_KV_A, TM, LANES), lambda i, b: (b, 0, i, 0))
    per = TK_A // TM
    vt = pl.BlockSpec((None, N_KV_A, None, VT_ROWS, TM), lambda i, b: (b, 0, i // per, 0, i % per))
    wide_shape = jax.ShapeDtypeStruct((B, S, A_Q), BF16)
    dup_shape = jax.ShapeDtypeStruct((B, N_KV_A, S, LANES), BF16)
    vt_shape = jax.ShapeDtypeStruct((B, N_KV_A, S // TK_A, VT_ROWS, TK_A), BF16)
    b_specs, b_shapes = [], []
    for d in DILATIONS:
        b_specs += [pl.BlockSpec((None, d, TM // d, B_QKV), lambda i, b: (b, 0, i, 0))] * 3
        b_shapes += [jax.ShapeDtypeStruct((B, d, S // d, B_QKV), BF16)] * 3
    outs = pl.pallas_call(
        _qkv_kernel,
        grid=(S // TM, B),
        in_specs=[pl.BlockSpec((None, TM, D_MODEL), lambda i, b: (b, i, 0)),
                  _resident((1, D_MODEL)), _resident(w.shape), _resident(seg.shape),
                  _resident(qg.shape), _resident(kg.shape), tab, tab, tab, tab],
        out_specs=[wide, dup, vt] + b_specs,
        out_shape=[wide_shape, dup_shape, vt_shape] + b_shapes,
        scratch_shapes=[pltpu.VMEM((B_QKV // LANES, TM, LANES), F32)] * 3,
        compiler_params=pltpu.CompilerParams(
            dimension_semantics=("arbitrary", "arbitrary"), vmem_limit_bytes=VMEM_LIMIT),
        name="qkv",
    )(x1, pre_g, w, seg, qg, kg, cosa, sina, cosb, sinb)
    qa, ka, vta = outs[:3]
    qkv_b = [tuple(outs[3 + 3 * i:6 + 3 * i]) for i in range(len(DILATIONS))]
    return qa, ka, vta, qkv_b


def _attn_a_kernel(q_ref, k_ref, vt_ref, o_ref, qs_sc, m_sc, acc_sc, sa_sc, sb_sc, cma_sc, cmb_sc):
    tq = q_ref.shape[0]
    n_pairs = qs_sc.shape[0]
    low = lax.broadcasted_iota(jnp.int32, (tq, LANES), 1) < HEAD_DIM
    for c in range(n_pairs):
        qc = q_ref[:, c * LANES:(c + 1) * LANES]
        zero = jnp.zeros_like(qc)
        qs_sc[c, 0:tq, :] = jnp.where(low, qc, zero)
        qs_sc[c, tq:, :] = jnp.where(low, zero, qc)

    m_sc[...] = jnp.full(m_sc.shape, NEG_INF, F32)
    acc_sc[...] = jnp.zeros(acc_sc.shape, F32)

    n_k = vt_ref.shape[0]

    def scores(kb, s_out, cm_out):
        k = k_ref[pl.ds(pl.multiple_of(kb * TK_A, TK_A), TK_A), :]
        for c in range(n_pairs):
            s = _dot_nt(k, qs_sc[c])
            s_out[c] = s
            cm_out[c] = jnp.max(s, axis=0, keepdims=True)

    def consume(kb, s_in, cm_in):
        vt = vt_ref[kb]
        for c in range(n_pairs):
            m_old = m_sc[c]
            m_new = jnp.maximum(m_old, cm_in[c])
            alpha = jnp.exp2(m_old - m_new)
            p = jnp.exp2(s_in[c] - m_new).astype(BF16)
            acc_sc[c] = alpha * acc_sc[c] + _dot(vt, p)
            m_sc[c] = m_new

    bufs = ((sa_sc, cma_sc), (sb_sc, cmb_sc))
    scores(0, *bufs[0])

    def fused(kb, cur, nxt):
        s_in, cm_in = cur
        s_out, cm_out = nxt
        k0 = pl.multiple_of((kb + 1) * TK_A, TK_A)
        m_new, alpha, pv, cm = [], [], [], []
        for c in range(n_pairs):
            m_old = m_sc[c]
            m_new.append(jnp.maximum(m_old, cm_in[c]))
            alpha.append(jnp.exp2(m_old - m_new[c]))
            pv.append(None)
            cm.append(None)
        for j in range(TK_A // SUB_A):
            rows = slice(j * SUB_A, (j + 1) * SUB_A)
            k = k_ref[pl.ds(k0 + j * SUB_A, SUB_A), :]
            vt = vt_ref[kb, :, rows]
            for c in range(n_pairs):
                s = _dot_nt(k, qs_sc[c])
                s_out[c, rows, :] = s
                part = jnp.max(s, axis=0, keepdims=True)
                cm[c] = part if cm[c] is None else jnp.maximum(cm[c], part)
            for c in range(n_pairs):
                p = jnp.exp2(s_in[c, rows, :] - m_new[c]).astype(BF16)
                part = _dot(vt, p)
                pv[c] = part if pv[c] is None else pv[c] + part
        for c in range(n_pairs):
            cm_out[c] = cm[c]
            acc_sc[c] = alpha[c] * acc_sc[c] + pv[c]
            m_sc[c] = m_new[c]

    def body(kb, carry):
        for parity in range(2):
            @pl.when(kb % 2 == parity)
            def _():
                fused(kb, bufs[parity], bufs[1 - parity])
        return carry

    lax.fori_loop(0, n_k - 1, body, 0)
    consume(n_k - 1, *bufs[(n_k - 1) % 2])
    for c in range(n_pairs):
        o = acc_sc[c, 0:HEAD_DIM, :] / acc_sc[c, HEAD_DIM:HEAD_DIM + 1, :]
        pair = jnp.concatenate([o[:, 0:tq], o[:, tq:]], axis=0)
        o_ref[:, c * LANES:(c + 1) * LANES] = pair.T.astype(o_ref.dtype)


def _attn_a_bounded_kernel(q_ref, k_ref, vt_ref, o_ref, qs_sc, acc_sc):
    tq = q_ref.shape[0]
    n_pairs = qs_sc.shape[0]
    low = lax.broadcasted_iota(jnp.int32, (tq, LANES), 1) < HEAD_DIM
    for c in range(n_pairs):
        qc = q_ref[:, c * LANES:(c + 1) * LANES]
        zero = jnp.zeros_like(qc)
        qs_sc[c, 0:tq, :] = jnp.where(low, qc, zero)
        qs_sc[c, tq:, :] = jnp.where(low, zero, qc)
    n_sub = TK_A // SUB_A
    n_pieces = vt_ref.shape[0] * n_sub

    def scores(t):
        k = k_ref[t * SUB_A:(t + 1) * SUB_A, :]
        return [_dot_nt(k, qs_sc[c]) for c in range(n_pairs)]

    pv = [None] * n_pairs
    s_cur = scores(0)
    for t in range(n_pieces):
        s_nxt = scores(t + 1) if t + 1 < n_pieces else None
        vt = vt_ref[t // n_sub, :, (t % n_sub) * SUB_A:(t % n_sub + 1) * SUB_A]
        for c in range(n_pairs):
            part = _dot(vt, jnp.exp2(s_cur[c]).astype(BF16))
            pv[c] = part if pv[c] is None else pv[c] + part
        s_cur = s_nxt
    for c in range(n_pairs):
        acc_sc[c] = pv[c]
    for c in range(n_pairs):
        o = acc_sc[c, 0:HEAD_DIM, :] / acc_sc[c, HEAD_DIM:HEAD_DIM + 1, :]
        pair = jnp.concatenate([o[:, 0:tq], o[:, tq:]], axis=0)
        o_ref[:, c * LANES:(c + 1) * LANES] = pair.T.astype(o_ref.dtype)


def _attn_a_bounded_call(qa, ka, vta):
    B, S, _ = qa.shape
    qcols = A_Q // N_KV_A
    pairs = N_HEADS_A // N_KV_A // 2
    qspec = pl.BlockSpec((None, TQ_A, qcols), lambda b, g, i: (b, i, g))
    kspec = pl.BlockSpec((None, None, S, LANES), lambda b, g, i: (b, g, 0, 0))
    vspec = pl.BlockSpec((None, None, S // TK_A, VT_ROWS, TK_A), lambda b, g, i: (b, g, 0, 0, 0))
    return pl.pallas_call(
        _attn_a_bounded_kernel,
        grid=(B, N_KV_A, S // TQ_A),
        in_specs=[qspec, kspec, vspec],
        out_specs=qspec,
        out_shape=jax.ShapeDtypeStruct((B, S, A_Q), BF16),
        scratch_shapes=[pltpu.VMEM((pairs, 2 * TQ_A, LANES), BF16),
                        pltpu.VMEM((pairs, VT_ROWS, 2 * TQ_A), F32)],
        compiler_params=pltpu.CompilerParams(
            dimension_semantics=("arbitrary", "arbitrary", "arbitrary"),
            vmem_limit_bytes=VMEM_LIMIT),
        name="attn_a_bounded",
    )(qa, ka, vta)


def _attn_a_call(qa, ka, vta):
    B, S, _ = qa.shape
    qcols = A_Q // N_KV_A
    pairs = N_HEADS_A // N_KV_A // 2
    qspec = pl.BlockSpec((None, TQ_A, qcols), lambda b, g, i: (b, i, g))
    kspec = pl.BlockSpec((None, None, S, LANES), lambda b, g, i: (b, g, 0, 0))
    vspec = pl.BlockSpec((None, None, S // TK_A, VT_ROWS, TK_A), lambda b, g, i: (b, g, 0, 0, 0))
    return pl.pallas_call(
        _attn_a_kernel,
        grid=(B, N_KV_A, S // TQ_A),
        in_specs=[qspec, kspec, vspec],
        out_specs=qspec,
        out_shape=jax.ShapeDtypeStruct((B, S, A_Q), BF16),
        scratch_shapes=[pltpu.VMEM((pairs, 2 * TQ_A, LANES), BF16),
                        pltpu.VMEM((pairs, 1, 2 * TQ_A), F32),
                        pltpu.VMEM((pairs, VT_ROWS, 2 * TQ_A), F32),
                        pltpu.VMEM((pairs, TK_A, 2 * TQ_A), F32),
                        pltpu.VMEM((pairs, TK_A, 2 * TQ_A), F32),
                        pltpu.VMEM((pairs, 1, 2 * TQ_A), F32),
                        pltpu.VMEM((pairs, 1, 2 * TQ_A), F32)],
        compiler_params=pltpu.CompilerParams(
            dimension_semantics=("arbitrary", "arbitrary", "arbitrary"),
            vmem_limit_bytes=VMEM_LIMIT),
        name="attn_a",
    )(qa, ka, vta)


def _attn_b_kernel(q_ref, kp_ref, kc_ref, kn_ref, vp_ref, vc_ref, vn_ref,
                   o_ref, st_ref, kbuf, vbuf, *, seq_len):
    j = pl.program_id(2)
    kbuf[0:SPAN_B, :] = kp_ref[...]
    kbuf[SPAN_B:SPAN_B + TQ_B, :] = kc_ref[...]
    kbuf[SPAN_B + TQ_B:, :] = kn_ref[...]
    vbuf[0:SPAN_B, :] = vp_ref[...]
    vbuf[SPAN_B:SPAN_B + TQ_B, :] = vc_ref[...]
    vbuf[SPAN_B + TQ_B:, :] = vn_ref[...]

    n_keys = SUB_B + 2 * SPAN_B
    c_idx = lax.broadcasted_iota(jnp.int32, (n_keys, 2 * SUB_B), 0)
    a_idx = lax.broadcasted_iota(jnp.int32, (n_keys, 2 * SUB_B), 1) % SUB_B
    band = jnp.abs(c_idx - SPAN_B - a_idx) <= SPAN_B
    low = lax.broadcasted_iota(jnp.int32, (SUB_B, LANES), 1) < HEAD_DIM
    ones = jnp.ones((VT_ROWS - HEAD_DIM, n_keys), BF16)
    pad = jnp.zeros((LANES - 2 * N_HEADS_B, SUB_B), F32)

    def scores_of(i):
        q0 = i * SUB_B
        kpos = j * TQ_B + q0 - SPAN_B + c_idx
        valid = band & (kpos >= 0) & (kpos < seq_len)
        bias = jnp.where(valid, 0.0, NEG_INF).astype(F32)
        out = []
        for hp in range(N_HEADS_B // 2):
            cols = slice(hp * LANES, (hp + 1) * LANES)
            qp = q_ref[q0:q0 + SUB_B, cols]
            zero = jnp.zeros_like(qp)
            qs = jnp.concatenate([jnp.where(low, qp, zero), jnp.where(low, zero, qp)], axis=0)
            out.append(_dot_nt(kbuf[q0:q0 + n_keys, cols], qs) + bias)
        return out

    def finish(i, scores):
        q0 = i * SUB_B
        ms, ls = [], []
        for hp in range(N_HEADS_B // 2):
            cols = slice(hp * LANES, (hp + 1) * LANES)
            vw = vbuf[q0:q0 + n_keys, cols]
            s = scores[hp]
            m = jnp.max(s, axis=0, keepdims=True)
            p = jnp.exp2(s - m).astype(BF16)
            vt = jnp.concatenate([vw.T, ones], axis=0)
            o_all = _dot(vt, p)
            o_t = jnp.concatenate([o_all[0:HEAD_DIM, 0:SUB_B],
                                   o_all[HEAD_DIM:LANES, SUB_B:]], axis=0)
            o_ref[q0:q0 + SUB_B, cols] = o_t.T.astype(o_ref.dtype)
            l = o_all[LANES:LANES + 1, :]
            ms += [m[:, 0:SUB_B], m[:, SUB_B:]]
            ls += [l[:, 0:SUB_B], l[:, SUB_B:]]
        st_t = jnp.concatenate(ms + ls + [pad], axis=0)
        st_ref[q0:q0 + SUB_B, :] = st_t.T

    n_sub = TQ_B // SUB_B
    scores = scores_of(0)
    for i in range(n_sub):
        nxt = scores_of(i + 1) if i + 1 < n_sub else None
        finish(i, scores)
        scores = nxt


def _attn_b_call(qb, kb, vb):
    B, d, L, C = qb.shape
    nh = L // SPAN_B
    per = TQ_B // SPAN_B
    cur = pl.BlockSpec((None, None, TQ_B, C), lambda b, r, j: (b, r, j, 0))
    prev = pl.BlockSpec((None, None, SPAN_B, C),
                        lambda b, r, j: (b, r, jnp.maximum(j * per - 1, 0), 0))
    nxt = pl.BlockSpec((None, None, SPAN_B, C),
                       lambda b, r, j: (b, r, jnp.minimum((j + 1) * per, nh - 1), 0))
    st_spec = pl.BlockSpec((None, None, TQ_B, LANES), lambda b, r, j: (b, r, j, 0))
    return pl.pallas_call(
        functools.partial(_attn_b_kernel, seq_len=L),
        grid=(B, d, L // TQ_B),
        in_specs=[cur, prev, cur, nxt, prev, cur, nxt],
        out_specs=[cur, st_spec],
        out_shape=[jax.ShapeDtypeStruct((B, d, L, C), BF16),
                   jax.ShapeDtypeStruct((B, d, L, LANES), F32)],
        scratch_shapes=[pltpu.VMEM((TQ_B + 2 * SPAN_B, C), BF16),
                        pltpu.VMEM((TQ_B + 2 * SPAN_B, C), BF16)],
        compiler_params=pltpu.CompilerParams(
            dimension_semantics=("arbitrary", "arbitrary", "arbitrary"),
            vmem_limit_bytes=VMEM_LIMIT),
        name=f"attn_b_d{d}",
    )(qb, kb, kb, kb, vb, vb, vb)


def _natural_order(ref, scratch):
    d, _, cols = ref.shape
    if d == 1:
        return ref[0].astype(F32)
    for r in range(d):
        for c in range(cols // LANES):
            scratch[c, pl.ds(r, TM // d, stride=d), :] = ref[r, :, c * LANES:(c + 1) * LANES].astype(F32)
    return jnp.concatenate([scratch[c] for c in range(cols // LANES)], axis=1)


def _out_ffn_kernel(x_ref, ha_ref, o1_ref, o2_ref, o3_ref, s1_ref, s2_ref, s3_ref, ex_ref,
                    wo_ref, mixg_ref, pre_ref, post_ref, wg_ref, wu_ref, wd_ref, y_ref,
                    of2_sc, of3_sc, sf2_sc, sf3_sc):
    stats = (_natural_order(s1_ref, None), _natural_order(s2_ref, sf2_sc),
             _natural_order(s3_ref, sf3_sc))
    parts = (_natural_order(o1_ref, None), _natural_order(o2_ref, of2_sc),
             _natural_order(o3_ref, of3_sc))
    is_max = lax.broadcasted_iota(jnp.int32, (TM, LANES), 1) < N_HEADS_B
    m_all = jnp.maximum(jnp.maximum(stats[0], stats[1]), stats[2])
    es = [jnp.exp2(s - m_all) for s in stats]
    den = None
    for e, s in zip(es, stats):
        term = e * pltpu.roll(s, LANES - N_HEADS_B, 1)
        den = term if den is None else den + term
    heads_b = None
    for e, o in zip(es, parts):
        w = jnp.where(is_max, e / den, 0.0)
        hi = w.astype(BF16)
        lo = (w - hi.astype(F32)).astype(BF16)
        term = (_dot(hi, ex_ref[...]) + _dot(lo, ex_ref[...])) * o
        heads_b = term if heads_b is None else heads_b + term

    mixed = _dot(ha_ref[...], wo_ref[0:A_Q, :]) + _dot(heads_b.astype(BF16), wo_ref[A_Q:, :])
    x2 = x_ref[...] + _rms(mixed, mixg_ref[...])
    y_ref[...] = _swiglu_half_step(x2, pre_ref[...], post_ref[...], wg_ref, wu_ref, wd_ref)


def _out_ffn_call(x1, heads_a, parts, expand, wo, mix_g, pre_g, post_g, wg, wu, wd):
    n = x1.shape[0]
    per_batch = parts[0][0].shape[2] // TM
    row = lambda c: pl.BlockSpec((TM, c), lambda i: (i, 0))

    def strided(a):
        _, d, _, c = a.shape
        return pl.BlockSpec((None, d, TM // d, c), lambda i: (i // per_batch, 0, i % per_batch, 0))

    (o1, s1), (o2, s2), (o3, s3) = parts
    return pl.pallas_call(
        _out_ffn_kernel,
        grid=(n // TM,),
        in_specs=[row(D_MODEL), row(A_Q), strided(o1), strided(o2), strided(o3),
                  strided(s1), strided(s2), strided(s3), _resident(expand.shape),
                  _resident(wo.shape), _resident((1, D_MODEL)), _resident((1, D_MODEL)),
                  _resident((1, D_MODEL)), _resident(wg.shape), _resident(wu.shape),
                  _resident(wd.shape)],
        out_specs=row(D_MODEL),
        out_shape=jax.ShapeDtypeStruct((n, D_MODEL), F32),
        scratch_shapes=[pltpu.VMEM((B_QKV // LANES, TM, LANES), F32),
                        pltpu.VMEM((B_QKV // LANES, TM, LANES), F32),
                        pltpu.VMEM((1, TM, LANES), F32), pltpu.VMEM((1, TM, LANES), F32)],
        compiler_params=pltpu.CompilerParams(
            dimension_semantics=("arbitrary",), vmem_limit_bytes=VMEM_LIMIT),
        name="out_ffn2",
    )(x1, heads_a, o1, o2, o3, s1, s2, s3, expand, wo, mix_g, pre_g, post_g, wg, wu, wd)


def _rope_tables(seq):
    pos = jnp.arange(seq, dtype=jnp.int32)
    row = (pos // GRID_W).astype(F32)[:, None]
    col = (pos % GRID_W).astype(F32)[:, None]
    dim_a = HEAD_DIM // 2
    fa = ROPE_THETA ** (-jnp.arange(0, dim_a, 2, dtype=F32) / dim_a)
    fb = ROPE_THETA ** (-jnp.arange(0, HEAD_DIM, 2, dtype=F32) / HEAD_DIM)
    ar, ac = row * fa[None, :], col * fa[None, :]
    ab = pos.astype(F32)[:, None] * fb[None, :]
    cos_a = jnp.concatenate([jnp.cos(ar), jnp.cos(ar), jnp.cos(ac), jnp.cos(ac)], axis=-1)
    sin_a = jnp.concatenate([-jnp.sin(ar), jnp.sin(ar), -jnp.sin(ac), jnp.sin(ac)], axis=-1)
    cos_b = jnp.concatenate([jnp.cos(ab), jnp.cos(ab)], axis=-1)
    sin_b = jnp.concatenate([-jnp.sin(ab), jnp.sin(ab)], axis=-1)
    two = lambda t: jnp.tile(t, (1, LANES // HEAD_DIM))
    return two(cos_a), two(sin_a), two(cos_b), two(sin_b)


def _layer(x, ffn1_pre_g, ffn1_post_g, ffn1_w_gate, ffn1_w_up, ffn1_w_down,
           mix_pre_g, mix_post_g, w_qkv, q_norm_g, k_norm_g, w_out,
           ffn2_pre_g, ffn2_post_g, ffn2_w_gate, ffn2_w_up, ffn2_w_down, tables, seg, expand):
    B, S, D = x.shape
    vec = lambda g: g.reshape(1, -1).astype(F32)
    bf = lambda w: w.astype(BF16)

    x1 = _ffn_call(x.reshape(B * S, D), vec(ffn1_pre_g), vec(ffn1_post_g),
                   bf(ffn1_w_gate), bf(ffn1_w_up), bf(ffn1_w_down))
    heads_per_tile = 2 * LANES // HEAD_DIM
    kv_gain = jnp.concatenate([jnp.tile(vec(k_norm_g), (1, N_KV_A)), jnp.ones((1, A_KV), F32)], axis=1)
    qa, ka, vta, qkv_b = _qkv_call(x1.reshape(B, S, D), vec(mix_pre_g), bf(w_qkv), seg,
                                   jnp.tile(vec(q_norm_g), (1, heads_per_tile)), kv_gain, *tables)
    score_bound = ((HEAD_DIM * SCALE * LOG2E) * jnp.max(jnp.abs(q_norm_g))
                   * jnp.max(jnp.abs(k_norm_g)))
    heads_a = lax.cond(score_bound <= MAX_SCORE_BOUND,
                       lambda: _attn_a_bounded_call(qa, ka, vta),
                       lambda: _attn_a_call(qa, ka, vta))
    parts = [_attn_b_call(*qkv) for qkv in qkv_b]
    y = _out_ffn_call(x1, heads_a.reshape(B * S, A_Q), parts, expand,
                      bf(w_out), vec(mix_post_g), vec(ffn2_pre_g), vec(ffn2_post_g),
                      bf(ffn2_w_gate), bf(ffn2_w_up), bf(ffn2_w_down))
    return y.reshape(B, S, D)


def kernel(x, ffn1_pre_g, ffn1_post_g, ffn1_w_gate, ffn1_w_up, ffn1_w_down, mix_pre_g, mix_post_g, w_qkv, q_norm_g, k_norm_g, w_out, ffn2_pre_g, ffn2_post_g, ffn2_w_gate, ffn2_w_up, ffn2_w_down):
    assert all(w // 2 // d == SPAN_B for w, d in DILATED_CONFIGS) and TK_A % TM == 0
    S = x.shape[1]
    tables = _rope_tables(S)
    head_of_lane = jnp.arange(2 * LANES) // HEAD_DIM
    seg = (head_of_lane[:, None] == head_of_lane[None, :]).astype(BF16) / HEAD_DIM
    expand = (jnp.arange(LANES)[:, None] == jnp.arange(B_QKV)[None, :] // HEAD_DIM).astype(BF16)
    params = (ffn1_pre_g, ffn1_post_g, ffn1_w_gate, ffn1_w_up, ffn1_w_down, mix_pre_g, mix_post_g,
              w_qkv, q_norm_g, k_norm_g, w_out, ffn2_pre_g, ffn2_post_g, ffn2_w_gate, ffn2_w_up,
              ffn2_w_down)
    for l in range(ffn1_pre_g.shape[0]):
        x = _layer(x, *(p[l] for p in params), tables, seg, expand)
    return x
```

```python
import functools

import jax
import jax.numpy as jnp
from jax import lax
from jax.experimental import pallas as pl
from jax.experimental.pallas import tpu as pltpu

D_MODEL = 1024
HEAD_DIM = 64
N_HEADS_A = 8
N_KV_A = 2
N_HEADS_B = 8
DILATED_CONFIGS = ((128, 1), (512, 4), (2048, 16))
DILATIONS = tuple(d for _, d in DILATED_CONFIGS)
GRID_W = 64
ROPE_THETA = 10000.0
D_FF = 2816
EPS = 1e-6
NEG_INF = -1e30

A_Q = N_HEADS_A * HEAD_DIM
A_KV = N_KV_A * HEAD_DIM
B_QKV = N_HEADS_B * HEAD_DIM
QKV_COLS = A_Q + 2 * A_KV + 3 * B_QKV
SCALE = HEAD_DIM ** -0.5
LOG2E = 1.4426950408889634

LANES = 128
FF_CHUNKS = ((0, 1024), (1024, 2048), (2048, 2816))
TM = 512
TQ_A = 256
TK_A = 1024
SUB_A = 256
MAX_SCORE_BOUND = 48.0
VT_ROWS = HEAD_DIM + 16
TQ_B = 512
SUB_B = 128
SPAN_B = 64
VMEM_LIMIT = 52 * 1024 * 1024

BF16 = jnp.bfloat16
F32 = jnp.float32


def _dot(a, b):
    return jnp.dot(a, b, preferred_element_type=F32)


def _dot_nt(a, b):
    return lax.dot_general(a, b, (((1,), (1,)), ((), ())), preferred_element_type=F32)


def _rms(x, g):
    ms = jnp.mean(x * x, axis=-1, keepdims=True)
    return x * lax.rsqrt(ms + EPS) * g


def _swiglu_half_step(x, pre_g, post_g, wg_ref, wu_ref, wd_ref):
    h = _rms(x, pre_g).astype(BF16)
    f = None
    for lo, hi in FF_CHUNKS:
        g = _dot(h, wg_ref[:, lo:hi])
        u = _dot(h, wu_ref[:, lo:hi])
        a = (g / (1.0 + jnp.exp(-g)) * u).astype(BF16)
        part = _dot(a, wd_ref[lo:hi, :])
        f = part if f is None else f + part
    return x + 0.5 * _rms(f, post_g)


def _ffn_kernel(x_ref, pre_ref, post_ref, wg_ref, wu_ref, wd_ref, o_ref):
    o_ref[...] = _swiglu_half_step(x_ref[...], pre_ref[...], post_ref[...],
                                   wg_ref, wu_ref, wd_ref)


def _resident(shape):
    nd = len(shape)
    return pl.BlockSpec(shape, lambda *_: (0,) * nd, pipeline_mode=pl.Buffered(1))


def _ffn_call(x2d, pre_g, post_g, wg, wu, wd):
    n = x2d.shape[0]
    row = pl.BlockSpec((TM, D_MODEL), lambda i: (i, 0))
    return pl.pallas_call(
        _ffn_kernel,
        grid=(n // TM,),
        in_specs=[row, _resident((1, D_MODEL)), _resident((1, D_MODEL)),
                  _resident(wg.shape), _resident(wu.shape), _resident(wd.shape)],
        out_specs=row,
        out_shape=jax.ShapeDtypeStruct(x2d.shape, F32),
        compiler_params=pltpu.CompilerParams(
            dimension_semantics=("arbitrary",), vmem_limit_bytes=VMEM_LIMIT),
        name="ffn1",
    )(x2d, pre_g, post_g, wg, wu, wd)


def _rope(c, cos, sin_signed, half):
    lane = lax.broadcasted_iota(jnp.int32, c.shape, 1)
    first = (lane % (2 * half)) < half
    partner = jnp.where(first, pltpu.roll(c, LANES - half, 1), pltpu.roll(c, half, 1))
    return c * cos + partner * sin_signed


def _head_rms(c, seg_ref, g):
    sq = c * c
    hi = sq.astype(BF16)
    lo = (sq - hi.astype(F32)).astype(BF16)
    ms = _dot(hi, seg_ref[...]) + _dot(lo, seg_ref[...])
    return c * lax.rsqrt(ms + EPS) * g


def _dup_halves(c):
    lane = lax.broadcasted_iota(jnp.int32, c.shape, 1)
    low = lane < HEAD_DIM
    r = pltpu.roll(c, HEAD_DIM, 1)
    return jnp.where(low, c, r), jnp.where(low, r, c)


def _qkv_kernel(x_ref, pre_ref, w_ref, seg_ref, qg_ref, kg_ref,
                cosa_ref, sina_ref, cosb_ref, sinb_ref,
                qa_ref, ka_ref, va_ref, *rest):
    b_refs, (qf_sc, kf_sc, vf_sc) = rest[:-3], rest[-3:]
    h = _rms(x_ref[...], pre_ref[...]).astype(BF16)
    cosa, sina = cosa_ref[...], sina_ref[...]
    cosb, sinb = cosb_ref[...], sinb_ref[...]
    wide = 2 * LANES

    qa = _dot(h, w_ref[:, 0:A_Q])
    for t in range(A_Q // wide):
        qn = _head_rms(qa[:, t * wide:(t + 1) * wide], seg_ref, qg_ref[...])
        for c in range(wide // LANES):
            q = _rope(qn[:, c * LANES:(c + 1) * LANES], cosa, sina, HEAD_DIM // 4) * (SCALE * LOG2E)
            col = t * wide + c * LANES
            qa_ref[:, col:col + LANES] = q.astype(BF16)

    kv = _dot(h, w_ref[:, A_Q:A_Q + 2 * A_KV])
    k = _rope(_head_rms(kv, seg_ref, kg_ref[...])[:, 0:A_KV], cosa, sina, HEAD_DIM // 4)
    k0, k1 = _dup_halves(k)
    ka_ref[0] = k0.astype(BF16)
    ka_ref[1] = k1.astype(BF16)
    vt = kv[:, A_KV:].T.astype(BF16)
    ones = jnp.ones((VT_ROWS - HEAD_DIM, TM), BF16)
    for g in range(N_KV_A):
        va_ref[g, 0:HEAD_DIM, :] = vt[g * HEAD_DIM:(g + 1) * HEAD_DIM, :]
        va_ref[g, HEAD_DIM:, :] = ones

    base = A_Q + 2 * A_KV
    qb = _dot(h, w_ref[:, base:base + B_QKV])
    kb = _dot(h, w_ref[:, base + B_QKV:base + 2 * B_QKV])
    vb = _dot(h, w_ref[:, base + 2 * B_QKV:base + 3 * B_QKV])
    for c in range(B_QKV // LANES):
        sl = slice(c * LANES, (c + 1) * LANES)
        qf_sc[c] = _rope(qb[:, sl], cosb, sinb, HEAD_DIM // 2) * (SCALE * LOG2E)
        kf_sc[c] = _rope(kb[:, sl], cosb, sinb, HEAD_DIM // 2)
        vf_sc[c] = vb[:, sl]
    for j, src in enumerate((qf_sc, kf_sc, vf_sc)):
        for i, d in enumerate(DILATIONS):
            out = b_refs[3 * i + j]
            for r in range(d):
                rows = slice(None) if d == 1 else pl.ds(r, TM // d, stride=d)
                for c in range(B_QKV // LANES):
                    out[r, :, c * LANES:(c + 1) * LANES] = src[c, rows, :].astype(BF16)


def _qkv_call(x1, pre_g, w, seg, qg, kg, cosa, sina, cosb, sinb):
    B, S, _ = x1.shape
    tab = pl.BlockSpec((TM, LANES), lambda i, b: (i, 0))
    wide = pl.BlockSpec((None, TM, A_Q), lambda i, b: (b, i, 0))
    dup = pl.BlockSpec((None, N_KV_A, TM, LANES), lambda i, b: (b, 0, i, 0))
    per = TK_A // TM
    vt = pl.BlockSpec((None, N_KV_A, None, VT_ROWS, TM), lambda i, b: (b, 0, i // per, 0, i % per))
    wide_shape = jax.ShapeDtypeStruct((B, S, A_Q), BF16)
    dup_shape = jax.ShapeDtypeStruct((B, N_KV_A, S, LANES), BF16)
    vt_shape = jax.ShapeDtypeStruct((B, N_KV_A, S // TK_A, VT_ROWS, TK_A), BF16)
    b_specs, b_shapes = [], []
    for d in DILATIONS:
        b_specs += [pl.BlockSpec((None, d, TM // d, B_QKV), lambda i, b: (b, 0, i, 0))] * 3
        b_shapes += [jax.ShapeDtypeStruct((B, d, S // d, B_QKV), BF16)] * 3
    outs = pl.pallas_call(
        _qkv_kernel,
        grid=(S // TM, B),
        in_specs=[pl.BlockSpec((None, TM, D_MODEL), lambda i, b: (b, i, 0)),
                  _resident((1, D_MODEL)), _resident(w.shape), _resident(seg.shape),
                  _resident(qg.shape), _resident(kg.shape), tab, tab, tab, tab],
        out_specs=[wide, dup, vt] + b_specs,
        out_shape=[wide_shape, dup_shape, vt_shape] + b_shapes,
        scratch_shapes=[pltpu.VMEM((B_QKV // LANES, TM, LANES), F32)] * 3,
        compiler_params=pltpu.CompilerParams(
            dimension_semantics=("arbitrary", "arbitrary"), vmem_limit_bytes=VMEM_LIMIT),
        name="qkv",
    )(x1, pre_g, w, seg, qg, kg, cosa, sina, cosb, sinb)
    qa, ka, vta = outs[:3]
    qkv_b = [tuple(outs[3 + 3 * i:6 + 3 * i]) for i in range(len(DILATIONS))]
    return qa, ka, vta, qkv_b


def _attn_a_kernel(q_ref, k_ref, vt_ref, o_ref, qs_sc, m_sc, acc_sc, sa_sc, sb_sc, cma_sc, cmb_sc):
    tq = q_ref.shape[0]
    n_pairs = qs_sc.shape[0]
    low = lax.broadcasted_iota(jnp.int32, (tq, LANES), 1) < HEAD_DIM
    for c in range(n_pairs):
        qc = q_ref[:, c * LANES:(c + 1) * LANES]
        zero = jnp.zeros_like(qc)
        qs_sc[c, 0:tq, :] = jnp.where(low, qc, zero)
        qs_sc[c, tq:, :] = jnp.where(low, zero, qc)

    m_sc[...] = jnp.full(m_sc.shape, NEG_INF, F32)
    acc_sc[...] = jnp.zeros(acc_sc.shape, F32)

    n_k = vt_ref.shape[0]

    def scores(kb, s_out, cm_out):
        k = k_ref[pl.ds(pl.multiple_of(kb * TK_A, TK_A), TK_A), :]
        for c in range(n_pairs):
            s = _dot_nt(k, qs_sc[c])
            s_out[c] = s
            cm_out[c] = jnp.max(s, axis=0, keepdims=True)

    def consume(kb, s_in, cm_in):
        vt = vt_ref[kb]
        for c in range(n_pairs):
            m_old = m_sc[c]
            m_new = jnp.maximum(m_old, cm_in[c])
            alpha = jnp.exp2(m_old - m_new)
            p = jnp.exp2(s_in[c] - m_new).astype(BF16)
            acc_sc[c] = alpha * acc_sc[c] + _dot(vt, p)
            m_sc[c] = m_new

    bufs = ((sa_sc, cma_sc), (sb_sc, cmb_sc))
    scores(0, *bufs[0])

    def fused(kb, cur, nxt):
        s_in, cm_in = cur
        s_out, cm_out = nxt
        k0 = pl.multiple_of((kb + 1) * TK_A, TK_A)
        m_new, alpha, pv, cm = [], [], [], []
        for c in range(n_pairs):
            m_old = m_sc[c]
            m_new.append(jnp.maximum(m_old, cm_in[c]))
            alpha.append(jnp.exp2(m_old - m_new[c]))
            pv.append(None)
            cm.append(None)
        for j in range(TK_A // SUB_A):
            rows = slice(j * SUB_A, (j + 1) * SUB_A)
            k = k_ref[pl.ds(k0 + j * SUB_A, SUB_A), :]
            vt = vt_ref[kb, :, rows]
            for c in range(n_pairs):
                s = _dot_nt(k, qs_sc[c])
                s_out[c, rows, :] = s
                part = jnp.max(s, axis=0, keepdims=True)
                cm[c] = part if cm[c] is None else jnp.maximum(cm[c], part)
            for c in range(n_pairs):
                p = jnp.exp2(s_in[c, rows, :] - m_new[c]).astype(BF16)
                part = _dot(vt, p)
                pv[c] = part if pv[c] is None else pv[c] + part
        for c in range(n_pairs):
            cm_out[c] = cm[c]
            acc_sc[c] = alpha[c] * acc_sc[c] + pv[c]
            m_sc[c] = m_new[c]

    def body(kb, carry):
        for parity in range(2):
            @pl.when(kb % 2 == parity)
            def _():
                fused(kb, bufs[parity], bufs[1 - parity])
        return carry

    lax.fori_loop(0, n_k - 1, body, 0)
    consume(n_k - 1, *bufs[(n_k - 1) % 2])
    for c in range(n_pairs):
        o = acc_sc[c, 0:HEAD_DIM, :] / acc_sc[c, HEAD_DIM:HEAD_DIM + 1, :]
        pair = jnp.concatenate([o[:, 0:tq], o[:, tq:]], axis=0)
        o_ref[:, c * LANES:(c + 1) * LANES] = pair.T.astype(o_ref.dtype)


def _attn_a_bounded_kernel(q_ref, k_ref, vt_ref, o_ref, qt_sc, acc_sc):
    tq = q_ref.shape[0]
    n_pairs = qt_sc.shape[0]
    low = lax.broadcasted_iota(jnp.int32, (tq, LANES), 1) < HEAD_DIM
    for c in range(n_pairs):
        qc = q_ref[:, c * LANES:(c + 1) * LANES]
        zero = jnp.zeros_like(qc)
        qs = jnp.concatenate([jnp.where(low, qc, zero), jnp.where(low, zero, qc)], axis=0)
        qt_sc[c] = qs.T
    n_sub = TK_A // SUB_A
    n_pieces = vt_ref.shape[0] * n_sub

    def scores(t):
        k = k_ref[t * SUB_A:(t + 1) * SUB_A, :]
        return [_dot(k, qt_sc[c]) for c in range(n_pairs)]

    pv = [None] * n_pairs
    s_cur = scores(0)
    for t in range(n_pieces):
        s_nxt = scores(t + 1) if t + 1 < n_pieces else None
        vt = vt_ref[t // n_sub, :, (t % n_sub) * SUB_A:(t % n_sub + 1) * SUB_A]
        for c in range(n_pairs):
            part = _dot(vt, jnp.exp2(s_cur[c]).astype(BF16))
            pv[c] = part if pv[c] is None else pv[c] + part
        s_cur = s_nxt
    for c in range(n_pairs):
        acc_sc[c] = pv[c]
    for c in range(n_pairs):
        o = acc_sc[c, 0:HEAD_DIM, :] / acc_sc[c, HEAD_DIM:HEAD_DIM + 1, :]
        pair = jnp.concatenate([o[:, 0:tq], o[:, tq:]], axis=0)
        o_ref[:, c * LANES:(c + 1) * LANES] = pair.T.astype(o_ref.dtype)


def _attn_a_bounded_call(qa, ka, vta):
    B, S, _ = qa.shape
    qcols = A_Q // N_KV_A
    pairs = N_HEADS_A // N_KV_A // 2
    qspec = pl.BlockSpec((None, TQ_A, qcols), lambda b, g, i: (b, i, g))
    kspec = pl.BlockSpec((None, None, S, LANES), lambda b, g, i: (b, g, 0, 0))
    vspec = pl.BlockSpec((None, None, S // TK_A, VT_ROWS, TK_A), lambda b, g, i: (b, g, 0, 0, 0))
    return pl.pallas_call(
        _attn_a_bounded_kernel,
        grid=(B, N_KV_A, S // TQ_A),
        in_specs=[qspec, kspec, vspec],
        out_specs=qspec,
        out_shape=jax.ShapeDtypeStruct((B, S, A_Q), BF16),
        scratch_shapes=[pltpu.VMEM((pairs, LANES, 2 * TQ_A), BF16),
                        pltpu.VMEM((pairs, VT_ROWS, 2 * TQ_A), F32)],
        compiler_params=pltpu.CompilerParams(
            dimension_semantics=("arbitrary", "arbitrary", "arbitrary"),
            vmem_limit_bytes=VMEM_LIMIT),
        name="attn_a_bounded",
    )(qa, ka, vta)


def _attn_a_call(qa, ka, vta):
    B, S, _ = qa.shape
    qcols = A_Q // N_KV_A
    pairs = N_HEADS_A // N_KV_A // 2
    qspec = pl.BlockSpec((None, TQ_A, qcols), lambda b, g, i: (b, i, g))
    kspec = pl.BlockSpec((None, None, S, LANES), lambda b, g, i: (b, g, 0, 0))
    vspec = pl.BlockSpec((None, None, S // TK_A, VT_ROWS, TK_A), lambda b, g, i: (b, g, 0, 0, 0))
    return pl.pallas_call(
        _attn_a_kernel,
        grid=(B, N_KV_A, S // TQ_A),
        in_specs=[qspec, kspec, vspec],
        out_specs=qspec,
        out_shape=jax.ShapeDtypeStruct((B, S, A_Q), BF16),
        scratch_shapes=[pltpu.VMEM((pairs, 2 * TQ_A, LANES), BF16),
                        pltpu.VMEM((pairs, 1, 2 * TQ_A), F32),
                        pltpu.VMEM((pairs, VT_ROWS, 2 * TQ_A), F32),
                        pltpu.VMEM((pairs, TK_A, 2 * TQ_A), F32),
                        pltpu.VMEM((pairs, TK_A, 2 * TQ_A), F32),
                        pltpu.VMEM((pairs, 1, 2 * TQ_A), F32),
                        pltpu.VMEM((pairs, 1, 2 * TQ_A), F32)],
        compiler_params=pltpu.CompilerParams(
            dimension_semantics=("arbitrary", "arbitrary", "arbitrary"),
            vmem_limit_bytes=VMEM_LIMIT),
        name="attn_a",
    )(qa, ka, vta)


def _attn_b_kernel(q_ref, kp_ref, kc_ref, kn_ref, vp_ref, vc_ref, vn_ref,
                   o_ref, st_ref, kbuf, vbuf, *, seq_len):
    j = pl.program_id(2)
    kbuf[0:SPAN_B, :] = kp_ref[...]
    kbuf[SPAN_B:SPAN_B + TQ_B, :] = kc_ref[...]
    kbuf[SPAN_B + TQ_B:, :] = kn_ref[...]
    vbuf[0:SPAN_B, :] = vp_ref[...]
    vbuf[SPAN_B:SPAN_B + TQ_B, :] = vc_ref[...]
    vbuf[SPAN_B + TQ_B:, :] = vn_ref[...]

    n_keys = SUB_B + 2 * SPAN_B
    c_idx = lax.broadcasted_iota(jnp.int32, (n_keys, 2 * SUB_B), 0)
    a_idx = lax.broadcasted_iota(jnp.int32, (n_keys, 2 * SUB_B), 1) % SUB_B
    band = jnp.abs(c_idx - SPAN_B - a_idx) <= SPAN_B
    low = lax.broadcasted_iota(jnp.int32, (SUB_B, LANES), 1) < HEAD_DIM
    ones = jnp.ones((VT_ROWS - HEAD_DIM, n_keys), BF16)
    pad = jnp.zeros((LANES - 2 * N_HEADS_B, SUB_B), F32)

    def scores_of(i):
        q0 = i * SUB_B
        kpos = j * TQ_B + q0 - SPAN_B + c_idx
        valid = band & (kpos >= 0) & (kpos < seq_len)
        bias = jnp.where(valid, 0.0, NEG_INF).astype(F32)
        out = []
        for hp in range(N_HEADS_B // 2):
            cols = slice(hp * LANES, (hp + 1) * LANES)
            qp = q_ref[q0:q0 + SUB_B, cols]
            zero = jnp.zeros_like(qp)
            qs = jnp.concatenate([jnp.where(low, qp, zero), jnp.where(low, zero, qp)], axis=0)
            out.append(_dot_nt(kbuf[q0:q0 + n_keys, cols], qs) + bias)
        return out

    def finish(i, scores):
        q0 = i * SUB_B
        ms, ls = [], []
        for hp in range(N_HEADS_B // 2):
            cols = slice(hp * LANES, (hp + 1) * LANES)
            vw = vbuf[q0:q0 + n_keys, cols]
            s = scores[hp]
            m = jnp.max(s, axis=0, keepdims=True)
            p = jnp.exp2(s - m).astype(BF16)
            vt = jnp.concatenate([vw.T, ones], axis=0)
            o_all = _dot(vt, p)
            o_t = jnp.concatenate([o_all[0:HEAD_DIM, 0:SUB_B],
                                   o_all[HEAD_DIM:LANES, SUB_B:]], axis=0)
            o_ref[q0:q0 + SUB_B, cols] = o_t.T.astype(o_ref.dtype)
            l = o_all[LANES:LANES + 1, :]
            ms += [m[:, 0:SUB_B], m[:, SUB_B:]]
            ls += [l[:, 0:SUB_B], l[:, SUB_B:]]
        st_t = jnp.concatenate(ms + ls + [pad], axis=0)
        st_ref[q0:q0 + SUB_B, :] = st_t.T

    n_sub = TQ_B // SUB_B
    scores = scores_of(0)
    for i in range(n_sub):
        nxt = scores_of(i + 1) if i + 1 < n_sub else None
        finish(i, scores)
        scores = nxt


def _attn_b_call(qb, kb, vb):
    B, d, L, C = qb.shape
    nh = L // SPAN_B
    per = TQ_B // SPAN_B
    cur = pl.BlockSpec((None, None, TQ_B, C), lambda b, r, j: (b, r, j, 0))
    prev = pl.BlockSpec((None, None, SPAN_B, C),
                        lambda b, r, j: (b, r, jnp.maximum(j * per - 1, 0), 0))
    nxt = pl.BlockSpec((None, None, SPAN_B, C),
                       lambda b, r, j: (b, r, jnp.minimum((j + 1) * per, nh - 1), 0))
    st_spec = pl.BlockSpec((None, None, TQ_B, LANES), lambda b, r, j: (b, r, j, 0))
    return pl.pallas_call(
        functools.partial(_attn_b_kernel, seq_len=L),
        grid=(B, d, L // TQ_B),
        in_specs=[cur, prev, cur, nxt, prev, cur, nxt],
        out_specs=[cur, st_spec],
        out_shape=[jax.ShapeDtypeStruct((B, d, L, C), BF16),
                   jax.ShapeDtypeStruct((B, d, L, LANES), F32)],
        scratch_shapes=[pltpu.VMEM((TQ_B + 2 * SPAN_B, C), BF16),
                        pltpu.VMEM((TQ_B + 2 * SPAN_B, C), BF16)],
        compiler_params=pltpu.CompilerParams(
            dimension_semantics=("arbitrary", "arbitrary", "arbitrary"),
            vmem_limit_bytes=VMEM_LIMIT),
        name=f"attn_b_d{d}",
    )(qb, kb, kb, kb, vb, vb, vb)


def _natural_order(ref, scratch):
    d, _, cols = ref.shape
    if d == 1:
        return ref[0].astype(F32)
    for r in range(d):
        for c in range(cols // LANES):
            scratch[c, pl.ds(r, TM // d, stride=d), :] = ref[r, :, c * LANES:(c + 1) * LANES].astype(F32)
    return jnp.concatenate([scratch[c] for c in range(cols // LANES)], axis=1)


def _out_ffn_kernel(x_ref, ha_ref, o1_ref, o2_ref, o3_ref, s1_ref, s2_ref, s3_ref, ex_ref,
                    wo_ref, mixg_ref, pre_ref, post_ref, wg_ref, wu_ref, wd_ref, y_ref,
                    of2_sc, of3_sc, sf2_sc, sf3_sc):
    stats = (_natural_order(s1_ref, None), _natural_order(s2_ref, sf2_sc),
             _natural_order(s3_ref, sf3_sc))
    parts = (_natural_order(o1_ref, None), _natural_order(o2_ref, of2_sc),
             _natural_order(o3_ref, of3_sc))
    is_max = lax.broadcasted_iota(jnp.int32, (TM, LANES), 1) < N_HEADS_B
    m_all = jnp.maximum(jnp.maximum(stats[0], stats[1]), stats[2])
    es = [jnp.exp2(s - m_all) for s in stats]
    den = None
    for e, s in zip(es, stats):
        term = e * pltpu.roll(s, LANES - N_HEADS_B, 1)
        den = term if den is None else den + term
    heads_b = None
    for e, o in zip(es, parts):
        w = jnp.where(is_max, e / den, 0.0)
        hi = w.astype(BF16)
        lo = (w - hi.astype(F32)).astype(BF16)
        term = (_dot(hi, ex_ref[...]) + _dot(lo, ex_ref[...])) * o
        heads_b = term if heads_b is None else heads_b + term

    mixed = _dot(ha_ref[...], wo_ref[0:A_Q, :]) + _dot(heads_b.astype(BF16), wo_ref[A_Q:, :])
    x2 = x_ref[...] + _rms(mixed, mixg_ref[...])
    y_ref[...] = _swiglu_half_step(x2, pre_ref[...], post_ref[...], wg_ref, wu_ref, wd_ref)


def _out_ffn_call(x1, heads_a, parts, expand, wo, mix_g, pre_g, post_g, wg, wu, wd):
    n = x1.shape[0]
    per_batch = parts[0][0].shape[2] // TM
    row = lambda c: pl.BlockSpec((TM, c), lambda i: (i, 0))

    def strided(a):
        _, d, _, c = a.shape
        return pl.BlockSpec((None, d, TM // d, c), lambda i: (i // per_batch, 0, i % per_batch, 0))

    (o1, s1), (o2, s2), (o3, s3) = parts
    return pl.pallas_call(
        _out_ffn_kernel,
        grid=(n // TM,),
        in_specs=[row(D_MODEL), row(A_Q), strided(o1), strided(o2), strided(o3),
                  strided(s1), strided(s2), strided(s3), _resident(expand.shape),
                  _resident(wo.shape), _resident((1, D_MODEL)), _resident((1, D_MODEL)),
                  _resident((1, D_MODEL)), _resident(wg.shape), _resident(wu.shape),
                  _resident(wd.shape)],
        out_specs=row(D_MODEL),
        out_shape=jax.ShapeDtypeStruct((n, D_MODEL), F32),
        scratch_shapes=[pltpu.VMEM((B_QKV // LANES, TM, LANES), F32),
                        pltpu.VMEM((B_QKV // LANES, TM, LANES), F32),
                        pltpu.VMEM((1, TM, LANES), F32), pltpu.VMEM((1, TM, LANES), F32)],
        compiler_params=pltpu.CompilerParams(
            dimension_semantics=("arbitrary",), vmem_limit_bytes=VMEM_LIMIT),
        name="out_ffn2",
    )(x1, heads_a, o1, o2, o3, s1, s2, s3, expand, wo, mix_g, pre_g, post_g, wg, wu, wd)


def _rope_tables(seq):
    pos = jnp.arange(seq, dtype=jnp.int32)
    row = (pos // GRID_W).astype(F32)[:, None]
    col = (pos % GRID_W).astype(F32)[:, None]
    dim_a = HEAD_DIM // 2
    fa = ROPE_THETA ** (-jnp.arange(0, dim_a, 2, dtype=F32) / dim_a)
    fb = ROPE_THETA ** (-jnp.arange(0, HEAD_DIM, 2, dtype=F32) / HEAD_DIM)
    ar, ac = row * fa[None, :], col * fa[None, :]
    ab = pos.astype(F32)[:, None] * fb[None, :]
    cos_a = jnp.concatenate([jnp.cos(ar), jnp.cos(ar), jnp.cos(ac), jnp.cos(ac)], axis=-1)
    sin_a = jnp.concatenate([-jnp.sin(ar), jnp.sin(ar), -jnp.sin(ac), jnp.sin(ac)], axis=-1)
    cos_b = jnp.concatenate([jnp.cos(ab), jnp.cos(ab)], axis=-1)
    sin_b = jnp.concatenate([-jnp.sin(ab), jnp.sin(ab)], axis=-1)
    two = lambda t: jnp.tile(t, (1, LANES // HEAD_DIM))
    return two(cos_a), two(sin_a), two(cos_b), two(sin_b)


def _layer(x, ffn1_pre_g, ffn1_post_g, ffn1_w_gate, ffn1_w_up, ffn1_w_down,
           mix_pre_g, mix_post_g, w_qkv, q_norm_g, k_norm_g, w_out,
           ffn2_pre_g, ffn2_post_g, ffn2_w_gate, ffn2_w_up, ffn2_w_down, tables, seg, expand):
    B, S, D = x.shape
    vec = lambda g: g.reshape(1, -1).astype(F32)
    bf = lambda w: w.astype(BF16)

    x1 = _ffn_call(x.reshape(B * S, D), vec(ffn1_pre_g), vec(ffn1_post_g),
                   bf(ffn1_w_gate), bf(ffn1_w_up), bf(ffn1_w_down))
    heads_per_tile = 2 * LANES // HEAD_DIM
    kv_gain = jnp.concatenate([jnp.tile(vec(k_norm_g), (1, N_KV_A)), jnp.ones((1, A_KV), F32)], axis=1)
    qa, ka, vta, qkv_b = _qkv_call(x1.reshape(B, S, D), vec(mix_pre_g), bf(w_qkv), seg,
                                   jnp.tile(vec(q_norm_g), (1, heads_per_tile)), kv_gain, *tables)
    score_bound = ((HEAD_DIM * SCALE * LOG2E) * jnp.max(jnp.abs(q_norm_g))
                   * jnp.max(jnp.abs(k_norm_g)))
    heads_a = lax.cond(score_bound <= MAX_SCORE_BOUND,
                       lambda: _attn_a_bounded_call(qa, ka, vta),
                       lambda: _attn_a_call(qa, ka, vta))
    parts = [_attn_b_call(*qkv) for qkv in qkv_b]
    y = _out_ffn_call(x1, heads_a.reshape(B * S, A_Q), parts, expand,
                      bf(w_out), vec(mix_post_g), vec(ffn2_pre_g), vec(ffn2_post_g),
                      bf(ffn2_w_gate), bf(ffn2_w_up), bf(ffn2_w_down))
    return y.reshape(B, S, D)


def kernel(x, ffn1_pre_g, ffn1_post_g, ffn1_w_gate, ffn1_w_up, ffn1_w_down, mix_pre_g, mix_post_g, w_qkv, q_norm_g, k_norm_g, w_out, ffn2_pre_g, ffn2_post_g, ffn2_w_gate, ffn2_w_up, ffn2_w_down):
    assert all(w // 2 // d == SPAN_B for w, d in DILATED_CONFIGS) and TK_A % TM == 0
    S = x.shape[1]
    tables = _rope_tables(S)
    head_of_lane = jnp.arange(2 * LANES) // HEAD_DIM
    seg = (head_of_lane[:, None] == head_of_lane[None, :]).astype(BF16) / HEAD_DIM
    expand = (jnp.arange(LANES)[:, None] == jnp.arange(B_QKV)[None, :] // HEAD_DIM).astype(BF16)
    params = (ffn1_pre_g, ffn1_post_g, ffn1_w_gate, ffn1_w_up, ffn1_w_down, mix_pre_g, mix_post_g,
              w_qkv, q_norm_g, k_norm_g, w_out, ffn2_pre_g, ffn2_post_g, ffn2_w_gate, ffn2_w_up,
              ffn2_w_down)
    for l in range(ffn1_pre_g.shape[0]):
        x = _layer(x, *(p[l] for p in params), tables, seg, expand)
    return x
```

```python
import functools

import jax
import jax.numpy as jnp
from jax import lax
from jax.experimental import pallas as pl
from jax.experimental.pallas import tpu as pltpu

D_MODEL = 1024
HEAD_DIM = 64
N_HEADS_A = 8
N_KV_A = 2
N_HEADS_B = 8
DILATED_CONFIGS = ((128, 1), (512, 4), (2048, 16))
DILATIONS = tuple(d for _, d in DILATED_CONFIGS)
GRID_W = 64
ROPE_THETA = 10000.0
D_FF = 2816
EPS = 1e-6
NEG_INF = -1e30

A_Q = N_HEADS_A * HEAD_DIM
A_KV = N_KV_A * HEAD_DIM
B_QKV = N_HEADS_B * HEAD_DIM
QKV_COLS = A_Q + 2 * A_KV + 3 * B_QKV
SCALE = HEAD_DIM ** -0.5
LOG2E = 1.4426950408889634

LANES = 128
FF_CHUNKS = ((0, 1024), (1024, 2048), (2048, 2816))
TM = 512
TQ_A = 256
TK_A = 1024
TQ_BOUNDED = 512
QH_A = 256
SUB_A = 256
MAX_SCORE_BOUND = 48.0
VT_ROWS = HEAD_DIM + 16
TQ_B = 512
SUB_B = 128
SPAN_B = 64
VMEM_LIMIT = 52 * 1024 * 1024

BF16 = jnp.bfloat16
F32 = jnp.float32


def _dot(a, b):
    return jnp.dot(a, b, preferred_element_type=F32)


def _dot_nt(a, b):
    return lax.dot_general(a, b, (((1,), (1,)), ((), ())), preferred_element_type=F32)


def _rms(x, g):
    ms = jnp.mean(x * x, axis=-1, keepdims=True)
    return x * lax.rsqrt(ms + EPS) * g


def _swiglu_half_step(x, pre_g, post_g, wg_ref, wu_ref, wd_ref):
    h = _rms(x, pre_g).astype(BF16)
    f = None
    for lo, hi in FF_CHUNKS:
        g = _dot(h, wg_ref[:, lo:hi])
        u = _dot(h, wu_ref[:, lo:hi])
        a = (g / (1.0 + jnp.exp(-g)) * u).astype(BF16)
        part = _dot(a, wd_ref[lo:hi, :])
        f = part if f is None else f + part
    return x + 0.5 * _rms(f, post_g)


def _ffn_kernel(x_ref, pre_ref, post_ref, wg_ref, wu_ref, wd_ref, o_ref):
    o_ref[...] = _swiglu_half_step(x_ref[...], pre_ref[...], post_ref[...],
                                   wg_ref, wu_ref, wd_ref)


def _resident(shape):
    nd = len(shape)
    return pl.BlockSpec(shape, lambda *_: (0,) * nd, pipeline_mode=pl.Buffered(1))


def _ffn_call(x2d, pre_g, post_g, wg, wu, wd):
    n = x2d.shape[0]
    row = pl.BlockSpec((TM, D_MODEL), lambda i: (i, 0))
    return pl.pallas_call(
        _ffn_kernel,
        grid=(n // TM,),
        in_specs=[row, _resident((1, D_MODEL)), _resident((1, D_MODEL)),
                  _resident(wg.shape), _resident(wu.shape), _resident(wd.shape)],
        out_specs=row,
        out_shape=jax.ShapeDtypeStruct(x2d.shape, F32),
        compiler_params=pltpu.CompilerParams(
            dimension_semantics=("arbitrary",), vmem_limit_bytes=VMEM_LIMIT),
        name="ffn1",
    )(x2d, pre_g, post_g, wg, wu, wd)


def _rope(c, cos, sin_signed, half):
    lane = lax.broadcasted_iota(jnp.int32, c.shape, 1)
    first = (lane % (2 * half)) < half
    partner = jnp.where(first, pltpu.roll(c, LANES - half, 1), pltpu.roll(c, half, 1))
    return c * cos + partner * sin_signed


def _head_rms(c, seg_ref, g):
    sq = c * c
    hi = sq.astype(BF16)
    lo = (sq - hi.astype(F32)).astype(BF16)
    ms = _dot(hi, seg_ref[...]) + _dot(lo, seg_ref[...])
    return c * lax.rsqrt(ms + EPS) * g


def _dup_halves(c):
    lane = lax.broadcasted_iota(jnp.int32, c.shape, 1)
    low = lane < HEAD_DIM
    r = pltpu.roll(c, HEAD_DIM, 1)
    return jnp.where(low, c, r), jnp.where(low, r, c)


def _qkv_kernel(x_ref, pre_ref, w_ref, seg_ref, qg_ref, kg_ref,
                cosa_ref, sina_ref, cosb_ref, sinb_ref,
                qa_ref, ka_ref, va_ref, *rest):
    b_refs, (qf_sc, kf_sc, vf_sc) = rest[:-3], rest[-3:]
    h = _rms(x_ref[...], pre_ref[...]).astype(BF16)
    cosa, sina = cosa_ref[...], sina_ref[...]
    cosb, sinb = cosb_ref[...], sinb_ref[...]
    wide = 2 * LANES

    qa = _dot(h, w_ref[:, 0:A_Q])
    for t in range(A_Q // wide):
        qn = _head_rms(qa[:, t * wide:(t + 1) * wide], seg_ref, qg_ref[...])
        for c in range(wide // LANES):
            q = _rope(qn[:, c * LANES:(c + 1) * LANES], cosa, sina, HEAD_DIM // 4) * (SCALE * LOG2E)
            col = t * wide + c * LANES
            qa_ref[:, col:col + LANES] = q.astype(BF16)

    kv = _dot(h, w_ref[:, A_Q:A_Q + 2 * A_KV])
    k = _rope(_head_rms(kv, seg_ref, kg_ref[...])[:, 0:A_KV], cosa, sina, HEAD_DIM // 4)
    k0, k1 = _dup_halves(k)
    ka_ref[0] = k0.astype(BF16)
    ka_ref[1] = k1.astype(BF16)
    vt = kv[:, A_KV:].T.astype(BF16)
    ones = jnp.ones((VT_ROWS - HEAD_DIM, TM), BF16)
    for g in range(N_KV_A):
        va_ref[g, 0:HEAD_DIM, :] = vt[g * HEAD_DIM:(g + 1) * HEAD_DIM, :]
        va_ref[g, HEAD_DIM:, :] = ones

    base = A_Q + 2 * A_KV
    qb = _dot(h, w_ref[:, base:base + B_QKV])
    kb = _dot(h, w_ref[:, base + B_QKV:base + 2 * B_QKV])
    vb = _dot(h, w_ref[:, base + 2 * B_QKV:base + 3 * B_QKV])
    for c in range(B_QKV // LANES):
        sl = slice(c * LANES, (c + 1) * LANES)
        qf_sc[c] = _rope(qb[:, sl], cosb, sinb, HEAD_DIM // 2) * (SCALE * LOG2E)
        kf_sc[c] = _rope(kb[:, sl], cosb, sinb, HEAD_DIM // 2)
        vf_sc[c] = vb[:, sl]
    for j, src in enumerate((qf_sc, kf_sc, vf_sc)):
        for i, d in enumerate(DILATIONS):
            out = b_refs[3 * i + j]
            for r in range(d):
                rows = slice(None) if d == 1 else pl.ds(r, TM // d, stride=d)
                for c in range(B_QKV // LANES):
                    out[r, :, c * LANES:(c + 1) * LANES] = src[c, rows, :].astype(BF16)


def _qkv_call(x1, pre_g, w, seg, qg, kg, cosa, sina, cosb, sinb):
    B, S, _ = x1.shape
    tab = pl.BlockSpec((TM, LANES), lambda i, b: (i, 0))
    wide = pl.BlockSpec((None, TM, A_Q), lambda i, b: (b, i, 0))
    dup = pl.BlockSpec((None, N_KV_A, TM, LANES), lambda i, b: (b, 0, i, 0))
    per = TK_A // TM
    vt = pl.BlockSpec((None, N_KV_A, None, VT_ROWS, TM), lambda i, b: (b, 0, i // per, 0, i % per))
    wide_shape = jax.ShapeDtypeStruct((B, S, A_Q), BF16)
    dup_shape = jax.ShapeDtypeStruct((B, N_KV_A, S, LANES), BF16)
    vt_shape = jax.ShapeDtypeStruct((B, N_KV_A, S // TK_A, VT_ROWS, TK_A), BF16)
    b_specs, b_shapes = [], []
    for d in DILATIONS:
        b_specs += [pl.BlockSpec((None, d, TM // d, B_QKV), lambda i, b: (b, 0, i, 0))] * 3
        b_shapes += [jax.ShapeDtypeStruct((B, d, S // d, B_QKV), BF16)] * 3
    outs = pl.pallas_call(
        _qkv_kernel,
        grid=(S // TM, B),
        in_specs=[pl.BlockSpec((None, TM, D_MODEL), lambda i, b: (b, i, 0)),
                  _resident((1, D_MODEL)), _resident(w.shape), _resident(seg.shape),
                  _resident(qg.shape), _resident(kg.shape), tab, tab, tab, tab],
        out_specs=[wide, dup, vt] + b_specs,
        out_shape=[wide_shape, dup_shape, vt_shape] + b_shapes,
        scratch_shapes=[pltpu.VMEM((B_QKV // LANES, TM, LANES), F32)] * 3,
        compiler_params=pltpu.CompilerParams(
            dimension_semantics=("arbitrary", "arbitrary"), vmem_limit_bytes=VMEM_LIMIT),
        name="qkv",
    )(x1, pre_g, w, seg, qg, kg, cosa, sina, cosb, sinb)
    qa, ka, vta = outs[:3]
    qkv_b = [tuple(outs[3 + 3 * i:6 + 3 * i]) for i in range(len(DILATIONS))]
    return qa, ka, vta, qkv_b


def _attn_a_kernel(q_ref, k_ref, vt_ref, o_ref, qs_sc, m_sc, acc_sc, sa_sc, sb_sc, cma_sc, cmb_sc):
    tq = q_ref.shape[0]
    n_pairs = qs_sc.shape[0]
    low = lax.broadcasted_iota(jnp.int32, (tq, LANES), 1) < HEAD_DIM
    for c in range(n_pairs):
        qc = q_ref[:, c * LANES:(c + 1) * LANES]
        zero = jnp.zeros_like(qc)
        qs_sc[c, 0:tq, :] = jnp.where(low, qc, zero)
        qs_sc[c, tq:, :] = jnp.where(low, zero, qc)

    m_sc[...] = jnp.full(m_sc.shape, NEG_INF, F32)
    acc_sc[...] = jnp.zeros(acc_sc.shape, F32)

    n_k = vt_ref.shape[0]

    def scores(kb, s_out, cm_out):
        k = k_ref[pl.ds(pl.multiple_of(kb * TK_A, TK_A), TK_A), :]
        for c in range(n_pairs):
            s = _dot_nt(k, qs_sc[c])
            s_out[c] = s
            cm_out[c] = jnp.max(s, axis=0, keepdims=True)

    def consume(kb, s_in, cm_in):
        vt = vt_ref[kb]
        for c in range(n_pairs):
            m_old = m_sc[c]
            m_new = jnp.maximum(m_old, cm_in[c])
            alpha = jnp.exp2(m_old - m_new)
            p = jnp.exp2(s_in[c] - m_new).astype(BF16)
            acc_sc[c] = alpha * acc_sc[c] + _dot(vt, p)
            m_sc[c] = m_new

    bufs = ((sa_sc, cma_sc), (sb_sc, cmb_sc))
    scores(0, *bufs[0])

    def fused(kb, cur, nxt):
        s_in, cm_in = cur
        s_out, cm_out = nxt
        k0 = pl.multiple_of((kb + 1) * TK_A, TK_A)
        m_new, alpha, pv, cm = [], [], [], []
        for c in range(n_pairs):
            m_old = m_sc[c]
            m_new.append(jnp.maximum(m_old, cm_in[c]))
            alpha.append(jnp.exp2(m_old - m_new[c]))
            pv.append(None)
            cm.append(None)
        for j in range(TK_A // SUB_A):
            rows = slice(j * SUB_A, (j + 1) * SUB_A)
            k = k_ref[pl.ds(k0 + j * SUB_A, SUB_A), :]
            vt = vt_ref[kb, :, rows]
            for c in range(n_pairs):
                s = _dot_nt(k, qs_sc[c])
                s_out[c, rows, :] = s
                part = jnp.max(s, axis=0, keepdims=True)
                cm[c] = part if cm[c] is None else jnp.maximum(cm[c], part)
            for c in range(n_pairs):
                p = jnp.exp2(s_in[c, rows, :] - m_new[c]).astype(BF16)
                part = _dot(vt, p)
                pv[c] = part if pv[c] is None else pv[c] + part
        for c in range(n_pairs):
            cm_out[c] = cm[c]
            acc_sc[c] = alpha[c] * acc_sc[c] + pv[c]
            m_sc[c] = m_new[c]

    def body(kb, carry):
        for parity in range(2):
            @pl.when(kb % 2 == parity)
            def _():
                fused(kb, bufs[parity], bufs[1 - parity])
        return carry

    lax.fori_loop(0, n_k - 1, body, 0)
    consume(n_k - 1, *bufs[(n_k - 1) % 2])
    for c in range(n_pairs):
        o = acc_sc[c, 0:HEAD_DIM, :] / acc_sc[c, HEAD_DIM:HEAD_DIM + 1, :]
        pair = jnp.concatenate([o[:, 0:tq], o[:, tq:]], axis=0)
        o_ref[:, c * LANES:(c + 1) * LANES] = pair.T.astype(o_ref.dtype)


def _attn_a_bounded_kernel(q_ref, k_ref, vt_ref, o_ref, qt_sc, acc_sc):
    n_pairs = q_ref.shape[1] // LANES
    units = [(c, h) for h in range(q_ref.shape[0] // QH_A) for c in range(n_pairs)]
    low = lax.broadcasted_iota(jnp.int32, (QH_A, LANES), 1) < HEAD_DIM
    for u, (c, h) in enumerate(units):
        qc = q_ref[h * QH_A:(h + 1) * QH_A, c * LANES:(c + 1) * LANES]
        zero = jnp.zeros_like(qc)
        qs = jnp.concatenate([jnp.where(low, qc, zero), jnp.where(low, zero, qc)], axis=0)
        qt_sc[u] = qs.T
    n_sub = TK_A // SUB_A
    stream = [(t, u) for t in range(vt_ref.shape[0] * n_sub) for u in range(len(units))]

    def scores(i):
        t, u = stream[i]
        return _dot(k_ref[t * SUB_A:(t + 1) * SUB_A, :], qt_sc[u])

    depth = n_pairs
    ahead = [scores(i) for i in range(depth)]
    pv = [None] * len(units)
    for i, (t, u) in enumerate(stream):
        if i + depth < len(stream):
            ahead.append(scores(i + depth))
        vt = vt_ref[t // n_sub, :, (t % n_sub) * SUB_A:(t % n_sub + 1) * SUB_A]
        part = _dot(vt, jnp.exp2(ahead.pop(0)).astype(BF16))
        pv[u] = part if pv[u] is None else pv[u] + part
    for u in range(len(units)):
        acc_sc[u] = pv[u]
    for u, (c, h) in enumerate(units):
        o = acc_sc[u, 0:HEAD_DIM, :] / acc_sc[u, HEAD_DIM:HEAD_DIM + 1, :]
        pair = jnp.concatenate([o[:, 0:QH_A], o[:, QH_A:]], axis=0)
        o_ref[h * QH_A:(h + 1) * QH_A, c * LANES:(c + 1) * LANES] = pair.T.astype(o_ref.dtype)


def _attn_a_bounded_call(qa, ka, vta):
    B, S, _ = qa.shape
    qcols = A_Q // N_KV_A
    n_units = (qcols // LANES) * (TQ_BOUNDED // QH_A)
    qspec = pl.BlockSpec((None, TQ_BOUNDED, qcols), lambda b, g, i: (b, i, g))
    kspec = pl.BlockSpec((None, None, S, LANES), lambda b, g, i: (b, g, 0, 0))
    vspec = pl.BlockSpec((None, None, S // TK_A, VT_ROWS, TK_A), lambda b, g, i: (b, g, 0, 0, 0))
    return pl.pallas_call(
        _attn_a_bounded_kernel,
        grid=(B, N_KV_A, S // TQ_BOUNDED),
        in_specs=[qspec, kspec, vspec],
        out_specs=qspec,
        out_shape=jax.ShapeDtypeStruct((B, S, A_Q), BF16),
        scratch_shapes=[pltpu.VMEM((n_units, LANES, 2 * QH_A), BF16),
                        pltpu.VMEM((n_units, VT_ROWS, 2 * QH_A), F32)],
        compiler_params=pltpu.CompilerParams(
            dimension_semantics=("arbitrary", "arbitrary", "arbitrary"),
            vmem_limit_bytes=VMEM_LIMIT),
        name="attn_a_bounded",
    )(qa, ka, vta)


def _attn_a_call(qa, ka, vta):
    B, S, _ = qa.shape
    qcols = A_Q // N_KV_A
    pairs = N_HEADS_A // N_KV_A // 2
    qspec = pl.BlockSpec((None, TQ_A, qcols), lambda b, g, i: (b, i, g))
    kspec = pl.BlockSpec((None, None, S, LANES), lambda b, g, i: (b, g, 0, 0))
    vspec = pl.BlockSpec((None, None, S // TK_A, VT_ROWS, TK_A), lambda b, g, i: (b, g, 0, 0, 0))
    return pl.pallas_call(
        _attn_a_kernel,
        grid=(B, N_KV_A, S // TQ_A),
        in_specs=[qspec, kspec, vspec],
        out_specs=qspec,
        out_shape=jax.ShapeDtypeStruct((B, S, A_Q), BF16),
        scratch_shapes=[pltpu.VMEM((pairs, 2 * TQ_A, LANES), BF16),
                        pltpu.VMEM((pairs, 1, 2 * TQ_A), F32),
                        pltpu.VMEM((pairs, VT_ROWS, 2 * TQ_A), F32),
                        pltpu.VMEM((pairs, TK_A, 2 * TQ_A), F32),
                        pltpu.VMEM((pairs, TK_A, 2 * TQ_A), F32),
                        pltpu.VMEM((pairs, 1, 2 * TQ_A), F32),
                        pltpu.VMEM((pairs, 1, 2 * TQ_A), F32)],
        compiler_params=pltpu.CompilerParams(
            dimension_semantics=("arbitrary", "arbitrary", "arbitrary"),
            vmem_limit_bytes=VMEM_LIMIT),
        name="attn_a",
    )(qa, ka, vta)


def _attn_b_kernel(q_ref, kp_ref, kc_ref, kn_ref, vp_ref, vc_ref, vn_ref,
                   o_ref, st_ref, kbuf, vbuf, *, seq_len):
    j = pl.program_id(2)
    kbuf[0:SPAN_B, :] = kp_ref[...]
    kbuf[SPAN_B:SPAN_B + TQ_B, :] = kc_ref[...]
    kbuf[SPAN_B + TQ_B:, :] = kn_ref[...]
    vbuf[0:SPAN_B, :] = vp_ref[...]
    vbuf[SPAN_B:SPAN_B + TQ_B, :] = vc_ref[...]
    vbuf[SPAN_B + TQ_B:, :] = vn_ref[...]

    n_keys = SUB_B + 2 * SPAN_B
    c_idx = lax.broadcasted_iota(jnp.int32, (n_keys, 2 * SUB_B), 0)
    a_idx = lax.broadcasted_iota(jnp.int32, (n_keys, 2 * SUB_B), 1) % SUB_B
    band = jnp.abs(c_idx - SPAN_B - a_idx) <= SPAN_B
    low = lax.broadcasted_iota(jnp.int32, (SUB_B, LANES), 1) < HEAD_DIM
    ones = jnp.ones((VT_ROWS - HEAD_DIM, n_keys), BF16)
    pad = jnp.zeros((LANES - 2 * N_HEADS_B, SUB_B), F32)

    def scores_of(i):
        q0 = i * SUB_B
        kpos = j * TQ_B + q0 - SPAN_B + c_idx
        valid = band & (kpos >= 0) & (kpos < seq_len)
        bias = jnp.where(valid, 0.0, NEG_INF).astype(F32)
        out = []
        for hp in range(N_HEADS_B // 2):
            cols = slice(hp * LANES, (hp + 1) * LANES)
            qp = q_ref[q0:q0 + SUB_B, cols]
            zero = jnp.zeros_like(qp)
            qs = jnp.concatenate([jnp.where(low, qp, zero), jnp.where(low, zero, qp)], axis=0)
            out.append(_dot_nt(kbuf[q0:q0 + n_keys, cols], qs) + bias)
        return out

    def finish(i, scores):
        q0 = i * SUB_B
        ms, ls = [], []
        for hp in range(N_HEADS_B // 2):
            cols = slice(hp * LANES, (hp + 1) * LANES)
            vw = vbuf[q0:q0 + n_keys, cols]
            s = scores[hp]
            m = jnp.max(s, axis=0, keepdims=True)
            p = jnp.exp2(s - m).astype(BF16)
            vt = jnp.concatenate([vw.T, ones], axis=0)
            o_all = _dot(vt, p)
            o_t = jnp.concatenate([o_all[0:HEAD_DIM, 0:SUB_B],
                                   o_all[HEAD_DIM:LANES, SUB_B:]], axis=0)
            o_ref[q0:q0 + SUB_B, cols] = o_t.T.astype(o_ref.dtype)
            l = o_all[LANES:LANES + 1, :]
            ms += [m[:, 0:SUB_B], m[:, SUB_B:]]
            ls += [l[:, 0:SUB_B], l[:, SUB_B:]]
        st_t = jnp.concatenate(ms + ls + [pad], axis=0)
        st_ref[q0:q0 + SUB_B, :] = st_t.T

    n_sub = TQ_B // SUB_B
    scores = scores_of(0)
    for i in range(n_sub):
        nxt = scores_of(i + 1) if i + 1 < n_sub else None
        finish(i, scores)
        scores = nxt


def _attn_b_call(qb, kb, vb):
    B, d, L, C = qb.shape
    nh = L // SPAN_B
    per = TQ_B // SPAN_B
    cur = pl.BlockSpec((None, None, TQ_B, C), lambda b, r, j: (b, r, j, 0))
    prev = pl.BlockSpec((None, None, SPAN_B, C),
                        lambda b, r, j: (b, r, jnp.maximum(j * per - 1, 0), 0))
    nxt = pl.BlockSpec((None, None, SPAN_B, C),
                       lambda b, r, j: (b, r, jnp.minimum((j + 1) * per, nh - 1), 0))
    st_spec = pl.BlockSpec((None, None, TQ_B, LANES), lambda b, r, j: (b, r, j, 0))
    return pl.pallas_call(
        functools.partial(_attn_b_kernel, seq_len=L),
        grid=(B, d, L // TQ_B),
        in_specs=[cur, prev, cur, nxt, prev, cur, nxt],
        out_specs=[cur, st_spec],
        out_shape=[jax.ShapeDtypeStruct((B, d, L, C), BF16),
                   jax.ShapeDtypeStruct((B, d, L, LANES), F32)],
        scratch_shapes=[pltpu.VMEM((TQ_B + 2 * SPAN_B, C), BF16),
                        pltpu.VMEM((TQ_B + 2 * SPAN_B, C), BF16)],
        compiler_params=pltpu.CompilerParams(
            dimension_semantics=("arbitrary", "arbitrary", "arbitrary"),
            vmem_limit_bytes=VMEM_LIMIT),
        name=f"attn_b_d{d}",
    )(qb, kb, kb, kb, vb, vb, vb)


def _natural_order(ref, scratch):
    d, _, cols = ref.shape
    if d == 1:
        return ref[0].astype(F32)
    for r in range(d):
        for c in range(cols // LANES):
            scratch[c, pl.ds(r, TM // d, stride=d), :] = ref[r, :, c * LANES:(c + 1) * LANES].astype(F32)
    return jnp.concatenate([scratch[c] for c in range(cols // LANES)], axis=1)


def _out_ffn_kernel(x_ref, ha_ref, o1_ref, o2_ref, o3_ref, s1_ref, s2_ref, s3_ref, ex_ref,
                    wo_ref, mixg_ref, pre_ref, post_ref, wg_ref, wu_ref, wd_ref, y_ref,
                    of2_sc, of3_sc, sf2_sc, sf3_sc):
    stats = (_natural_order(s1_ref, None), _natural_order(s2_ref, sf2_sc),
             _natural_order(s3_ref, sf3_sc))
    parts = (_natural_order(o1_ref, None), _natural_order(o2_ref, of2_sc),
             _natural_order(o3_ref, of3_sc))
    is_max = lax.broadcasted_iota(jnp.int32, (TM, LANES), 1) < N_HEADS_B
    m_all = jnp.maximum(jnp.maximum(stats[0], stats[1]), stats[2])
    es = [jnp.exp2(s - m_all) for s in stats]
    den = None
    for e, s in zip(es, stats):
        term = e * pltpu.roll(s, LANES - N_HEADS_B, 1)
        den = term if den is None else den + term
    heads_b = None
    for e, o in zip(es, parts):
        w = jnp.where(is_max, e / den, 0.0)
        hi = w.astype(BF16)
        lo = (w - hi.astype(F32)).astype(BF16)
        term = (_dot(hi, ex_ref[...]) + _dot(lo, ex_ref[...])) * o
        heads_b = term if heads_b is None else heads_b + term

    mixed = _dot(ha_ref[...], wo_ref[0:A_Q, :]) + _dot(heads_b.astype(BF16), wo_ref[A_Q:, :])
    x2 = x_ref[...] + _rms(mixed, mixg_ref[...])
    y_ref[...] = _swiglu_half_step(x2, pre_ref[...], post_ref[...], wg_ref, wu_ref, wd_ref)


def _out_ffn_call(x1, heads_a, parts, expand, wo, mix_g, pre_g, post_g, wg, wu, wd):
    n = x1.shape[0]
    per_batch = parts[0][0].shape[2] // TM
    row = lambda c: pl.BlockSpec((TM, c), lambda i: (i, 0))

    def strided(a):
        _, d, _, c = a.shape
        return pl.BlockSpec((None, d, TM // d, c), lambda i: (i // per_batch, 0, i % per_batch, 0))

    (o1, s1), (o2, s2), (o3, s3) = parts
    return pl.pallas_call(
        _out_ffn_kernel,
        grid=(n // TM,),
        in_specs=[row(D_MODEL), row(A_Q), strided(o1), strided(o2), strided(o3),
                  strided(s1), strided(s2), strided(s3), _resident(expand.shape),
                  _resident(wo.shape), _resident((1, D_MODEL)), _resident((1, D_MODEL)),
                  _resident((1, D_MODEL)), _resident(wg.shape), _resident(wu.shape),
                  _resident(wd.shape)],
        out_specs=row(D_MODEL),
        out_shape=jax.ShapeDtypeStruct((n, D_MODEL), F32),
        scratch_shapes=[pltpu.VMEM((B_QKV // LANES, TM, LANES), F32),
                        pltpu.VMEM((B_QKV // LANES, TM, LANES), F32),
                        pltpu.VMEM((1, TM, LANES), F32), pltpu.VMEM((1, TM, LANES), F32)],
        compiler_params=pltpu.CompilerParams(
            dimension_semantics=("arbitrary",), vmem_limit_bytes=VMEM_LIMIT),
        name="out_ffn2",
    )(x1, heads_a, o1, o2, o3, s1, s2, s3, expand, wo, mix_g, pre_g, post_g, wg, wu, wd)


def _rope_tables(seq):
    pos = jnp.arange(seq, dtype=jnp.int32)
    row = (pos // GRID_W).astype(F32)[:, None]
    col = (pos % GRID_W).astype(F32)[:, None]
    dim_a = HEAD_DIM // 2
    fa = ROPE_THETA ** (-jnp.arange(0, dim_a, 2, dtype=F32) / dim_a)
    fb = ROPE_THETA ** (-jnp.arange(0, HEAD_DIM, 2, dtype=F32) / HEAD_DIM)
    ar, ac = row * fa[None, :], col * fa[None, :]
    ab = pos.astype(F32)[:, None] * fb[None, :]
    cos_a = jnp.concatenate([jnp.cos(ar), jnp.cos(ar), jnp.cos(ac), jnp.cos(ac)], axis=-1)
    sin_a = jnp.concatenate([-jnp.sin(ar), jnp.sin(ar), -jnp.sin(ac), jnp.sin(ac)], axis=-1)
    cos_b = jnp.concatenate([jnp.cos(ab), jnp.cos(ab)], axis=-1)
    sin_b = jnp.concatenate([-jnp.sin(ab), jnp.sin(ab)], axis=-1)
    two = lambda t: jnp.tile(t, (1, LANES // HEAD_DIM))
    return two(cos_a), two(sin_a), two(cos_b), two(sin_b)


def _layer(x, ffn1_pre_g, ffn1_post_g, ffn1_w_gate, ffn1_w_up, ffn1_w_down,
           mix_pre_g, mix_post_g, w_qkv, q_norm_g, k_norm_g, w_out,
           ffn2_pre_g, ffn2_post_g, ffn2_w_gate, ffn2_w_up, ffn2_w_down, tables, seg, expand):
    B, S, D = x.shape
    vec = lambda g: g.reshape(1, -1).astype(F32)
    bf = lambda w: w.astype(BF16)

    x1 = _ffn_call(x.reshape(B * S, D), vec(ffn1_pre_g), vec(ffn1_post_g),
                   bf(ffn1_w_gate), bf(ffn1_w_up), bf(ffn1_w_down))
    heads_per_tile = 2 * LANES // HEAD_DIM
    kv_gain = jnp.concatenate([jnp.tile(vec(k_norm_g), (1, N_KV_A)), jnp.ones((1, A_KV), F32)], axis=1)
    qa, ka, vta, qkv_b = _qkv_call(x1.reshape(B, S, D), vec(mix_pre_g), bf(w_qkv), seg,
                                   jnp.tile(vec(q_norm_g), (1, heads_per_tile)), kv_gain, *tables)
    score_bound = ((HEAD_DIM * SCALE * LOG2E) * jnp.max(jnp.abs(q_norm_g))
                   * jnp.max(jnp.abs(k_norm_g)))
    heads_a = lax.cond(score_bound <= MAX_SCORE_BOUND,
                       lambda: _attn_a_bounded_call(qa, ka, vta),
                       lambda: _attn_a_call(qa, ka, vta))
    parts = [_attn_b_call(*qkv) for qkv in qkv_b]
    y = _out_ffn_call(x1, heads_a.reshape(B * S, A_Q), parts, expand,
                      bf(w_out), vec(mix_post_g), vec(ffn2_pre_g), vec(ffn2_post_g),
                      bf(ffn2_w_gate), bf(ffn2_w_up), bf(ffn2_w_down))
    return y.reshape(B, S, D)


def kernel(x, ffn1_pre_g, ffn1_post_g, ffn1_w_gate, ffn1_w_up, ffn1_w_down, mix_pre_g, mix_post_g, w_qkv, q_norm_g, k_norm_g, w_out, ffn2_pre_g, ffn2_post_g, ffn2_w_gate, ffn2_w_up, ffn2_w_down):
    assert all(w // 2 // d == SPAN_B for w, d in DILATED_CONFIGS) and TK_A % TM == 0
    S = x.shape[1]
    tables = _rope_tables(S)
    head_of_lane = jnp.arange(2 * LANES) // HEAD_DIM
    seg = (head_of_lane[:, None] == head_of_lane[None, :]).astype(BF16) / HEAD_DIM
    expand = (jnp.arange(LANES)[:, None] == jnp.arange(B_QKV)[None, :] // HEAD_DIM).astype(BF16)
    params = (ffn1_pre_g, ffn1_post_g, ffn1_w_gate, ffn1_w_up, ffn1_w_down, mix_pre_g, mix_post_g,
              w_qkv, q_norm_g, k_norm_g, w_out, ffn2_pre_g, ffn2_post_g, ffn2_w_gate, ffn2_w_up,
              ffn2_w_down)
    for l in range(ffn1_pre_g.shape[0]):
        x = _layer(x, *(p[l] for p in params), tables, seg, expand)
    return x
```

```python
import functools

import jax
import jax.numpy as jnp
from jax import lax
from jax.experimental import pallas as pl
from jax.experimental.pallas import tpu as pltpu

D_MODEL = 1024
HEAD_DIM = 64
N_HEADS_A = 8
N_KV_A = 2
N_HEADS_B = 8
DILATED_CONFIGS = ((128, 1), (512, 4), (2048, 16))
DILATIONS = tuple(d for _, d in DILATED_CONFIGS)
GRID_W = 64
ROPE_THETA = 10000.0
D_FF = 2816
EPS = 1e-6
NEG_INF = -1e30

A_Q = N_HEADS_A * HEAD_DIM
A_KV = N_KV_A * HEAD_DIM
B_QKV = N_HEADS_B * HEAD_DIM
QKV_COLS = A_Q + 2 * A_KV + 3 * B_QKV
SCALE = HEAD_DIM ** -0.5
LOG2E = 1.4426950408889634

LANES = 128
FF_CHUNKS = ((0, 1024), (1024, 2048), (2048, 2816))
TM = 512
TQ_A = 256
TK_A = 1024
TQ_BOUNDED = 512
QH_A = 256
SUB_A = 256
MAX_SCORE_BOUND = 48.0
VT_ROWS = HEAD_DIM + 16
TQ_B = 512
SUB_B = 128
SPAN_B = 64
VMEM_LIMIT = 52 * 1024 * 1024

BF16 = jnp.bfloat16
F32 = jnp.float32


def _dot(a, b):
    return jnp.dot(a, b, preferred_element_type=F32)


def _dot_nt(a, b):
    return lax.dot_general(a, b, (((1,), (1,)), ((), ())), preferred_element_type=F32)


def _rms(x, g):
    ms = jnp.mean(x * x, axis=-1, keepdims=True)
    return x * lax.rsqrt(ms + EPS) * g


def _swiglu_half_step(x, pre_g, post_g, wg_ref, wu_ref, wd_ref):
    h = _rms(x, pre_g).astype(BF16)
    f = None
    for lo, hi in FF_CHUNKS:
        g = _dot(h, wg_ref[:, lo:hi])
        u = _dot(h, wu_ref[:, lo:hi])
        a = (g / (1.0 + jnp.exp(-g)) * u).astype(BF16)
        part = _dot(a, wd_ref[lo:hi, :])
        f = part if f is None else f + part
    return x + 0.5 * _rms(f, post_g)


def _ffn_kernel(x_ref, pre_ref, post_ref, wg_ref, wu_ref, wd_ref, o_ref):
    o_ref[...] = _swiglu_half_step(x_ref[...], pre_ref[...], post_ref[...],
                                   wg_ref, wu_ref, wd_ref)


def _resident(shape):
    nd = len(shape)
    return pl.BlockSpec(shape, lambda *_: (0,) * nd, pipeline_mode=pl.Buffered(1))


def _ffn_call(x2d, pre_g, post_g, wg, wu, wd):
    n = x2d.shape[0]
    row = pl.BlockSpec((TM, D_MODEL), lambda i: (i, 0))
    return pl.pallas_call(
        _ffn_kernel,
        grid=(n // TM,),
        in_specs=[row, _resident((1, D_MODEL)), _resident((1, D_MODEL)),
                  _resident(wg.shape), _resident(wu.shape), _resident(wd.shape)],
        out_specs=row,
        out_shape=jax.ShapeDtypeStruct(x2d.shape, F32),
        compiler_params=pltpu.CompilerParams(
            dimension_semantics=("arbitrary",), vmem_limit_bytes=VMEM_LIMIT),
        name="ffn1",
    )(x2d, pre_g, post_g, wg, wu, wd)


def _rope(c, cos, sin_signed, half):
    lane = lax.broadcasted_iota(jnp.int32, c.shape, 1)
    first = (lane % (2 * half)) < half
    partner = jnp.where(first, pltpu.roll(c, LANES - half, 1), pltpu.roll(c, half, 1))
    return c * cos + partner * sin_signed


def _head_rms(c, seg_ref, g):
    sq = c * c
    hi = sq.astype(BF16)
    lo = (sq - hi.astype(F32)).astype(BF16)
    ms = _dot(hi, seg_ref[...]) + _dot(lo, seg_ref[...])
    return c * lax.rsqrt(ms + EPS) * g


def _dup_halves(c):
    lane = lax.broadcasted_iota(jnp.int32, c.shape, 1)
    low = lane < HEAD_DIM
    r = pltpu.roll(c, HEAD_DIM, 1)
    return jnp.where(low, c, r), jnp.where(low, r, c)


def _qkv_kernel(x_ref, pre_ref, w_ref, seg_ref, qg_ref, kg_ref,
                cosa_ref, sina_ref, cosb_ref, sinb_ref,
                qa_ref, ka_ref, va_ref, *rest):
    b_refs, (qf_sc, kf_sc, vf_sc) = rest[:-3], rest[-3:]
    h = _rms(x_ref[...], pre_ref[...]).astype(BF16)
    cosa, sina = cosa_ref[...], sina_ref[...]
    cosb, sinb = cosb_ref[...], sinb_ref[...]
    wide = 2 * LANES

    qa = _dot(h, w_ref[:, 0:A_Q])
    for t in range(A_Q // wide):
        qn = _head_rms(qa[:, t * wide:(t + 1) * wide], seg_ref, qg_ref[...])
        for c in range(wide // LANES):
            q = _rope(qn[:, c * LANES:(c + 1) * LANES], cosa, sina, HEAD_DIM // 4) * (SCALE * LOG2E)
            col = t * wide + c * LANES
            qa_ref[:, col:col + LANES] = q.astype(BF16)

    kv = _dot(h, w_ref[:, A_Q:A_Q + 2 * A_KV])
    k = _rope(_head_rms(kv, seg_ref, kg_ref[...])[:, 0:A_KV], cosa, sina, HEAD_DIM // 4)
    k0, k1 = _dup_halves(k)
    ka_ref[0] = k0.astype(BF16)
    ka_ref[1] = k1.astype(BF16)
    vt = kv[:, A_KV:].T.astype(BF16)
    ones = jnp.ones((VT_ROWS - HEAD_DIM, TM), BF16)
    for g in range(N_KV_A):
        va_ref[g, 0:HEAD_DIM, :] = vt[g * HEAD_DIM:(g + 1) * HEAD_DIM, :]
        va_ref[g, HEAD_DIM:, :] = ones

    base = A_Q + 2 * A_KV
    qb = _dot(h, w_ref[:, base:base + B_QKV])
    kb = _dot(h, w_ref[:, base + B_QKV:base + 2 * B_QKV])
    vb = _dot(h, w_ref[:, base + 2 * B_QKV:base + 3 * B_QKV])
    for c in range(B_QKV // LANES):
        sl = slice(c * LANES, (c + 1) * LANES)
        qf_sc[c] = _rope(qb[:, sl], cosb, sinb, HEAD_DIM // 2) * (SCALE * LOG2E)
        kf_sc[c] = _rope(kb[:, sl], cosb, sinb, HEAD_DIM // 2)
        vf_sc[c] = vb[:, sl]
    for j, src in enumerate((qf_sc, kf_sc, vf_sc)):
        for i, d in enumerate(DILATIONS):
            out = b_refs[3 * i + j]
            for r in range(d):
                rows = slice(None) if d == 1 else pl.ds(r, TM // d, stride=d)
                for c in range(B_QKV // LANES):
                    out[r, :, c * LANES:(c + 1) * LANES] = src[c, rows, :].astype(BF16)


def _qkv_call(x1, pre_g, w, seg, qg, kg, cosa, sina, cosb, sinb):
    B, S, _ = x1.shape
    tab = pl.BlockSpec((TM, LANES), lambda i, b: (i, 0))
    wide = pl.BlockSpec((None, TM, A_Q), lambda i, b: (b, i, 0))
    dup = pl.BlockSpec((None, N_KV_A, TM, LANES), lambda i, b: (b, 0, i, 0))
    per = TK_A // TM
    vt = pl.BlockSpec((None, N_KV_A, None, VT_ROWS, TM), lambda i, b: (b, 0, i // per, 0, i % per))
    wide_shape = jax.ShapeDtypeStruct((B, S, A_Q), BF16)
    dup_shape = jax.ShapeDtypeStruct((B, N_KV_A, S, LANES), BF16)
    vt_shape = jax.ShapeDtypeStruct((B, N_KV_A, S // TK_A, VT_ROWS, TK_A), BF16)
    b_specs, b_shapes = [], []
    for d in DILATIONS:
        b_specs += [pl.BlockSpec((None, d, TM // d, B_QKV), lambda i, b: (b, 0, i, 0))] * 3
        b_shapes += [jax.ShapeDtypeStruct((B, d, S // d, B_QKV), BF16)] * 3
    outs = pl.pallas_call(
        _qkv_kernel,
        grid=(S // TM, B),
        in_specs=[pl.BlockSpec((None, TM, D_MODEL), lambda i, b: (b, i, 0)),
                  _resident((1, D_MODEL)), _resident(w.shape), _resident(seg.shape),
                  _resident(qg.shape), _resident(kg.shape), tab, tab, tab, tab],
        out_specs=[wide, dup, vt] + b_specs,
        out_shape=[wide_shape, dup_shape, vt_shape] + b_shapes,
        scratch_shapes=[pltpu.VMEM((B_QKV // LANES, TM, LANES), F32)] * 3,
        compiler_params=pltpu.CompilerParams(
            dimension_semantics=("arbitrary", "arbitrary"), vmem_limit_bytes=VMEM_LIMIT),
        name="qkv",
    )(x1, pre_g, w, seg, qg, kg, cosa, sina, cosb, sinb)
    qa, ka, vta = outs[:3]
    qkv_b = [tuple(outs[3 + 3 * i:6 + 3 * i]) for i in range(len(DILATIONS))]
    return qa, ka, vta, qkv_b


def _attn_a_kernel(q_ref, k_ref, vt_ref, o_ref, qs_sc, m_sc, acc_sc, sa_sc, sb_sc, cma_sc, cmb_sc):
    tq = q_ref.shape[0]
    n_pairs = qs_sc.shape[0]
    low = lax.broadcasted_iota(jnp.int32, (tq, LANES), 1) < HEAD_DIM
    for c in range(n_pairs):
        qc = q_ref[:, c * LANES:(c + 1) * LANES]
        zero = jnp.zeros_like(qc)
        qs_sc[c, 0:tq, :] = jnp.where(low, qc, zero)
        qs_sc[c, tq:, :] = jnp.where(low, zero, qc)

    m_sc[...] = jnp.full(m_sc.shape, NEG_INF, F32)
    acc_sc[...] = jnp.zeros(acc_sc.shape, F32)

    n_k = vt_ref.shape[0]

    def scores(kb, s_out, cm_out):
        k = k_ref[pl.ds(pl.multiple_of(kb * TK_A, TK_A), TK_A), :]
        for c in range(n_pairs):
            s = _dot_nt(k, qs_sc[c])
            s_out[c] = s
            cm_out[c] = jnp.max(s, axis=0, keepdims=True)

    def consume(kb, s_in, cm_in):
        vt = vt_ref[kb]
        for c in range(n_pairs):
            m_old = m_sc[c]
            m_new = jnp.maximum(m_old, cm_in[c])
            alpha = jnp.exp2(m_old - m_new)
            p = jnp.exp2(s_in[c] - m_new).astype(BF16)
            acc_sc[c] = alpha * acc_sc[c] + _dot(vt, p)
            m_sc[c] = m_new

    bufs = ((sa_sc, cma_sc), (sb_sc, cmb_sc))
    scores(0, *bufs[0])

    def fused(kb, cur, nxt):
        s_in, cm_in = cur
        s_out, cm_out = nxt
        k0 = pl.multiple_of((kb + 1) * TK_A, TK_A)
        m_new, alpha, pv, cm = [], [], [], []
        for c in range(n_pairs):
            m_old = m_sc[c]
            m_new.append(jnp.maximum(m_old, cm_in[c]))
            alpha.append(jnp.exp2(m_old - m_new[c]))
            pv.append(None)
            cm.append(None)
        for j in range(TK_A // SUB_A):
            rows = slice(j * SUB_A, (j + 1) * SUB_A)
            k = k_ref[pl.ds(k0 + j * SUB_A, SUB_A), :]
            vt = vt_ref[kb, :, rows]
            for c in range(n_pairs):
                s = _dot_nt(k, qs_sc[c])
                s_out[c, rows, :] = s
                part = jnp.max(s, axis=0, keepdims=True)
                cm[c] = part if cm[c] is None else jnp.maximum(cm[c], part)
            for c in range(n_pairs):
                p = jnp.exp2(s_in[c, rows, :] - m_new[c]).astype(BF16)
                part = _dot(vt, p)
                pv[c] = part if pv[c] is None else pv[c] + part
        for c in range(n_pairs):
            cm_out[c] = cm[c]
            acc_sc[c] = alpha[c] * acc_sc[c] + pv[c]
            m_sc[c] = m_new[c]

    def body(kb, carry):
        for parity in range(2):
            @pl.when(kb % 2 == parity)
            def _():
                fused(kb, bufs[parity], bufs[1 - parity])
        return carry

    lax.fori_loop(0, n_k - 1, body, 0)
    consume(n_k - 1, *bufs[(n_k - 1) % 2])
    for c in range(n_pairs):
        o = acc_sc[c, 0:HEAD_DIM, :] / acc_sc[c, HEAD_DIM:HEAD_DIM + 1, :]
        pair = jnp.concatenate([o[:, 0:tq], o[:, tq:]], axis=0)
        o_ref[:, c * LANES:(c + 1) * LANES] = pair.T.astype(o_ref.dtype)


def _attn_a_bounded_kernel(q_ref, k_ref, vt_ref, o_ref, qt_sc, acc_sc):
    n_pairs = q_ref.shape[1] // LANES
    units = [(c, h) for h in range(q_ref.shape[0] // QH_A) for c in range(n_pairs)]
    low = lax.broadcasted_iota(jnp.int32, (QH_A, LANES), 1) < HEAD_DIM
    for u, (c, h) in enumerate(units):
        qc = q_ref[h * QH_A:(h + 1) * QH_A, c * LANES:(c + 1) * LANES]
        zero = jnp.zeros_like(qc)
        qs = jnp.concatenate([jnp.where(low, qc, zero), jnp.where(low, zero, qc)], axis=0)
        qt_sc[u] = qs.T
    n_sub = TK_A // SUB_A
    stream = [(t, u) for t in range(vt_ref.shape[0] * n_sub) for u in range(len(units))]

    def scores(i):
        t, u = stream[i]
        return _dot(k_ref[t * SUB_A:(t + 1) * SUB_A, :], qt_sc[u])

    depth = n_pairs
    ahead = [scores(i) for i in range(depth)]
    pv = [None] * len(units)
    for i, (t, u) in enumerate(stream):
        if i + depth < len(stream):
            ahead.append(scores(i + depth))
        vt = vt_ref[t // n_sub, :, (t % n_sub) * SUB_A:(t % n_sub + 1) * SUB_A]
        part = _dot(vt, jnp.exp2(ahead.pop(0)).astype(BF16))
        pv[u] = part if pv[u] is None else pv[u] + part
    for u in range(len(units)):
        acc_sc[u] = pv[u]
    for u, (c, h) in enumerate(units):
        o = acc_sc[u, 0:HEAD_DIM, :] / acc_sc[u, HEAD_DIM:HEAD_DIM + 1, :]
        pair = jnp.concatenate([o[:, 0:QH_A], o[:, QH_A:]], axis=0)
        o_ref[h * QH_A:(h + 1) * QH_A, c * LANES:(c + 1) * LANES] = pair.T.astype(o_ref.dtype)


def _attn_a_bounded_call(qa, ka, vta):
    B, S, _ = qa.shape
    qcols = A_Q // N_KV_A
    n_units = (qcols // LANES) * (TQ_BOUNDED // QH_A)
    qspec = pl.BlockSpec((None, TQ_BOUNDED, qcols), lambda b, g, i: (b, i, g))
    kspec = pl.BlockSpec((None, None, S, LANES), lambda b, g, i: (b, g, 0, 0))
    vspec = pl.BlockSpec((None, None, S // TK_A, VT_ROWS, TK_A), lambda b, g, i: (b, g, 0, 0, 0))
    return pl.pallas_call(
        _attn_a_bounded_kernel,
        grid=(B, N_KV_A, S // TQ_BOUNDED),
        in_specs=[qspec, kspec, vspec],
        out_specs=qspec,
        out_shape=jax.ShapeDtypeStruct((B, S, A_Q), BF16),
        scratch_shapes=[pltpu.VMEM((n_units, LANES, 2 * QH_A), BF16),
                        pltpu.VMEM((n_units, VT_ROWS, 2 * QH_A), F32)],
        compiler_params=pltpu.CompilerParams(
            dimension_semantics=("arbitrary", "arbitrary", "arbitrary"),
            vmem_limit_bytes=VMEM_LIMIT),
        name="attn_a_bounded",
    )(qa, ka, vta)


def _attn_a_call(qa, ka, vta):
    B, S, _ = qa.shape
    qcols = A_Q // N_KV_A
    pairs = N_HEADS_A // N_KV_A // 2
    qspec = pl.BlockSpec((None, TQ_A, qcols), lambda b, g, i: (b, i, g))
    kspec = pl.BlockSpec((None, None, S, LANES), lambda b, g, i: (b, g, 0, 0))
    vspec = pl.BlockSpec((None, None, S // TK_A, VT_ROWS, TK_A), lambda b, g, i: (b, g, 0, 0, 0))
    return pl.pallas_call(
        _attn_a_kernel,
        grid=(B, N_KV_A, S // TQ_A),
        in_specs=[qspec, kspec, vspec],
        out_specs=qspec,
        out_shape=jax.ShapeDtypeStruct((B, S, A_Q), BF16),
        scratch_shapes=[pltpu.VMEM((pairs, 2 * TQ_A, LANES), BF16),
                        pltpu.VMEM((pairs, 1, 2 * TQ_A), F32),
                        pltpu.VMEM((pairs, VT_ROWS, 2 * TQ_A), F32),
                        pltpu.VMEM((pairs, TK_A, 2 * TQ_A), F32),
                        pltpu.VMEM((pairs, TK_A, 2 * TQ_A), F32),
                        pltpu.VMEM((pairs, 1, 2 * TQ_A), F32),
                        pltpu.VMEM((pairs, 1, 2 * TQ_A), F32)],
        compiler_params=pltpu.CompilerParams(
            dimension_semantics=("arbitrary", "arbitrary", "arbitrary"),
            vmem_limit_bytes=VMEM_LIMIT),
        name="attn_a",
    )(qa, ka, vta)


def _attn_b_kernel(q_ref, kp_ref, kc_ref, kn_ref, vp_ref, vc_ref, vn_ref,
                   o_ref, st_ref, kbuf, vbuf, *, seq_len):
    j = pl.program_id(2)
    kbuf[0:SPAN_B, :] = kp_ref[...]
    kbuf[SPAN_B:SPAN_B + TQ_B, :] = kc_ref[...]
    kbuf[SPAN_B + TQ_B:, :] = kn_ref[...]
    vbuf[0:SPAN_B, :] = vp_ref[...]
    vbuf[SPAN_B:SPAN_B + TQ_B, :] = vc_ref[...]
    vbuf[SPAN_B + TQ_B:, :] = vn_ref[...]

    n_keys = SUB_B + 2 * SPAN_B
    c_idx = lax.broadcasted_iota(jnp.int32, (n_keys, 2 * SUB_B), 0)
    a_idx = lax.broadcasted_iota(jnp.int32, (n_keys, 2 * SUB_B), 1) % SUB_B
    band_bias = jnp.where(jnp.abs(c_idx - SPAN_B - a_idx) <= SPAN_B, 0.0, NEG_INF).astype(F32)
    low = lax.broadcasted_iota(jnp.int32, (SUB_B, LANES), 1) < HEAD_DIM
    ones = jnp.ones((VT_ROWS - HEAD_DIM, n_keys), BF16)
    pad = jnp.zeros((LANES - 2 * N_HEADS_B, SUB_B), F32)

    n_sub = TQ_B // SUB_B
    n_pairs = N_HEADS_B // 2
    biases = {}

    def bias_of(i):
        if i not in biases:
            first_key = j * TQ_B + i * SUB_B - SPAN_B
            bias = band_bias
            if i == 0:
                bias = jnp.where(c_idx >= -first_key, bias, NEG_INF)
            if i == n_sub - 1:
                bias = jnp.where(c_idx < seq_len - first_key, bias, NEG_INF)
            biases[i] = bias
        return biases[i]

    def scores(i, hp):
        q0, cols = i * SUB_B, slice(hp * LANES, (hp + 1) * LANES)
        qp = q_ref[q0:q0 + SUB_B, cols]
        zero = jnp.zeros_like(qp)
        qs = jnp.concatenate([jnp.where(low, qp, zero), jnp.where(low, zero, qp)], axis=0)
        return _dot(kbuf[q0:q0 + n_keys, cols], qs.T) + bias_of(i)

    def finish(i, hp, s):
        q0, cols = i * SUB_B, slice(hp * LANES, (hp + 1) * LANES)
        m = jnp.max(s, axis=0, keepdims=True)
        p = jnp.exp2(s - m).astype(BF16)
        vt = jnp.concatenate([vbuf[q0:q0 + n_keys, cols].T, ones], axis=0)
        o_all = _dot(vt, p)
        o_t = jnp.concatenate([o_all[0:HEAD_DIM, 0:SUB_B],
                               o_all[HEAD_DIM:LANES, SUB_B:]], axis=0)
        o_ref[q0:q0 + SUB_B, cols] = o_t.T.astype(o_ref.dtype)
        l = o_all[LANES:LANES + 1, :]
        return [m[:, 0:SUB_B], m[:, SUB_B:]], [l[:, 0:SUB_B], l[:, SUB_B:]]

    stream = [(i, hp) for i in range(n_sub) for hp in range(n_pairs)]
    depth = 6
    ahead = [scores(*stream[u]) for u in range(depth)]
    ms, ls = [], []
    for u, (i, hp) in enumerate(stream):
        if u + depth < len(stream):
            ahead.append(scores(*stream[u + depth]))
        m2, l2 = finish(i, hp, ahead.pop(0))
        ms += m2
        ls += l2
        if hp == n_pairs - 1:
            st_t = jnp.concatenate(ms + ls + [pad], axis=0)
            st_ref[i * SUB_B:(i + 1) * SUB_B, :] = st_t.T
            ms, ls = [], []


def _attn_b_call(qb, kb, vb):
    B, d, L, C = qb.shape
    nh = L // SPAN_B
    per = TQ_B // SPAN_B
    cur = pl.BlockSpec((None, None, TQ_B, C), lambda b, r, j: (b, r, j, 0))
    prev = pl.BlockSpec((None, None, SPAN_B, C),
                        lambda b, r, j: (b, r, jnp.maximum(j * per - 1, 0), 0))
    nxt = pl.BlockSpec((None, None, SPAN_B, C),
                       lambda b, r, j: (b, r, jnp.minimum((j + 1) * per, nh - 1), 0))
    st_spec = pl.BlockSpec((None, None, TQ_B, LANES), lambda b, r, j: (b, r, j, 0))
    return pl.pallas_call(
        functools.partial(_attn_b_kernel, seq_len=L),
        grid=(B, d, L // TQ_B),
        in_specs=[cur, prev, cur, nxt, prev, cur, nxt],
        out_specs=[cur, st_spec],
        out_shape=[jax.ShapeDtypeStruct((B, d, L, C), BF16),
                   jax.ShapeDtypeStruct((B, d, L, LANES), F32)],
        scratch_shapes=[pltpu.VMEM((TQ_B + 2 * SPAN_B, C), BF16),
                        pltpu.VMEM((TQ_B + 2 * SPAN_B, C), BF16)],
        compiler_params=pltpu.CompilerParams(
            dimension_semantics=("arbitrary", "arbitrary", "arbitrary"),
            vmem_limit_bytes=VMEM_LIMIT),
        name=f"attn_b_d{d}",
    )(qb, kb, kb, kb, vb, vb, vb)


def _natural_order(ref, scratch):
    d, _, cols = ref.shape
    if d == 1:
        return ref[0].astype(F32)
    for r in range(d):
        for c in range(cols // LANES):
            scratch[c, pl.ds(r, TM // d, stride=d), :] = ref[r, :, c * LANES:(c + 1) * LANES].astype(F32)
    return jnp.concatenate([scratch[c] for c in range(cols // LANES)], axis=1)


def _out_ffn_kernel(x_ref, ha_ref, o1_ref, o2_ref, o3_ref, s1_ref, s2_ref, s3_ref, ex_ref,
                    wo_ref, mixg_ref, pre_ref, post_ref, wg_ref, wu_ref, wd_ref, y_ref,
                    of2_sc, of3_sc, sf2_sc, sf3_sc):
    stats = (_natural_order(s1_ref, None), _natural_order(s2_ref, sf2_sc),
             _natural_order(s3_ref, sf3_sc))
    parts = (_natural_order(o1_ref, None), _natural_order(o2_ref, of2_sc),
             _natural_order(o3_ref, of3_sc))
    is_max = lax.broadcasted_iota(jnp.int32, (TM, LANES), 1) < N_HEADS_B
    m_all = jnp.maximum(jnp.maximum(stats[0], stats[1]), stats[2])
    es = [jnp.exp2(s - m_all) for s in stats]
    den = None
    for e, s in zip(es, stats):
        term = e * pltpu.roll(s, LANES - N_HEADS_B, 1)
        den = term if den is None else den + term
    heads_b = None
    for e, o in zip(es, parts):
        w = jnp.where(is_max, e / den, 0.0)
        hi = w.astype(BF16)
        lo = (w - hi.astype(F32)).astype(BF16)
        term = (_dot(hi, ex_ref[...]) + _dot(lo, ex_ref[...])) * o
        heads_b = term if heads_b is None else heads_b + term

    mixed = _dot(ha_ref[...], wo_ref[0:A_Q, :]) + _dot(heads_b.astype(BF16), wo_ref[A_Q:, :])
    x2 = x_ref[...] + _rms(mixed, mixg_ref[...])
    y_ref[...] = _swiglu_half_step(x2, pre_ref[...], post_ref[...], wg_ref, wu_ref, wd_ref)


def _out_ffn_call(x1, heads_a, parts, expand, wo, mix_g, pre_g, post_g, wg, wu, wd):
    n = x1.shape[0]
    per_batch = parts[0][0].shape[2] // TM
    row = lambda c: pl.BlockSpec((TM, c), lambda i: (i, 0))

    def strided(a):
        _, d, _, c = a.shape
        return pl.BlockSpec((None, d, TM // d, c), lambda i: (i // per_batch, 0, i % per_batch, 0))

    (o1, s1), (o2, s2), (o3, s3) = parts
    return pl.pallas_call(
        _out_ffn_kernel,
        grid=(n // TM,),
        in_specs=[row(D_MODEL), row(A_Q), strided(o1), strided(o2), strided(o3),
                  strided(s1), strided(s2), strided(s3), _resident(expand.shape),
                  _resident(wo.shape), _resident((1, D_MODEL)), _resident((1, D_MODEL)),
                  _resident((1, D_MODEL)), _resident(wg.shape), _resident(wu.shape),
                  _resident(wd.shape)],
        out_specs=row(D_MODEL),
        out_shape=jax.ShapeDtypeStruct((n, D_MODEL), F32),
        scratch_shapes=[pltpu.VMEM((B_QKV // LANES, TM, LANES), F32),
                        pltpu.VMEM((B_QKV // LANES, TM, LANES), F32),
                        pltpu.VMEM((1, TM, LANES), F32), pltpu.VMEM((1, TM, LANES), F32)],
        compiler_params=pltpu.CompilerParams(
            dimension_semantics=("arbitrary",), vmem_limit_bytes=VMEM_LIMIT),
        name="out_ffn2",
    )(x1, heads_a, o1, o2, o3, s1, s2, s3, expand, wo, mix_g, pre_g, post_g, wg, wu, wd)


def _rope_tables(seq):
    pos = jnp.arange(seq, dtype=jnp.int32)
    row = (pos // GRID_W).astype(F32)[:, None]
    col = (pos % GRID_W).astype(F32)[:, None]
    dim_a = HEAD_DIM // 2
    fa = ROPE_THETA ** (-jnp.arange(0, dim_a, 2, dtype=F32) / dim_a)
    fb = ROPE_THETA ** (-jnp.arange(0, HEAD_DIM, 2, dtype=F32) / HEAD_DIM)
    ar, ac = row * fa[None, :], col * fa[None, :]
    ab = pos.astype(F32)[:, None] * fb[None, :]
    cos_a = jnp.concatenate([jnp.cos(ar), jnp.cos(ar), jnp.cos(ac), jnp.cos(ac)], axis=-1)
    sin_a = jnp.concatenate([-jnp.sin(ar), jnp.sin(ar), -jnp.sin(ac), jnp.sin(ac)], axis=-1)
    cos_b = jnp.concatenate([jnp.cos(ab), jnp.cos(ab)], axis=-1)
    sin_b = jnp.concatenate([-jnp.sin(ab), jnp.sin(ab)], axis=-1)
    two = lambda t: jnp.tile(t, (1, LANES // HEAD_DIM))
    return two(cos_a), two(sin_a), two(cos_b), two(sin_b)


def _layer(x, ffn1_pre_g, ffn1_post_g, ffn1_w_gate, ffn1_w_up, ffn1_w_down,
           mix_pre_g, mix_post_g, w_qkv, q_norm_g, k_norm_g, w_out,
           ffn2_pre_g, ffn2_post_g, ffn2_w_gate, ffn2_w_up, ffn2_w_down, tables, seg, expand):
    B, S, D = x.shape
    vec = lambda g: g.reshape(1, -1).astype(F32)
    bf = lambda w: w.astype(BF16)

    x1 = _ffn_call(x.reshape(B * S, D), vec(ffn1_pre_g), vec(ffn1_post_g),
                   bf(ffn1_w_gate), bf(ffn1_w_up), bf(ffn1_w_down))
    heads_per_tile = 2 * LANES // HEAD_DIM
    kv_gain = jnp.concatenate([jnp.tile(vec(k_norm_g), (1, N_KV_A)), jnp.ones((1, A_KV), F32)], axis=1)
    qa, ka, vta, qkv_b = _qkv_call(x1.reshape(B, S, D), vec(mix_pre_g), bf(w_qkv), seg,
                                   jnp.tile(vec(q_norm_g), (1, heads_per_tile)), kv_gain, *tables)
    score_bound = ((HEAD_DIM * SCALE * LOG2E) * jnp.max(jnp.abs(q_norm_g))
                   * jnp.max(jnp.abs(k_norm_g)))
    heads_a = lax.cond(score_bound <= MAX_SCORE_BOUND,
                       lambda: _attn_a_bounded_call(qa, ka, vta),
                       lambda: _attn_a_call(qa, ka, vta))
    parts = [_attn_b_call(*qkv) for qkv in qkv_b]
    y = _out_ffn_call(x1, heads_a.reshape(B * S, A_Q), parts, expand,
                      bf(w_out), vec(mix_post_g), vec(ffn2_pre_g), vec(ffn2_post_g),
                      bf(ffn2_w_gate), bf(ffn2_w_up), bf(ffn2_w_down))
    return y.reshape(B, S, D)


def kernel(x, ffn1_pre_g, ffn1_post_g, ffn1_w_gate, ffn1_w_up, ffn1_w_down, mix_pre_g, mix_post_g, w_qkv, q_norm_g, k_norm_g, w_out, ffn2_pre_g, ffn2_post_g, ffn2_w_gate, ffn2_w_up, ffn2_w_down):
    assert all(w // 2 // d == SPAN_B for w, d in DILATED_CONFIGS) and TK_A % TM == 0
    S = x.shape[1]
    tables = _rope_tables(S)
    head_of_lane = jnp.arange(2 * LANES) // HEAD_DIM
    seg = (head_of_lane[:, None] == head_of_lane[None, :]).astype(BF16) / HEAD_DIM
    expand = (jnp.arange(LANES)[:, None] == jnp.arange(B_QKV)[None, :] // HEAD_DIM).astype(BF16)
    params = (ffn1_pre_g, ffn1_post_g, ffn1_w_gate, ffn1_w_up, ffn1_w_down, mix_pre_g, mix_post_g,
              w_qkv, q_norm_g, k_norm_g, w_out, ffn2_pre_g, ffn2_post_g, ffn2_w_gate, ffn2_w_up,
              ffn2_w_down)
    for l in range(ffn1_pre_g.shape[0]):
        x = _layer(x, *(p[l] for p in params), tables, seg, expand)
    return x
```

```python
import functools

import jax
import jax.numpy as jnp
from jax import lax
from jax.experimental import pallas as pl
from jax.experimental.pallas import tpu as pltpu

D_MODEL = 1024
HEAD_DIM = 64
N_HEADS_A = 8
N_KV_A = 2
N_HEADS_B = 8
DILATED_CONFIGS = ((128, 1), (512, 4), (2048, 16))
DILATIONS = tuple(d for _, d in DILATED_CONFIGS)
GRID_W = 64
ROPE_THETA = 10000.0
D_FF = 2816
EPS = 1e-6
NEG_INF = -1e30

A_Q = N_HEADS_A * HEAD_DIM
A_KV = N_KV_A * HEAD_DIM
B_QKV = N_HEADS_B * HEAD_DIM
QKV_COLS = A_Q + 2 * A_KV + 3 * B_QKV
SCALE = HEAD_DIM ** -0.5
LOG2E = 1.4426950408889634

LANES = 128
FF_CHUNKS = ((0, 1024), (1024, 2048), (2048, 2816))
TM = 512
TQ_A = 256
TK_A = 1024
TQ_BOUNDED = 512
QH_A = 256
SUB_A = 256
MAX_SCORE_BOUND = 48.0
VT_ROWS = HEAD_DIM + 16
TQ_B = 1024
SUB_B = 128
SPAN_B = 64
VMEM_LIMIT = 52 * 1024 * 1024

BF16 = jnp.bfloat16
F32 = jnp.float32


def _dot(a, b):
    return jnp.dot(a, b, preferred_element_type=F32)


def _dot_nt(a, b):
    return lax.dot_general(a, b, (((1,), (1,)), ((), ())), preferred_element_type=F32)


def _rms(x, g):
    ms = jnp.mean(x * x, axis=-1, keepdims=True)
    return x * lax.rsqrt(ms + EPS) * g


def _swiglu_half_step(x, pre_g, post_g, wg_ref, wu_ref, wd_ref):
    h = _rms(x, pre_g).astype(BF16)
    f = None
    for lo, hi in FF_CHUNKS:
        g = _dot(h, wg_ref[:, lo:hi])
        u = _dot(h, wu_ref[:, lo:hi])
        a = (g / (1.0 + jnp.exp(-g)) * u).astype(BF16)
        part = _dot(a, wd_ref[lo:hi, :])
        f = part if f is None else f + part
    return x + 0.5 * _rms(f, post_g)


def _ffn_kernel(x_ref, pre_ref, post_ref, wg_ref, wu_ref, wd_ref, o_ref):
    o_ref[...] = _swiglu_half_step(x_ref[...], pre_ref[...], post_ref[...],
                                   wg_ref, wu_ref, wd_ref)


def _resident(shape):
    nd = len(shape)
    return pl.BlockSpec(shape, lambda *_: (0,) * nd, pipeline_mode=pl.Buffered(1))


def _ffn_call(x2d, pre_g, post_g, wg, wu, wd):
    n = x2d.shape[0]
    row = pl.BlockSpec((TM, D_MODEL), lambda i: (i, 0))
    return pl.pallas_call(
        _ffn_kernel,
        grid=(n // TM,),
        in_specs=[row, _resident((1, D_MODEL)), _resident((1, D_MODEL)),
                  _resident(wg.shape), _resident(wu.shape), _resident(wd.shape)],
        out_specs=row,
        out_shape=jax.ShapeDtypeStruct(x2d.shape, F32),
        compiler_params=pltpu.CompilerParams(
            dimension_semantics=("arbitrary",), vmem_limit_bytes=VMEM_LIMIT),
        name="ffn1",
    )(x2d, pre_g, post_g, wg, wu, wd)


def _rope(c, cos, sin_signed, half):
    lane = lax.broadcasted_iota(jnp.int32, c.shape, 1)
    first = (lane % (2 * half)) < half
    partner = jnp.where(first, pltpu.roll(c, LANES - half, 1), pltpu.roll(c, half, 1))
    return c * cos + partner * sin_signed


def _head_rms(c, seg_ref, g):
    sq = c * c
    hi = sq.astype(BF16)
    lo = (sq - hi.astype(F32)).astype(BF16)
    ms = _dot(hi, seg_ref[...]) + _dot(lo, seg_ref[...])
    return c * lax.rsqrt(ms + EPS) * g


def _dup_halves(c):
    lane = lax.broadcasted_iota(jnp.int32, c.shape, 1)
    low = lane < HEAD_DIM
    r = pltpu.roll(c, HEAD_DIM, 1)
    return jnp.where(low, c, r), jnp.where(low, r, c)


def _qkv_kernel(x_ref, pre_ref, w_ref, seg_ref, qg_ref, kg_ref,
                cosa_ref, sina_ref, cosb_ref, sinb_ref,
                qa_ref, ka_ref, va_ref, *rest):
    b_refs, (qf_sc, kf_sc, vf_sc) = rest[:-3], rest[-3:]
    h = _rms(x_ref[...], pre_ref[...]).astype(BF16)
    cosa, sina = cosa_ref[...], sina_ref[...]
    cosb, sinb = cosb_ref[...], sinb_ref[...]
    wide = 2 * LANES

    qa = _dot(h, w_ref[:, 0:A_Q])
    for t in range(A_Q // wide):
        qn = _head_rms(qa[:, t * wide:(t + 1) * wide], seg_ref, qg_ref[...])
        for c in range(wide // LANES):
            q = _rope(qn[:, c * LANES:(c + 1) * LANES], cosa, sina, HEAD_DIM // 4) * (SCALE * LOG2E)
            col = t * wide + c * LANES
            qa_ref[:, col:col + LANES] = q.astype(BF16)

    kv = _dot(h, w_ref[:, A_Q:A_Q + 2 * A_KV])
    k = _rope(_head_rms(kv, seg_ref, kg_ref[...])[:, 0:A_KV], cosa, sina, HEAD_DIM // 4)
    k0, k1 = _dup_halves(k)
    ka_ref[0] = k0.astype(BF16)
    ka_ref[1] = k1.astype(BF16)
    vt = kv[:, A_KV:].T.astype(BF16)
    ones = jnp.ones((VT_ROWS - HEAD_DIM, TM), BF16)
    for g in range(N_KV_A):
        va_ref[g, 0:HEAD_DIM, :] = vt[g * HEAD_DIM:(g + 1) * HEAD_DIM, :]
        va_ref[g, HEAD_DIM:, :] = ones

    base = A_Q + 2 * A_KV
    qb = _dot(h, w_ref[:, base:base + B_QKV])
    kb = _dot(h, w_ref[:, base + B_QKV:base + 2 * B_QKV])
    vb = _dot(h, w_ref[:, base + 2 * B_QKV:base + 3 * B_QKV])
    for c in range(B_QKV // LANES):
        sl = slice(c * LANES, (c + 1) * LANES)
        qf_sc[c] = _rope(qb[:, sl], cosb, sinb, HEAD_DIM // 2) * (SCALE * LOG2E)
        kf_sc[c] = _rope(kb[:, sl], cosb, sinb, HEAD_DIM // 2)
        vf_sc[c] = vb[:, sl]
    for j, src in enumerate((qf_sc, kf_sc, vf_sc)):
        for i, d in enumerate(DILATIONS):
            out = b_refs[3 * i + j]
            for r in range(d):
                rows = slice(None) if d == 1 else pl.ds(r, TM // d, stride=d)
                for c in range(B_QKV // LANES):
                    out[r, :, c * LANES:(c + 1) * LANES] = src[c, rows, :].astype(BF16)


def _qkv_call(x1, pre_g, w, seg, qg, kg, cosa, sina, cosb, sinb):
    B, S, _ = x1.shape
    tab = pl.BlockSpec((TM, LANES), lambda i, b: (i, 0))
    wide = pl.BlockSpec((None, TM, A_Q), lambda i, b: (b, i, 0))
    dup = pl.BlockSpec((None, N_KV_A, TM, LANES), lambda i, b: (b, 0, i, 0))
    per = TK_A // TM
    vt = pl.BlockSpec((None, N_KV_A, None, VT_ROWS, TM), lambda i, b: (b, 0, i // per, 0, i % per))
    wide_shape = jax.ShapeDtypeStruct((B, S, A_Q), BF16)
    dup_shape = jax.ShapeDtypeStruct((B, N_KV_A, S, LANES), BF16)
    vt_shape = jax.ShapeDtypeStruct((B, N_KV_A, S // TK_A, VT_ROWS, TK_A), BF16)
    b_specs, b_shapes = [], []
    for d in DILATIONS:
        b_specs += [pl.BlockSpec((None, d, TM // d, B_QKV), lambda i, b: (b, 0, i, 0))] * 3
        b_shapes += [jax.ShapeDtypeStruct((B, d, S // d, B_QKV), BF16)] * 3
    outs = pl.pallas_call(
        _qkv_kernel,
        grid=(S // TM, B),
        in_specs=[pl.BlockSpec((None, TM, D_MODEL), lambda i, b: (b, i, 0)),
                  _resident((1, D_MODEL)), _resident(w.shape), _resident(seg.shape),
                  _resident(qg.shape), _resident(kg.shape), tab, tab, tab, tab],
        out_specs=[wide, dup, vt] + b_specs,
        out_shape=[wide_shape, dup_shape, vt_shape] + b_shapes,
        scratch_shapes=[pltpu.VMEM((B_QKV // LANES, TM, LANES), F32)] * 3,
        compiler_params=pltpu.CompilerParams(
            dimension_semantics=("arbitrary", "arbitrary"), vmem_limit_bytes=VMEM_LIMIT),
        name="qkv",
    )(x1, pre_g, w, seg, qg, kg, cosa, sina, cosb, sinb)
    qa, ka, vta = outs[:3]
    qkv_b = [tuple(outs[3 + 3 * i:6 + 3 * i]) for i in range(len(DILATIONS))]
    return qa, ka, vta, qkv_b


def _attn_a_kernel(q_ref, k_ref, vt_ref, o_ref, qs_sc, m_sc, acc_sc, sa_sc, sb_sc, cma_sc, cmb_sc):
    tq = q_ref.shape[0]
    n_pairs = qs_sc.shape[0]
    low = lax.broadcasted_iota(jnp.int32, (tq, LANES), 1) < HEAD_DIM
    for c in range(n_pairs):
        qc = q_ref[:, c * LANES:(c + 1) * LANES]
        zero = jnp.zeros_like(qc)
        qs_sc[c, 0:tq, :] = jnp.where(low, qc, zero)
        qs_sc[c, tq:, :] = jnp.where(low, zero, qc)

    m_sc[...] = jnp.full(m_sc.shape, NEG_INF, F32)
    acc_sc[...] = jnp.zeros(acc_sc.shape, F32)

    n_k = vt_ref.shape[0]

    def scores(kb, s_out, cm_out):
        k = k_ref[pl.ds(pl.multiple_of(kb * TK_A, TK_A), TK_A), :]
        for c in range(n_pairs):
            s = _dot_nt(k, qs_sc[c])
            s_out[c] = s
            cm_out[c] = jnp.max(s, axis=0, keepdims=True)

    def consume(kb, s_in, cm_in):
        vt = vt_ref[kb]
        for c in range(n_pairs):
            m_old = m_sc[c]
            m_new = jnp.maximum(m_old, cm_in[c])
            alpha = jnp.exp2(m_old - m_new)
            p = jnp.exp2(s_in[c] - m_new).astype(BF16)
            acc_sc[c] = alpha * acc_sc[c] + _dot(vt, p)
            m_sc[c] = m_new

    bufs = ((sa_sc, cma_sc), (sb_sc, cmb_sc))
    scores(0, *bufs[0])

    def fused(kb, cur, nxt):
        s_in, cm_in = cur
        s_out, cm_out = nxt
        k0 = pl.multiple_of((kb + 1) * TK_A, TK_A)
        m_new, alpha, pv, cm = [], [], [], []
        for c in range(n_pairs):
            m_old = m_sc[c]
            m_new.append(jnp.maximum(m_old, cm_in[c]))
            alpha.append(jnp.exp2(m_old - m_new[c]))
            pv.append(None)
            cm.append(None)
        for j in range(TK_A // SUB_A):
            rows = slice(j * SUB_A, (j + 1) * SUB_A)
            k = k_ref[pl.ds(k0 + j * SUB_A, SUB_A), :]
            vt = vt_ref[kb, :, rows]
            for c in range(n_pairs):
                s = _dot_nt(k, qs_sc[c])
                s_out[c, rows, :] = s
                part = jnp.max(s, axis=0, keepdims=True)
                cm[c] = part if cm[c] is None else jnp.maximum(cm[c], part)
            for c in range(n_pairs):
                p = jnp.exp2(s_in[c, rows, :] - m_new[c]).astype(BF16)
                part = _dot(vt, p)
                pv[c] = part if pv[c] is None else pv[c] + part
        for c in range(n_pairs):
            cm_out[c] = cm[c]
            acc_sc[c] = alpha[c] * acc_sc[c] + pv[c]
            m_sc[c] = m_new[c]

    def body(kb, carry):
        for parity in range(2):
            @pl.when(kb % 2 == parity)
            def _():
                fused(kb, bufs[parity], bufs[1 - parity])
        return carry

    lax.fori_loop(0, n_k - 1, body, 0)
    consume(n_k - 1, *bufs[(n_k - 1) % 2])
    for c in range(n_pairs):
        o = acc_sc[c, 0:HEAD_DIM, :] / acc_sc[c, HEAD_DIM:HEAD_DIM + 1, :]
        pair = jnp.concatenate([o[:, 0:tq], o[:, tq:]], axis=0)
        o_ref[:, c * LANES:(c + 1) * LANES] = pair.T.astype(o_ref.dtype)


def _attn_a_bounded_kernel(q_ref, k_ref, vt_ref, o_ref, qt_sc, acc_sc):
    n_pairs = q_ref.shape[1] // LANES
    units = [(c, h) for h in range(q_ref.shape[0] // QH_A) for c in range(n_pairs)]
    low = lax.broadcasted_iota(jnp.int32, (QH_A, LANES), 1) < HEAD_DIM
    for u, (c, h) in enumerate(units):
        qc = q_ref[h * QH_A:(h + 1) * QH_A, c * LANES:(c + 1) * LANES]
        zero = jnp.zeros_like(qc)
        qs = jnp.concatenate([jnp.where(low, qc, zero), jnp.where(low, zero, qc)], axis=0)
        qt_sc[u] = qs.T
    n_sub = TK_A // SUB_A
    stream = [(t, u) for t in range(vt_ref.shape[0] * n_sub) for u in range(len(units))]

    def scores(i):
        t, u = stream[i]
        return _dot(k_ref[t * SUB_A:(t + 1) * SUB_A, :], qt_sc[u])

    depth = n_pairs
    ahead = [scores(i) for i in range(depth)]
    pv = [None] * len(units)
    for i, (t, u) in enumerate(stream):
        if i + depth < len(stream):
            ahead.append(scores(i + depth))
        vt = vt_ref[t // n_sub, :, (t % n_sub) * SUB_A:(t % n_sub + 1) * SUB_A]
        part = _dot(vt, jnp.exp2(ahead.pop(0)).astype(BF16))
        pv[u] = part if pv[u] is None else pv[u] + part
    for u in range(len(units)):
        acc_sc[u] = pv[u]
    for u, (c, h) in enumerate(units):
        o = acc_sc[u, 0:HEAD_DIM, :] / acc_sc[u, HEAD_DIM:HEAD_DIM + 1, :]
        pair = jnp.concatenate([o[:, 0:QH_A], o[:, QH_A:]], axis=0)
        o_ref[h * QH_A:(h + 1) * QH_A, c * LANES:(c + 1) * LANES] = pair.T.astype(o_ref.dtype)


def _attn_a_bounded_call(qa, ka, vta):
    B, S, _ = qa.shape
    qcols = A_Q // N_KV_A
    n_units = (qcols // LANES) * (TQ_BOUNDED // QH_A)
    qspec = pl.BlockSpec((None, TQ_BOUNDED, qcols), lambda b, g, i: (b, i, g))
    kspec = pl.BlockSpec((None, None, S, LANES), lambda b, g, i: (b, g, 0, 0))
    vspec = pl.BlockSpec((None, None, S // TK_A, VT_ROWS, TK_A), lambda b, g, i: (b, g, 0, 0, 0))
    return pl.pallas_call(
        _attn_a_bounded_kernel,
        grid=(B, N_KV_A, S // TQ_BOUNDED),
        in_specs=[qspec, kspec, vspec],
        out_specs=qspec,
        out_shape=jax.ShapeDtypeStruct((B, S, A_Q), BF16),
        scratch_shapes=[pltpu.VMEM((n_units, LANES, 2 * QH_A), BF16),
                        pltpu.VMEM((n_units, VT_ROWS, 2 * QH_A), F32)],
        compiler_params=pltpu.CompilerParams(
            dimension_semantics=("arbitrary", "arbitrary", "arbitrary"),
            vmem_limit_bytes=VMEM_LIMIT),
        name="attn_a_bounded",
    )(qa, ka, vta)


def _attn_a_call(qa, ka, vta):
    B, S, _ = qa.shape
    qcols = A_Q // N_KV_A
    pairs = N_HEADS_A // N_KV_A // 2
    qspec = pl.BlockSpec((None, TQ_A, qcols), lambda b, g, i: (b, i, g))
    kspec = pl.BlockSpec((None, None, S, LANES), lambda b, g, i: (b, g, 0, 0))
    vspec = pl.BlockSpec((None, None, S // TK_A, VT_ROWS, TK_A), lambda b, g, i: (b, g, 0, 0, 0))
    return pl.pallas_call(
        _attn_a_kernel,
        grid=(B, N_KV_A, S // TQ_A),
        in_specs=[qspec, kspec, vspec],
        out_specs=qspec,
        out_shape=jax.ShapeDtypeStruct((B, S, A_Q), BF16),
        scratch_shapes=[pltpu.VMEM((pairs, 2 * TQ_A, LANES), BF16),
                        pltpu.VMEM((pairs, 1, 2 * TQ_A), F32),
                        pltpu.VMEM((pairs, VT_ROWS, 2 * TQ_A), F32),
                        pltpu.VMEM((pairs, TK_A, 2 * TQ_A), F32),
                        pltpu.VMEM((pairs, TK_A, 2 * TQ_A), F32),
                        pltpu.VMEM((pairs, 1, 2 * TQ_A), F32),
                        pltpu.VMEM((pairs, 1, 2 * TQ_A), F32)],
        compiler_params=pltpu.CompilerParams(
            dimension_semantics=("arbitrary", "arbitrary", "arbitrary"),
            vmem_limit_bytes=VMEM_LIMIT),
        name="attn_a",
    )(qa, ka, vta)


def _attn_b_kernel(q_ref, kp_ref, kc_ref, kn_ref, vp_ref, vc_ref, vn_ref,
                   o_ref, st_ref, kbuf, vbuf, *, seq_len):
    j = pl.program_id(2)
    tq = q_ref.shape[0]
    kbuf[0:SPAN_B, :] = kp_ref[...]
    kbuf[SPAN_B:SPAN_B + tq, :] = kc_ref[...]
    kbuf[SPAN_B + tq:, :] = kn_ref[...]
    vbuf[0:SPAN_B, :] = vp_ref[...]
    vbuf[SPAN_B:SPAN_B + tq, :] = vc_ref[...]
    vbuf[SPAN_B + tq:, :] = vn_ref[...]

    n_keys = SUB_B + 2 * SPAN_B
    c_idx = lax.broadcasted_iota(jnp.int32, (n_keys, 2 * SUB_B), 0)
    a_idx = lax.broadcasted_iota(jnp.int32, (n_keys, 2 * SUB_B), 1) % SUB_B
    band_bias = jnp.where(jnp.abs(c_idx - SPAN_B - a_idx) <= SPAN_B, 0.0, NEG_INF).astype(F32)
    low = lax.broadcasted_iota(jnp.int32, (SUB_B, LANES), 1) < HEAD_DIM
    ones = jnp.ones((VT_ROWS - HEAD_DIM, n_keys), BF16)
    pad = jnp.zeros((LANES - 2 * N_HEADS_B, SUB_B), F32)

    n_sub = tq // SUB_B
    n_pairs = N_HEADS_B // 2
    biases = {}

    def bias_of(i):
        if i not in biases:
            first_key = j * tq + i * SUB_B - SPAN_B
            bias = band_bias
            if i == 0:
                bias = jnp.where(c_idx >= -first_key, bias, NEG_INF)
            if i == n_sub - 1:
                bias = jnp.where(c_idx < seq_len - first_key, bias, NEG_INF)
            biases[i] = bias
        return biases[i]

    def scores(i, hp):
        q0, cols = i * SUB_B, slice(hp * LANES, (hp + 1) * LANES)
        qp = q_ref[q0:q0 + SUB_B, cols]
        zero = jnp.zeros_like(qp)
        qs = jnp.concatenate([jnp.where(low, qp, zero), jnp.where(low, zero, qp)], axis=0)
        return _dot(kbuf[q0:q0 + n_keys, cols], qs.T) + bias_of(i)

    def finish(i, hp, s):
        q0, cols = i * SUB_B, slice(hp * LANES, (hp + 1) * LANES)
        m = jnp.max(s, axis=0, keepdims=True)
        p = jnp.exp2(s - m).astype(BF16)
        vt = jnp.concatenate([vbuf[q0:q0 + n_keys, cols].T, ones], axis=0)
        o_all = _dot(vt, p)
        o_t = jnp.concatenate([o_all[0:HEAD_DIM, 0:SUB_B],
                               o_all[HEAD_DIM:LANES, SUB_B:]], axis=0)
        o_ref[q0:q0 + SUB_B, cols] = o_t.T.astype(o_ref.dtype)
        l = o_all[LANES:LANES + 1, :]
        return [m[:, 0:SUB_B], m[:, SUB_B:]], [l[:, 0:SUB_B], l[:, SUB_B:]]

    stream = [(i, hp) for i in range(n_sub) for hp in range(n_pairs)]
    depth = 6
    ahead = [scores(*stream[u]) for u in range(depth)]
    ms, ls = [], []
    for u, (i, hp) in enumerate(stream):
        if u + depth < len(stream):
            ahead.append(scores(*stream[u + depth]))
        m2, l2 = finish(i, hp, ahead.pop(0))
        ms += m2
        ls += l2
        if hp == n_pairs - 1:
            st_t = jnp.concatenate(ms + ls + [pad], axis=0)
            st_ref[i * SUB_B:(i + 1) * SUB_B, :] = st_t.T
            ms, ls = [], []


def _attn_b_call(qb, kb, vb):
    B, d, L, C = qb.shape
    nh = L // SPAN_B
    tq = min(TQ_B, L)
    per = tq // SPAN_B
    cur = pl.BlockSpec((None, None, tq, C), lambda b, r, j: (b, r, j, 0))
    prev = pl.BlockSpec((None, None, SPAN_B, C),
                        lambda b, r, j: (b, r, jnp.maximum(j * per - 1, 0), 0))
    nxt = pl.BlockSpec((None, None, SPAN_B, C),
                       lambda b, r, j: (b, r, jnp.minimum((j + 1) * per, nh - 1), 0))
    st_spec = pl.BlockSpec((None, None, tq, LANES), lambda b, r, j: (b, r, j, 0))
    return pl.pallas_call(
        functools.partial(_attn_b_kernel, seq_len=L),
        grid=(B, d, L // tq),
        in_specs=[cur, prev, cur, nxt, prev, cur, nxt],
        out_specs=[cur, st_spec],
        out_shape=[jax.ShapeDtypeStruct((B, d, L, C), BF16),
                   jax.ShapeDtypeStruct((B, d, L, LANES), F32)],
        scratch_shapes=[pltpu.VMEM((tq + 2 * SPAN_B, C), BF16),
                        pltpu.VMEM((tq + 2 * SPAN_B, C), BF16)],
        compiler_params=pltpu.CompilerParams(
            dimension_semantics=("arbitrary", "arbitrary", "arbitrary"),
            vmem_limit_bytes=VMEM_LIMIT),
        name=f"attn_b_d{d}",
    )(qb, kb, kb, kb, vb, vb, vb)


def _natural_order(ref, scratch):
    d, _, cols = ref.shape
    if d == 1:
        return ref[0].astype(F32)
    for r in range(d):
        for c in range(cols // LANES):
            scratch[c, pl.ds(r, TM // d, stride=d), :] = ref[r, :, c * LANES:(c + 1) * LANES].astype(F32)
    return jnp.concatenate([scratch[c] for c in range(cols // LANES)], axis=1)


def _out_ffn_kernel(x_ref, ha_ref, o1_ref, o2_ref, o3_ref, s1_ref, s2_ref, s3_ref, ex_ref,
                    wo_ref, mixg_ref, pre_ref, post_ref, wg_ref, wu_ref, wd_ref, y_ref,
                    of2_sc, of3_sc, sf2_sc, sf3_sc):
    stats = (_natural_order(s1_ref, None), _natural_order(s2_ref, sf2_sc),
             _natural_order(s3_ref, sf3_sc))
    parts = (_natural_order(o1_ref, None), _natural_order(o2_ref, of2_sc),
             _natural_order(o3_ref, of3_sc))
    is_max = lax.broadcasted_iota(jnp.int32, (TM, LANES), 1) < N_HEADS_B
    m_all = jnp.maximum(jnp.maximum(stats[0], stats[1]), stats[2])
    es = [jnp.exp2(s - m_all) for s in stats]
    den = None
    for e, s in zip(es, stats):
        term = e * pltpu.roll(s, LANES - N_HEADS_B, 1)
        den = term if den is None else den + term
    heads_b = None
    for e, o in zip(es, parts):
        w = jnp.where(is_max, e / den, 0.0)
        hi = w.astype(BF16)
        lo = (w - hi.astype(F32)).astype(BF16)
        term = (_dot(hi, ex_ref[...]) + _dot(lo, ex_ref[...])) * o
        heads_b = term if heads_b is None else heads_b + term

    mixed = _dot(ha_ref[...], wo_ref[0:A_Q, :]) + _dot(heads_b.astype(BF16), wo_ref[A_Q:, :])
    x2 = x_ref[...] + _rms(mixed, mixg_ref[...])
    y_ref[...] = _swiglu_half_step(x2, pre_ref[...], post_ref[...], wg_ref, wu_ref, wd_ref)


def _out_ffn_call(x1, heads_a, parts, expand, wo, mix_g, pre_g, post_g, wg, wu, wd):
    n = x1.shape[0]
    per_batch = parts[0][0].shape[2] // TM
    row = lambda c: pl.BlockSpec((TM, c), lambda i: (i, 0))

    def strided(a):
        _, d, _, c = a.shape
        return pl.BlockSpec((None, d, TM // d, c), lambda i: (i // per_batch, 0, i % per_batch, 0))

    (o1, s1), (o2, s2), (o3, s3) = parts
    return pl.pallas_call(
        _out_ffn_kernel,
        grid=(n // TM,),
        in_specs=[row(D_MODEL), row(A_Q), strided(o1), strided(o2), strided(o3),
                  strided(s1), strided(s2), strided(s3), _resident(expand.shape),
                  _resident(wo.shape), _resident((1, D_MODEL)), _resident((1, D_MODEL)),
                  _resident((1, D_MODEL)), _resident(wg.shape), _resident(wu.shape),
                  _resident(wd.shape)],
        out_specs=row(D_MODEL),
        out_shape=jax.ShapeDtypeStruct((n, D_MODEL), F32),
        scratch_shapes=[pltpu.VMEM((B_QKV // LANES, TM, LANES), F32),
                        pltpu.VMEM((B_QKV // LANES, TM, LANES), F32),
                        pltpu.VMEM((1, TM, LANES), F32), pltpu.VMEM((1, TM, LANES), F32)],
        compiler_params=pltpu.CompilerParams(
            dimension_semantics=("arbitrary",), vmem_limit_bytes=VMEM_LIMIT),
        name="out_ffn2",
    )(x1, heads_a, o1, o2, o3, s1, s2, s3, expand, wo, mix_g, pre_g, post_g, wg, wu, wd)


def _rope_tables(seq):
    pos = jnp.arange(seq, dtype=jnp.int32)
    row = (pos // GRID_W).astype(F32)[:, None]
    col = (pos % GRID_W).astype(F32)[:, None]
    dim_a = HEAD_DIM // 2
    fa = ROPE_THETA ** (-jnp.arange(0, dim_a, 2, dtype=F32) / dim_a)
    fb = ROPE_THETA ** (-jnp.arange(0, HEAD_DIM, 2, dtype=F32) / HEAD_DIM)
    ar, ac = row * fa[None, :], col * fa[None, :]
    ab = pos.astype(F32)[:, None] * fb[None, :]
    cos_a = jnp.concatenate([jnp.cos(ar), jnp.cos(ar), jnp.cos(ac), jnp.cos(ac)], axis=-1)
    sin_a = jnp.concatenate([-jnp.sin(ar), jnp.sin(ar), -jnp.sin(ac), jnp.sin(ac)], axis=-1)
    cos_b = jnp.concatenate([jnp.cos(ab), jnp.cos(ab)], axis=-1)
    sin_b = jnp.concatenate([-jnp.sin(ab), jnp.sin(ab)], axis=-1)
    two = lambda t: jnp.tile(t, (1, LANES // HEAD_DIM))
    return two(cos_a), two(sin_a), two(cos_b), two(sin_b)


def _layer(x, ffn1_pre_g, ffn1_post_g, ffn1_w_gate, ffn1_w_up, ffn1_w_down,
           mix_pre_g, mix_post_g, w_qkv, q_norm_g, k_norm_g, w_out,
           ffn2_pre_g, ffn2_post_g, ffn2_w_gate, ffn2_w_up, ffn2_w_down, tables, seg, expand):
    B, S, D = x.shape
    vec = lambda g: g.reshape(1, -1).astype(F32)
    bf = lambda w: w.astype(BF16)

    x1 = _ffn_call(x.reshape(B * S, D), vec(ffn1_pre_g), vec(ffn1_post_g),
                   bf(ffn1_w_gate), bf(ffn1_w_up), bf(ffn1_w_down))
    heads_per_tile = 2 * LANES // HEAD_DIM
    kv_gain = jnp.concatenate([jnp.tile(vec(k_norm_g), (1, N_KV_A)), jnp.ones((1, A_KV), F32)], axis=1)
    qa, ka, vta, qkv_b = _qkv_call(x1.reshape(B, S, D), vec(mix_pre_g), bf(w_qkv), seg,
                                   jnp.tile(vec(q_norm_g), (1, heads_per_tile)), kv_gain, *tables)
    score_bound = ((HEAD_DIM * SCALE * LOG2E) * jnp.max(jnp.abs(q_norm_g))
                   * jnp.max(jnp.abs(k_norm_g)))
    heads_a = lax.cond(score_bound <= MAX_SCORE_BOUND,
                       lambda: _attn_a_bounded_call(qa, ka, vta),
                       lambda: _attn_a_call(qa, ka, vta))
    parts = [_attn_b_call(*qkv) for qkv in qkv_b]
    y = _out_ffn_call(x1, heads_a.reshape(B * S, A_Q), parts, expand,
                      bf(w_out), vec(mix_post_g), vec(ffn2_pre_g), vec(ffn2_post_g),
                      bf(ffn2_w_gate), bf(ffn2_w_up), bf(ffn2_w_down))
    return y.reshape(B, S, D)


def kernel(x, ffn1_pre_g, ffn1_post_g, ffn1_w_gate, ffn1_w_up, ffn1_w_down, mix_pre_g, mix_post_g, w_qkv, q_norm_g, k_norm_g, w_out, ffn2_pre_g, ffn2_post_g, ffn2_w_gate, ffn2_w_up, ffn2_w_down):
    assert all(w // 2 // d == SPAN_B for w, d in DILATED_CONFIGS) and TK_A % TM == 0
    S = x.shape[1]
    tables = _rope_tables(S)
    head_of_lane = jnp.arange(2 * LANES) // HEAD_DIM
    seg = (head_of_lane[:, None] == head_of_lane[None, :]).astype(BF16) / HEAD_DIM
    expand = (jnp.arange(LANES)[:, None] == jnp.arange(B_QKV)[None, :] // HEAD_DIM).astype(BF16)
    params = (ffn1_pre_g, ffn1_post_g, ffn1_w_gate, ffn1_w_up, ffn1_w_down, mix_pre_g, mix_post_g,
              w_qkv, q_norm_g, k_norm_g, w_out, ffn2_pre_g, ffn2_post_g, ffn2_w_gate, ffn2_w_up,
              ffn2_w_down)
    for l in range(ffn1_pre_g.shape[0]):
        x = _layer(x, *(p[l] for p in params), tables, seg, expand)
    return x
```

```python
import functools

import jax
import jax.numpy as jnp
from jax import lax
from jax.experimental import pallas as pl
from jax.experimental.pallas import tpu as pltpu

D_MODEL = 1024
HEAD_DIM = 64
N_HEADS_A = 8
N_KV_A = 2
N_HEADS_B = 8
DILATED_CONFIGS = ((128, 1), (512, 4), (2048, 16))
DILATIONS = tuple(d for _, d in DILATED_CONFIGS)
GRID_W = 64
ROPE_THETA = 10000.0
D_FF = 2816
EPS = 1e-6
NEG_INF = -1e30

A_Q = N_HEADS_A * HEAD_DIM
A_KV = N_KV_A * HEAD_DIM
B_QKV = N_HEADS_B * HEAD_DIM
QKV_COLS = A_Q + 2 * A_KV + 3 * B_QKV
SCALE = HEAD_DIM ** -0.5
LOG2E = 1.4426950408889634

LANES = 128
FF_CHUNKS = ((0, 1024), (1024, 2048), (2048, 2816))
TM = 512
TQ_A = 256
TK_A = 1024
TQ_BOUNDED = 512
QH_A = 256
SUB_A = 256
MAX_SCORE_BOUND = 48.0
VT_ROWS = HEAD_DIM + 16
TQ_B = 1024
SUB_B = 128
SPAN_B = 64
VMEM_LIMIT = 52 * 1024 * 1024

BF16 = jnp.bfloat16
F32 = jnp.float32


def _dot(a, b):
    return jnp.dot(a, b, preferred_element_type=F32)


def _dot_nt(a, b):
    return lax.dot_general(a, b, (((1,), (1,)), ((), ())), preferred_element_type=F32)


def _rms(x, g):
    ms = jnp.mean(x * x, axis=-1, keepdims=True)
    return x * lax.rsqrt(ms + EPS) * g


def _swiglu_half_step(x, pre_g, post_g, wg_ref, wu_ref, wd_ref):
    h = _rms(x, pre_g).astype(BF16)
    f = None
    for lo, hi in FF_CHUNKS:
        g = _dot(h, wg_ref[:, lo:hi])
        u = _dot(h, wu_ref[:, lo:hi])
        a = (g / (1.0 + jnp.exp(-g)) * u).astype(BF16)
        part = _dot(a, wd_ref[lo:hi, :])
        f = part if f is None else f + part
    return x + 0.5 * _rms(f, post_g)


def _ffn_kernel(x_ref, pre_ref, post_ref, wg_ref, wu_ref, wd_ref, o_ref):
    o_ref[...] = _swiglu_half_step(x_ref[...], pre_ref[...], post_ref[...],
                                   wg_ref, wu_ref, wd_ref)


def _resident(shape):
    nd = len(shape)
    return pl.BlockSpec(shape, lambda *_: (0,) * nd, pipeline_mode=pl.Buffered(1))


def _ffn_call(x2d, pre_g, post_g, wg, wu, wd):
    n = x2d.shape[0]
    row = pl.BlockSpec((TM, D_MODEL), lambda i: (i, 0))
    return pl.pallas_call(
        _ffn_kernel,
        grid=(n // TM,),
        in_specs=[row, _resident((1, D_MODEL)), _resident((1, D_MODEL)),
                  _resident(wg.shape), _resident(wu.shape), _resident(wd.shape)],
        out_specs=row,
        out_shape=jax.ShapeDtypeStruct(x2d.shape, F32),
        compiler_params=pltpu.CompilerParams(
            dimension_semantics=("arbitrary",), vmem_limit_bytes=VMEM_LIMIT),
        name="ffn1",
    )(x2d, pre_g, post_g, wg, wu, wd)


def _rope(c, cos, sin_signed, half):
    lane = lax.broadcasted_iota(jnp.int32, c.shape, 1)
    first = (lane % (2 * half)) < half
    partner = jnp.where(first, pltpu.roll(c, LANES - half, 1), pltpu.roll(c, half, 1))
    return c * cos + partner * sin_signed


def _head_rms(c, seg_ref, g):
    sq = c * c
    hi = sq.astype(BF16)
    lo = (sq - hi.astype(F32)).astype(BF16)
    ms = _dot(hi, seg_ref[...]) + _dot(lo, seg_ref[...])
    return c * lax.rsqrt(ms + EPS) * g


def _dup_halves(c):
    lane = lax.broadcasted_iota(jnp.int32, c.shape, 1)
    low = lane < HEAD_DIM
    r = pltpu.roll(c, HEAD_DIM, 1)
    return jnp.where(low, c, r), jnp.where(low, r, c)


def _qkv_kernel(x_ref, pre_ref, w_ref, seg_ref, qg_ref, kg_ref,
                cosa_ref, sina_ref, cosb_ref, sinb_ref,
                qa_ref, ka_ref, va_ref, *rest):
    b_refs, (qf_sc, kf_sc, vf_sc) = rest[:-3], rest[-3:]
    h = _rms(x_ref[...], pre_ref[...]).astype(BF16)
    cosa, sina = cosa_ref[...], sina_ref[...]
    cosb, sinb = cosb_ref[...], sinb_ref[...]
    wide = 2 * LANES

    qa = _dot(h, w_ref[:, 0:A_Q])
    for t in range(A_Q // wide):
        qn = _head_rms(qa[:, t * wide:(t + 1) * wide], seg_ref, qg_ref[...])
        for c in range(wide // LANES):
            q = _rope(qn[:, c * LANES:(c + 1) * LANES], cosa, sina, HEAD_DIM // 4) * (SCALE * LOG2E)
            col = t * wide + c * LANES
            qa_ref[:, col:col + LANES] = q.astype(BF16)

    kv = _dot(h, w_ref[:, A_Q:A_Q + 2 * A_KV])
    k = _rope(_head_rms(kv, seg_ref, kg_ref[...])[:, 0:A_KV], cosa, sina, HEAD_DIM // 4)
    k0, k1 = _dup_halves(k)
    ka_ref[0] = k0.astype(BF16)
    ka_ref[1] = k1.astype(BF16)
    vt = kv[:, A_KV:].T.astype(BF16)
    ones = jnp.ones((VT_ROWS - HEAD_DIM, TM), BF16)
    for g in range(N_KV_A):
        va_ref[g, 0:HEAD_DIM, :] = vt[g * HEAD_DIM:(g + 1) * HEAD_DIM, :]
        va_ref[g, HEAD_DIM:, :] = ones

    base = A_Q + 2 * A_KV
    qb = _dot(h, w_ref[:, base:base + B_QKV])
    kb = _dot(h, w_ref[:, base + B_QKV:base + 2 * B_QKV])
    vb = _dot(h, w_ref[:, base + 2 * B_QKV:base + 3 * B_QKV])
    for c in range(B_QKV // LANES):
        sl = slice(c * LANES, (c + 1) * LANES)
        qf_sc[c] = _rope(qb[:, sl], cosb, sinb, HEAD_DIM // 2) * (SCALE * LOG2E)
        kf_sc[c] = _rope(kb[:, sl], cosb, sinb, HEAD_DIM // 2)
        vf_sc[c] = vb[:, sl]
    for j, src in enumerate((qf_sc, kf_sc, vf_sc)):
        for i, d in enumerate(DILATIONS):
            out = b_refs[3 * i + j]
            for r in range(d):
                rows = slice(None) if d == 1 else pl.ds(r, TM // d, stride=d)
                for c in range(B_QKV // LANES):
                    out[r, :, c * LANES:(c + 1) * LANES] = src[c, rows, :].astype(BF16)


def _qkv_call(x1, pre_g, w, seg, qg, kg, cosa, sina, cosb, sinb):
    B, S, _ = x1.shape
    tab = pl.BlockSpec((TM, LANES), lambda i, b: (i, 0))
    wide = pl.BlockSpec((None, TM, A_Q), lambda i, b: (b, i, 0))
    dup = pl.BlockSpec((None, N_KV_A, TM, LANES), lambda i, b: (b, 0, i, 0))
    per = TK_A // TM
    vt = pl.BlockSpec((None, N_KV_A, None, VT_ROWS, TM), lambda i, b: (b, 0, i // per, 0, i % per))
    wide_shape = jax.ShapeDtypeStruct((B, S, A_Q), BF16)
    dup_shape = jax.ShapeDtypeStruct((B, N_KV_A, S, LANES), BF16)
    vt_shape = jax.ShapeDtypeStruct((B, N_KV_A, S // TK_A, VT_ROWS, TK_A), BF16)
    b_specs, b_shapes = [], []
    for d in DILATIONS:
        b_specs += [pl.BlockSpec((None, d, TM // d, B_QKV), lambda i, b: (b, 0, i, 0))] * 3
        b_shapes += [jax.ShapeDtypeStruct((B, d, S // d, B_QKV), BF16)] * 3
    outs = pl.pallas_call(
        _qkv_kernel,
        grid=(S // TM, B),
        in_specs=[pl.BlockSpec((None, TM, D_MODEL), lambda i, b: (b, i, 0)),
                  _resident((1, D_MODEL)), _resident(w.shape), _resident(seg.shape),
                  _resident(qg.shape), _resident(kg.shape), tab, tab, tab, tab],
        out_specs=[wide, dup, vt] + b_specs,
        out_shape=[wide_shape, dup_shape, vt_shape] + b_shapes,
        scratch_shapes=[pltpu.VMEM((B_QKV // LANES, TM, LANES), F32)] * 3,
        compiler_params=pltpu.CompilerParams(
            dimension_semantics=("arbitrary", "arbitrary"), vmem_limit_bytes=VMEM_LIMIT),
        name="qkv",
    )(x1, pre_g, w, seg, qg, kg, cosa, sina, cosb, sinb)
    qa, ka, vta = outs[:3]
    qkv_b = [tuple(outs[3 + 3 * i:6 + 3 * i]) for i in range(len(DILATIONS))]
    return qa, ka, vta, qkv_b


def _attn_a_kernel(q_ref, k_ref, vt_ref, o_ref, qs_sc, m_sc, acc_sc, sa_sc, sb_sc, cma_sc, cmb_sc):
    tq = q_ref.shape[0]
    n_pairs = qs_sc.shape[0]
    low = lax.broadcasted_iota(jnp.int32, (tq, LANES), 1) < HEAD_DIM
    for c in range(n_pairs):
        qc = q_ref[:, c * LANES:(c + 1) * LANES]
        zero = jnp.zeros_like(qc)
        qs_sc[c, 0:tq, :] = jnp.where(low, qc, zero)
        qs_sc[c, tq:, :] = jnp.where(low, zero, qc)

    m_sc[...] = jnp.full(m_sc.shape, NEG_INF, F32)
    acc_sc[...] = jnp.zeros(acc_sc.shape, F32)

    n_k = vt_ref.shape[0]

    def scores(kb, s_out, cm_out):
        k = k_ref[pl.ds(pl.multiple_of(kb * TK_A, TK_A), TK_A), :]
        for c in range(n_pairs):
            s = _dot_nt(k, qs_sc[c])
            s_out[c] = s
            cm_out[c] = jnp.max(s, axis=0, keepdims=True)

    def consume(kb, s_in, cm_in):
        vt = vt_ref[kb]
        for c in range(n_pairs):
            m_old = m_sc[c]
            m_new = jnp.maximum(m_old, cm_in[c])
            alpha = jnp.exp2(m_old - m_new)
            p = jnp.exp2(s_in[c] - m_new).astype(BF16)
            acc_sc[c] = alpha * acc_sc[c] + _dot(vt, p)
            m_sc[c] = m_new

    bufs = ((sa_sc, cma_sc), (sb_sc, cmb_sc))
    scores(0, *bufs[0])

    def fused(kb, cur, nxt):
        s_in, cm_in = cur
        s_out, cm_out = nxt
        k0 = pl.multiple_of((kb + 1) * TK_A, TK_A)
        m_new, alpha, pv, cm = [], [], [], []
        for c in range(n_pairs):
            m_old = m_sc[c]
            m_new.append(jnp.maximum(m_old, cm_in[c]))
            alpha.append(jnp.exp2(m_old - m_new[c]))
            pv.append(None)
            cm.append(None)
        for j in range(TK_A // SUB_A):
            rows = slice(j * SUB_A, (j + 1) * SUB_A)
            k = k_ref[pl.ds(k0 + j * SUB_A, SUB_A), :]
            vt = vt_ref[kb, :, rows]
            for c in range(n_pairs):
                s = _dot_nt(k, qs_sc[c])
                s_out[c, rows, :] = s
                part = jnp.max(s, axis=0, keepdims=True)
                cm[c] = part if cm[c] is None else jnp.maximum(cm[c], part)
            for c in range(n_pairs):
                p = jnp.exp2(s_in[c, rows, :] - m_new[c]).astype(BF16)
                part = _dot(vt, p)
                pv[c] = part if pv[c] is None else pv[c] + part
        for c in range(n_pairs):
            cm_out[c] = cm[c]
            acc_sc[c] = alpha[c] * acc_sc[c] + pv[c]
            m_sc[c] = m_new[c]

    def body(kb, carry):
        for parity in range(2):
            @pl.when(kb % 2 == parity)
            def _():
                fused(kb, bufs[parity], bufs[1 - parity])
        return carry

    lax.fori_loop(0, n_k - 1, body, 0)
    consume(n_k - 1, *bufs[(n_k - 1) % 2])
    for c in range(n_pairs):
        o = acc_sc[c, 0:HEAD_DIM, :] / acc_sc[c, HEAD_DIM:HEAD_DIM + 1, :]
        pair = jnp.concatenate([o[:, 0:tq], o[:, tq:]], axis=0)
        o_ref[:, c * LANES:(c + 1) * LANES] = pair.T.astype(o_ref.dtype)


def _attn_a_bounded_kernel(q_ref, k_ref, vt_ref, o_ref, qt_sc, acc_sc):
    n_pairs = q_ref.shape[1] // LANES
    units = [(c, h) for h in range(q_ref.shape[0] // QH_A) for c in range(n_pairs)]
    low = lax.broadcasted_iota(jnp.int32, (QH_A, LANES), 1) < HEAD_DIM
    for u, (c, h) in enumerate(units):
        qc = q_ref[h * QH_A:(h + 1) * QH_A, c * LANES:(c + 1) * LANES]
        zero = jnp.zeros_like(qc)
        qs = jnp.concatenate([jnp.where(low, qc, zero), jnp.where(low, zero, qc)], axis=0)
        qt_sc[u] = qs.T
    n_sub = TK_A // SUB_A
    stream = [(t, u) for t in range(vt_ref.shape[0] * n_sub) for u in range(len(units))]

    def scores(i):
        t, u = stream[i]
        return _dot(k_ref[t * SUB_A:(t + 1) * SUB_A, :], qt_sc[u])

    depth = n_pairs
    ahead = [scores(i) for i in range(depth)]
    pv = [None] * len(units)
    den = [None] * len(units)
    for i, (t, u) in enumerate(stream):
        if i + depth < len(stream):
            ahead.append(scores(i + depth))
        cols = slice((t % n_sub) * SUB_A, (t % n_sub + 1) * SUB_A)
        p = jnp.exp2(ahead.pop(0))
        part = _dot(vt_ref[t // n_sub, 0:HEAD_DIM, cols], p.astype(BF16))
        pv[u] = part if pv[u] is None else pv[u] + part
        rows8 = jnp.sum(p.reshape(SUB_A // 8, 8, p.shape[1]), axis=0)
        den[u] = rows8 if den[u] is None else den[u] + rows8
    for u in range(len(units)):
        acc_sc[u, 0:HEAD_DIM, :] = pv[u]
        acc_sc[u, HEAD_DIM:HEAD_DIM + 8, :] = den[u]
    for u, (c, h) in enumerate(units):
        l = jnp.sum(acc_sc[u, HEAD_DIM:HEAD_DIM + 8, :], axis=0, keepdims=True)
        o = acc_sc[u, 0:HEAD_DIM, :] / l
        pair = jnp.concatenate([o[:, 0:QH_A], o[:, QH_A:]], axis=0)
        o_ref[h * QH_A:(h + 1) * QH_A, c * LANES:(c + 1) * LANES] = pair.T.astype(o_ref.dtype)


def _attn_a_bounded_call(qa, ka, vta):
    B, S, _ = qa.shape
    qcols = A_Q // N_KV_A
    n_units = (qcols // LANES) * (TQ_BOUNDED // QH_A)
    qspec = pl.BlockSpec((None, TQ_BOUNDED, qcols), lambda b, g, i: (b, i, g))
    kspec = pl.BlockSpec((None, None, S, LANES), lambda b, g, i: (b, g, 0, 0))
    vspec = pl.BlockSpec((None, None, S // TK_A, VT_ROWS, TK_A), lambda b, g, i: (b, g, 0, 0, 0))
    return pl.pallas_call(
        _attn_a_bounded_kernel,
        grid=(B, N_KV_A, S // TQ_BOUNDED),
        in_specs=[qspec, kspec, vspec],
        out_specs=qspec,
        out_shape=jax.ShapeDtypeStruct((B, S, A_Q), BF16),
        scratch_shapes=[pltpu.VMEM((n_units, LANES, 2 * QH_A), BF16),
                        pltpu.VMEM((n_units, VT_ROWS, 2 * QH_A), F32)],
        compiler_params=pltpu.CompilerParams(
            dimension_semantics=("arbitrary", "arbitrary", "arbitrary"),
            vmem_limit_bytes=VMEM_LIMIT),
        name="attn_a_bounded",
    )(qa, ka, vta)


def _attn_a_call(qa, ka, vta):
    B, S, _ = qa.shape
    qcols = A_Q // N_KV_A
    pairs = N_HEADS_A // N_KV_A // 2
    qspec = pl.BlockSpec((None, TQ_A, qcols), lambda b, g, i: (b, i, g))
    kspec = pl.BlockSpec((None, None, S, LANES), lambda b, g, i: (b, g, 0, 0))
    vspec = pl.BlockSpec((None, None, S // TK_A, VT_ROWS, TK_A), lambda b, g, i: (b, g, 0, 0, 0))
    return pl.pallas_call(
        _attn_a_kernel,
        grid=(B, N_KV_A, S // TQ_A),
        in_specs=[qspec, kspec, vspec],
        out_specs=qspec,
        out_shape=jax.ShapeDtypeStruct((B, S, A_Q), BF16),
        scratch_shapes=[pltpu.VMEM((pairs, 2 * TQ_A, LANES), BF16),
                        pltpu.VMEM((pairs, 1, 2 * TQ_A), F32),
                        pltpu.VMEM((pairs, VT_ROWS, 2 * TQ_A), F32),
                        pltpu.VMEM((pairs, TK_A, 2 * TQ_A), F32),
                        pltpu.VMEM((pairs, TK_A, 2 * TQ_A), F32),
                        pltpu.VMEM((pairs, 1, 2 * TQ_A), F32),
                        pltpu.VMEM((pairs, 1, 2 * TQ_A), F32)],
        compiler_params=pltpu.CompilerParams(
            dimension_semantics=("arbitrary", "arbitrary", "arbitrary"),
            vmem_limit_bytes=VMEM_LIMIT),
        name="attn_a",
    )(qa, ka, vta)


def _attn_b_kernel(q_ref, kp_ref, kc_ref, kn_ref, vp_ref, vc_ref, vn_ref,
                   o_ref, st_ref, kbuf, vbuf, *, seq_len):
    j = pl.program_id(2)
    tq = q_ref.shape[0]
    kbuf[0:SPAN_B, :] = kp_ref[...]
    kbuf[SPAN_B:SPAN_B + tq, :] = kc_ref[...]
    kbuf[SPAN_B + tq:, :] = kn_ref[...]
    vbuf[0:SPAN_B, :] = vp_ref[...]
    vbuf[SPAN_B:SPAN_B + tq, :] = vc_ref[...]
    vbuf[SPAN_B + tq:, :] = vn_ref[...]

    n_keys = SUB_B + 2 * SPAN_B
    c_idx = lax.broadcasted_iota(jnp.int32, (n_keys, 2 * SUB_B), 0)
    a_idx = lax.broadcasted_iota(jnp.int32, (n_keys, 2 * SUB_B), 1) % SUB_B
    band_bias = jnp.where(jnp.abs(c_idx - SPAN_B - a_idx) <= SPAN_B, 0.0, NEG_INF).astype(F32)
    low = lax.broadcasted_iota(jnp.int32, (SUB_B, LANES), 1) < HEAD_DIM
    ones = jnp.ones((VT_ROWS - HEAD_DIM, n_keys), BF16)
    pad = jnp.zeros((LANES - 2 * N_HEADS_B, SUB_B), F32)

    n_sub = tq // SUB_B
    n_pairs = N_HEADS_B // 2
    biases = {}

    def bias_of(i):
        if i not in biases:
            first_key = j * tq + i * SUB_B - SPAN_B
            bias = band_bias
            if i == 0:
                bias = jnp.where(c_idx >= -first_key, bias, NEG_INF)
            if i == n_sub - 1:
                bias = jnp.where(c_idx < seq_len - first_key, bias, NEG_INF)
            biases[i] = bias
        return biases[i]

    def scores(i, hp):
        q0, cols = i * SUB_B, slice(hp * LANES, (hp + 1) * LANES)
        qp = q_ref[q0:q0 + SUB_B, cols]
        zero = jnp.zeros_like(qp)
        qs = jnp.concatenate([jnp.where(low, qp, zero), jnp.where(low, zero, qp)], axis=0)
        return _dot(kbuf[q0:q0 + n_keys, cols], qs.T) + bias_of(i)

    def finish(i, hp, s):
        q0, cols = i * SUB_B, slice(hp * LANES, (hp + 1) * LANES)
        m = jnp.max(s, axis=0, keepdims=True)
        p = jnp.exp2(s - m).astype(BF16)
        vt = jnp.concatenate([vbuf[q0:q0 + n_keys, cols].T, ones], axis=0)
        o_all = _dot(vt, p)
        o_t = jnp.concatenate([o_all[0:HEAD_DIM, 0:SUB_B],
                               o_all[HEAD_DIM:LANES, SUB_B:]], axis=0)
        o_ref[q0:q0 + SUB_B, cols] = o_t.T.astype(o_ref.dtype)
        l = o_all[LANES:LANES + 1, :]
        return [m[:, 0:SUB_B], m[:, SUB_B:]], [l[:, 0:SUB_B], l[:, SUB_B:]]

    stream = [(i, hp) for i in range(n_sub) for hp in range(n_pairs)]
    depth = 6
    ahead = [scores(*stream[u]) for u in range(depth)]
    ms, ls = [], []
    for u, (i, hp) in enumerate(stream):
        if u + depth < len(stream):
            ahead.append(scores(*stream[u + depth]))
        m2, l2 = finish(i, hp, ahead.pop(0))
        ms += m2
        ls += l2
        if hp == n_pairs - 1:
            st_t = jnp.concatenate(ms + ls + [pad], axis=0)
            st_ref[i * SUB_B:(i + 1) * SUB_B, :] = st_t.T
            ms, ls = [], []


def _attn_b_call(qb, kb, vb):
    B, d, L, C = qb.shape
    nh = L // SPAN_B
    tq = min(TQ_B, L)
    per = tq // SPAN_B
    cur = pl.BlockSpec((None, None, tq, C), lambda b, r, j: (b, r, j, 0))
    prev = pl.BlockSpec((None, None, SPAN_B, C),
                        lambda b, r, j: (b, r, jnp.maximum(j * per - 1, 0), 0))
    nxt = pl.BlockSpec((None, None, SPAN_B, C),
                       lambda b, r, j: (b, r, jnp.minimum((j + 1) * per, nh - 1), 0))
    st_spec = pl.BlockSpec((None, None, tq, LANES), lambda b, r, j: (b, r, j, 0))
    return pl.pallas_call(
        functools.partial(_attn_b_kernel, seq_len=L),
        grid=(B, d, L // tq),
        in_specs=[cur, prev, cur, nxt, prev, cur, nxt],
        out_specs=[cur, st_spec],
        out_shape=[jax.ShapeDtypeStruct((B, d, L, C), BF16),
                   jax.ShapeDtypeStruct((B, d, L, LANES), F32)],
        scratch_shapes=[pltpu.VMEM((tq + 2 * SPAN_B, C), BF16),
                        pltpu.VMEM((tq + 2 * SPAN_B, C), BF16)],
        compiler_params=pltpu.CompilerParams(
            dimension_semantics=("arbitrary", "arbitrary", "arbitrary"),
            vmem_limit_bytes=VMEM_LIMIT),
        name=f"attn_b_d{d}",
    )(qb, kb, kb, kb, vb, vb, vb)


def _natural_order(ref, scratch):
    d, _, cols = ref.shape
    if d == 1:
        return ref[0].astype(F32)
    for r in range(d):
        for c in range(cols // LANES):
            scratch[c, pl.ds(r, TM // d, stride=d), :] = ref[r, :, c * LANES:(c + 1) * LANES].astype(F32)
    return jnp.concatenate([scratch[c] for c in range(cols // LANES)], axis=1)


def _out_ffn_kernel(x_ref, ha_ref, o1_ref, o2_ref, o3_ref, s1_ref, s2_ref, s3_ref, ex_ref,
                    wo_ref, mixg_ref, pre_ref, post_ref, wg_ref, wu_ref, wd_ref, y_ref,
                    of2_sc, of3_sc, sf2_sc, sf3_sc):
    stats = (_natural_order(s1_ref, None), _natural_order(s2_ref, sf2_sc),
             _natural_order(s3_ref, sf3_sc))
    parts = (_natural_order(o1_ref, None), _natural_order(o2_ref, of2_sc),
             _natural_order(o3_ref, of3_sc))
    is_max = lax.broadcasted_iota(jnp.int32, (TM, LANES), 1) < N_HEADS_B
    m_all = jnp.maximum(jnp.maximum(stats[0], stats[1]), stats[2])
    es = [jnp.exp2(s - m_all) for s in stats]
    den = None
    for e, s in zip(es, stats):
        term = e * pltpu.roll(s, LANES - N_HEADS_B, 1)
        den = term if den is None else den + term
    heads_b = None
    for e, o in zip(es, parts):
        w = jnp.where(is_max, e / den, 0.0)
        hi = w.astype(BF16)
        lo = (w - hi.astype(F32)).astype(BF16)
        term = (_dot(hi, ex_ref[...]) + _dot(lo, ex_ref[...])) * o
        heads_b = term if heads_b is None else heads_b + term

    mixed = _dot(ha_ref[...], wo_ref[0:A_Q, :]) + _dot(heads_b.astype(BF16), wo_ref[A_Q:, :])
    x2 = x_ref[...] + _rms(mixed, mixg_ref[...])
    y_ref[...] = _swiglu_half_step(x2, pre_ref[...], post_ref[...], wg_ref, wu_ref, wd_ref)


def _out_ffn_call(x1, heads_a, parts, expand, wo, mix_g, pre_g, post_g, wg, wu, wd):
    n = x1.shape[0]
    per_batch = parts[0][0].shape[2] // TM
    row = lambda c: pl.BlockSpec((TM, c), lambda i: (i, 0))

    def strided(a):
        _, d, _, c = a.shape
        return pl.BlockSpec((None, d, TM // d, c), lambda i: (i // per_batch, 0, i % per_batch, 0))

    (o1, s1), (o2, s2), (o3, s3) = parts
    return pl.pallas_call(
        _out_ffn_kernel,
        grid=(n // TM,),
        in_specs=[row(D_MODEL), row(A_Q), strided(o1), strided(o2), strided(o3),
                  strided(s1), strided(s2), strided(s3), _resident(expand.shape),
                  _resident(wo.shape), _resident((1, D_MODEL)), _resident((1, D_MODEL)),
                  _resident((1, D_MODEL)), _resident(wg.shape), _resident(wu.shape),
                  _resident(wd.shape)],
        out_specs=row(D_MODEL),
        out_shape=jax.ShapeDtypeStruct((n, D_MODEL), F32),
        scratch_shapes=[pltpu.VMEM((B_QKV // LANES, TM, LANES), F32),
                        pltpu.VMEM((B_QKV // LANES, TM, LANES), F32),
                        pltpu.VMEM((1, TM, LANES), F32), pltpu.VMEM((1, TM, LANES), F32)],
        compiler_params=pltpu.CompilerParams(
            dimension_semantics=("arbitrary",), vmem_limit_bytes=VMEM_LIMIT),
        name="out_ffn2",
    )(x1, heads_a, o1, o2, o3, s1, s2, s3, expand, wo, mix_g, pre_g, post_g, wg, wu, wd)


def _rope_tables(seq):
    pos = jnp.arange(seq, dtype=jnp.int32)
    row = (pos // GRID_W).astype(F32)[:, None]
    col = (pos % GRID_W).astype(F32)[:, None]
    dim_a = HEAD_DIM // 2
    fa = ROPE_THETA ** (-jnp.arange(0, dim_a, 2, dtype=F32) / dim_a)
    fb = ROPE_THETA ** (-jnp.arange(0, HEAD_DIM, 2, dtype=F32) / HEAD_DIM)
    ar, ac = row * fa[None, :], col * fa[None, :]
    ab = pos.astype(F32)[:, None] * fb[None, :]
    cos_a = jnp.concatenate([jnp.cos(ar), jnp.cos(ar), jnp.cos(ac), jnp.cos(ac)], axis=-1)
    sin_a = jnp.concatenate([-jnp.sin(ar), jnp.sin(ar), -jnp.sin(ac), jnp.sin(ac)], axis=-1)
    cos_b = jnp.concatenate([jnp.cos(ab), jnp.cos(ab)], axis=-1)
    sin_b = jnp.concatenate([-jnp.sin(ab), jnp.sin(ab)], axis=-1)
    two = lambda t: jnp.tile(t, (1, LANES // HEAD_DIM))
    return two(cos_a), two(sin_a), two(cos_b), two(sin_b)


def _layer(x, ffn1_pre_g, ffn1_post_g, ffn1_w_gate, ffn1_w_up, ffn1_w_down,
           mix_pre_g, mix_post_g, w_qkv, q_norm_g, k_norm_g, w_out,
           ffn2_pre_g, ffn2_post_g, ffn2_w_gate, ffn2_w_up, ffn2_w_down, tables, seg, expand):
    B, S, D = x.shape
    vec = lambda g: g.reshape(1, -1).astype(F32)
    bf = lambda w: w.astype(BF16)

    x1 = _ffn_call(x.reshape(B * S, D), vec(ffn1_pre_g), vec(ffn1_post_g),
                   bf(ffn1_w_gate), bf(ffn1_w_up), bf(ffn1_w_down))
    heads_per_tile = 2 * LANES // HEAD_DIM
    kv_gain = jnp.concatenate([jnp.tile(vec(k_norm_g), (1, N_KV_A)), jnp.ones((1, A_KV), F32)], axis=1)
    qa, ka, vta, qkv_b = _qkv_call(x1.reshape(B, S, D), vec(mix_pre_g), bf(w_qkv), seg,
                                   jnp.tile(vec(q_norm_g), (1, heads_per_tile)), kv_gain, *tables)
    score_bound = ((HEAD_DIM * SCALE * LOG2E) * jnp.max(jnp.abs(q_norm_g))
                   * jnp.max(jnp.abs(k_norm_g)))
    heads_a = lax.cond(score_bound <= MAX_SCORE_BOUND,
                       lambda: _attn_a_bounded_call(qa, ka, vta),
                       lambda: _attn_a_call(qa, ka, vta))
    parts = [_attn_b_call(*qkv) for qkv in qkv_b]
    y = _out_ffn_call(x1, heads_a.reshape(B * S, A_Q), parts, expand,
                      bf(w_out), vec(mix_post_g), vec(ffn2_pre_g), vec(ffn2_post_g),
                      bf(ffn2_w_gate), bf(ffn2_w_up), bf(ffn2_w_down))
    return y.reshape(B, S, D)


def kernel(x, ffn1_pre_g, ffn1_post_g, ffn1_w_gate, ffn1_w_up, ffn1_w_down, mix_pre_g, mix_post_g, w_qkv, q_norm_g, k_norm_g, w_out, ffn2_pre_g, ffn2_post_g, ffn2_w_gate, ffn2_w_up, ffn2_w_down):
    assert all(w // 2 // d == SPAN_B for w, d in DILATED_CONFIGS) and TK_A % TM == 0
    S = x.shape[1]
    tables = _rope_tables(S)
    head_of_lane = jnp.arange(2 * LANES) // HEAD_DIM
    seg = (head_of_lane[:, None] == head_of_lane[None, :]).astype(BF16) / HEAD_DIM
    expand = (jnp.arange(LANES)[:, None] == jnp.arange(B_QKV)[None, :] // HEAD_DIM).astype(BF16)
    params = (ffn1_pre_g, ffn1_post_g, ffn1_w_gate, ffn1_w_up, ffn1_w_down, mix_pre_g, mix_post_g,
              w_qkv, q_norm_g, k_norm_g, w_out, ffn2_pre_g, ffn2_post_g, ffn2_w_gate, ffn2_w_up,
              ffn2_w_down)
    for l in range(ffn1_pre_g.shape[0]):
        x = _layer(x, *(p[l] for p in params), tables, seg, expand)
    return x
```

```python
import functools

import jax
import jax.numpy as jnp
from jax import lax
from jax.experimental import pallas as pl
from jax.experimental.pallas import tpu as pltpu

D_MODEL = 1024
HEAD_DIM = 64
N_HEADS_A = 8
N_KV_A = 2
N_HEADS_B = 8
DILATED_CONFIGS = ((128, 1), (512, 4), (2048, 16))
DILATIONS = tuple(d for _, d in DILATED_CONFIGS)
GRID_W = 64
ROPE_THETA = 10000.0
D_FF = 2816
EPS = 1e-6
NEG_INF = -1e30

A_Q = N_HEADS_A * HEAD_DIM
A_KV = N_KV_A * HEAD_DIM
B_QKV = N_HEADS_B * HEAD_DIM
QKV_COLS = A_Q + 2 * A_KV + 3 * B_QKV
SCALE = HEAD_DIM ** -0.5
LOG2E = 1.4426950408889634

LANES = 128
FF_CHUNKS = ((0, 1024), (1024, 2048), (2048, 2816))
TM = 512
TQ_A = 256
TK_A = 1024
TQ_BOUNDED = 512
QH_A = 256
SUB_A = 256
MAX_SCORE_BOUND = 48.0
VT_ROWS = HEAD_DIM + 16
TQ_B = 1024
SUB_B = 128
SPAN_B = 64
VMEM_LIMIT = 52 * 1024 * 1024
VMEM_LIMIT_OUT_FFN = 60 * 1024 * 1024

BF16 = jnp.bfloat16
F32 = jnp.float32


def _dot(a, b):
    return jnp.dot(a, b, preferred_element_type=F32)


def _dot_nt(a, b):
    return lax.dot_general(a, b, (((1,), (1,)), ((), ())), preferred_element_type=F32)


def _rms(x, g):
    ms = jnp.mean(x * x, axis=-1, keepdims=True)
    return x * lax.rsqrt(ms + EPS) * g


def _swiglu(h, wg_ref, wu_ref, wd_ref, next_tile):
    f = None
    for n, (lo, hi) in enumerate(FF_CHUNKS):
        if n == len(FF_CHUNKS) - 1:
            h = jnp.where(pl.program_id(0) < 0, ahead, h)
        g = _dot(h, wg_ref[:, lo:hi])
        u = _dot(h, wu_ref[:, lo:hi])
        a = (g / (1.0 + jnp.exp(-g)) * u).astype(BF16)
        part = _dot(a, wd_ref[lo:hi, :])
        f = part if f is None else f + part
        if n == 0:
            ahead = next_tile()
    return f


def _ffn_kernel(x_ref, xn_ref, pre_ref, post_ref, wg_ref, wu_ref, wd_ref, o_ref, h_sc):
    @pl.when(pl.program_id(0) == 0)
    def _():
        h_sc[...] = _rms(x_ref[...], pre_ref[...]).astype(BF16)

    nxt = []

    def next_tile():
        nxt.append(_rms(xn_ref[...], pre_ref[...]).astype(BF16))
        return nxt[0]

    f = _swiglu(h_sc[...], wg_ref, wu_ref, wd_ref, next_tile)
    o_ref[...] = x_ref[...] + 0.5 * _rms(f, post_ref[...])
    h_sc[...] = nxt[0]


def _resident(shape):
    nd = len(shape)
    return pl.BlockSpec(shape, lambda *_: (0,) * nd, pipeline_mode=pl.Buffered(1))


def _ffn_call(x2d, pre_g, post_g, wg, wu, wd):
    n_tiles = x2d.shape[0] // TM
    row = pl.BlockSpec((TM, D_MODEL), lambda i: (i, 0))
    nxt = pl.BlockSpec((TM, D_MODEL), lambda i: (jnp.minimum(i + 1, n_tiles - 1), 0))
    return pl.pallas_call(
        _ffn_kernel,
        grid=(n_tiles,),
        in_specs=[row, nxt, _resident((1, D_MODEL)), _resident((1, D_MODEL)),
                  _resident(wg.shape), _resident(wu.shape), _resident(wd.shape)],
        out_specs=row,
        out_shape=jax.ShapeDtypeStruct(x2d.shape, F32),
        scratch_shapes=[pltpu.VMEM((TM, D_MODEL), BF16)],
        compiler_params=pltpu.CompilerParams(
            dimension_semantics=("arbitrary",), vmem_limit_bytes=VMEM_LIMIT),
        name="ffn1",
    )(x2d, x2d, pre_g, post_g, wg, wu, wd)


def _rope(c, cos, sin_signed, half):
    lane = lax.broadcasted_iota(jnp.int32, c.shape, 1)
    first = (lane % (2 * half)) < half
    partner = jnp.where(first, pltpu.roll(c, LANES - half, 1), pltpu.roll(c, half, 1))
    return c * cos + partner * sin_signed


def _head_rms(c, seg_ref, g):
    sq = c * c
    hi = sq.astype(BF16)
    lo = (sq - hi.astype(F32)).astype(BF16)
    ms = _dot(hi, seg_ref[...]) + _dot(lo, seg_ref[...])
    return c * lax.rsqrt(ms + EPS) * g


def _dup_halves(c):
    lane = lax.broadcasted_iota(jnp.int32, c.shape, 1)
    low = lane < HEAD_DIM
    r = pltpu.roll(c, HEAD_DIM, 1)
    return jnp.where(low, c, r), jnp.where(low, r, c)


def _qkv_kernel(x_ref, pre_ref, w_ref, seg_ref, qg_ref, kg_ref,
                cosa_ref, sina_ref, cosb_ref, sinb_ref,
                qa_ref, ka_ref, va_ref, *rest):
    b_refs, (qf_sc, kf_sc, vf_sc) = rest[:-3], rest[-3:]
    h = _rms(x_ref[...], pre_ref[...]).astype(BF16)
    cosa, sina = cosa_ref[...], sina_ref[...]
    cosb, sinb = cosb_ref[...], sinb_ref[...]
    wide = 2 * LANES

    qa = _dot(h, w_ref[:, 0:A_Q])
    for t in range(A_Q // wide):
        qn = _head_rms(qa[:, t * wide:(t + 1) * wide], seg_ref, qg_ref[...])
        for c in range(wide // LANES):
            q = _rope(qn[:, c * LANES:(c + 1) * LANES], cosa, sina, HEAD_DIM // 4) * (SCALE * LOG2E)
            col = t * wide + c * LANES
            qa_ref[:, col:col + LANES] = q.astype(BF16)

    kv = _dot(h, w_ref[:, A_Q:A_Q + 2 * A_KV])
    k = _rope(_head_rms(kv, seg_ref, kg_ref[...])[:, 0:A_KV], cosa, sina, HEAD_DIM // 4)
    k0, k1 = _dup_halves(k)
    ka_ref[0] = k0.astype(BF16)
    ka_ref[1] = k1.astype(BF16)
    vt = kv[:, A_KV:].T.astype(BF16)
    ones = jnp.ones((VT_ROWS - HEAD_DIM, TM), BF16)
    for g in range(N_KV_A):
        va_ref[g, 0:HEAD_DIM, :] = vt[g * HEAD_DIM:(g + 1) * HEAD_DIM, :]
        va_ref[g, HEAD_DIM:, :] = ones

    base = A_Q + 2 * A_KV
    qb = _dot(h, w_ref[:, base:base + B_QKV])
    kb = _dot(h, w_ref[:, base + B_QKV:base + 2 * B_QKV])
    vb = _dot(h, w_ref[:, base + 2 * B_QKV:base + 3 * B_QKV])
    for c in range(B_QKV // LANES):
        sl = slice(c * LANES, (c + 1) * LANES)
        qf_sc[c] = _rope(qb[:, sl], cosb, sinb, HEAD_DIM // 2) * (SCALE * LOG2E)
        kf_sc[c] = _rope(kb[:, sl], cosb, sinb, HEAD_DIM // 2)
        vf_sc[c] = vb[:, sl]
    for j, src in enumerate((qf_sc, kf_sc, vf_sc)):
        for i, d in enumerate(DILATIONS):
            out = b_refs[3 * i + j]
            for r in range(d):
                rows = slice(None) if d == 1 else pl.ds(r, TM // d, stride=d)
                for c in range(B_QKV // LANES):
                    out[r, :, c * LANES:(c + 1) * LANES] = src[c, rows, :].astype(BF16)


def _qkv_call(x1, pre_g, w, seg, qg, kg, cosa, sina, cosb, sinb):
    B, S, _ = x1.shape
    tab = pl.BlockSpec((TM, LANES), lambda i, b: (i, 0))
    wide = pl.BlockSpec((None, TM, A_Q), lambda i, b: (b, i, 0))
    dup = pl.BlockSpec((None, N_KV_A, TM, LANES), lambda i, b: (b, 0, i, 0))
    per = TK_A // TM
    vt = pl.BlockSpec((None, N_KV_A, None, VT_ROWS, TM), lambda i, b: (b, 0, i // per, 0, i % per))
    wide_shape = jax.ShapeDtypeStruct((B, S, A_Q), BF16)
    dup_shape = jax.ShapeDtypeStruct((B, N_KV_A, S, LANES), BF16)
    vt_shape = jax.ShapeDtypeStruct((B, N_KV_A, S // TK_A, VT_ROWS, TK_A), BF16)
    b_specs, b_shapes = [], []
    for d in DILATIONS:
        b_specs += [pl.BlockSpec((None, d, TM // d, B_QKV), lambda i, b: (b, 0, i, 0))] * 3
        b_shapes += [jax.ShapeDtypeStruct((B, d, S // d, B_QKV), BF16)] * 3
    outs = pl.pallas_call(
        _qkv_kernel,
        grid=(S // TM, B),
        in_specs=[pl.BlockSpec((None, TM, D_MODEL), lambda i, b: (b, i, 0)),
                  _resident((1, D_MODEL)), _resident(w.shape), _resident(seg.shape),
                  _resident(qg.shape), _resident(kg.shape), tab, tab, tab, tab],
        out_specs=[wide, dup, vt] + b_specs,
        out_shape=[wide_shape, dup_shape, vt_shape] + b_shapes,
        scratch_shapes=[pltpu.VMEM((B_QKV // LANES, TM, LANES), F32)] * 3,
        compiler_params=pltpu.CompilerParams(
            dimension_semantics=("arbitrary", "arbitrary"), vmem_limit_bytes=VMEM_LIMIT),
        name="qkv",
    )(x1, pre_g, w, seg, qg, kg, cosa, sina, cosb, sinb)
    qa, ka, vta = outs[:3]
    qkv_b = [tuple(outs[3 + 3 * i:6 + 3 * i]) for i in range(len(DILATIONS))]
    return qa, ka, vta, qkv_b


def _attn_a_kernel(q_ref, k_ref, vt_ref, o_ref, qs_sc, m_sc, acc_sc, sa_sc, sb_sc, cma_sc, cmb_sc):
    tq = q_ref.shape[0]
    n_pairs = qs_sc.shape[0]
    low = lax.broadcasted_iota(jnp.int32, (tq, LANES), 1) < HEAD_DIM
    for c in range(n_pairs):
        qc = q_ref[:, c * LANES:(c + 1) * LANES]
        zero = jnp.zeros_like(qc)
        qs_sc[c, 0:tq, :] = jnp.where(low, qc, zero)
        qs_sc[c, tq:, :] = jnp.where(low, zero, qc)

    m_sc[...] = jnp.full(m_sc.shape, NEG_INF, F32)
    acc_sc[...] = jnp.zeros(acc_sc.shape, F32)

    n_k = vt_ref.shape[0]

    def scores(kb, s_out, cm_out):
        k = k_ref[pl.ds(pl.multiple_of(kb * TK_A, TK_A), TK_A), :]
        for c in range(n_pairs):
            s = _dot_nt(k, qs_sc[c])
            s_out[c] = s
            cm_out[c] = jnp.max(s, axis=0, keepdims=True)

    def consume(kb, s_in, cm_in):
        vt = vt_ref[kb]
        for c in range(n_pairs):
            m_old = m_sc[c]
            m_new = jnp.maximum(m_old, cm_in[c])
            alpha = jnp.exp2(m_old - m_new)
            p = jnp.exp2(s_in[c] - m_new).astype(BF16)
            acc_sc[c] = alpha * acc_sc[c] + _dot(vt, p)
            m_sc[c] = m_new

    bufs = ((sa_sc, cma_sc), (sb_sc, cmb_sc))
    scores(0, *bufs[0])

    def fused(kb, cur, nxt):
        s_in, cm_in = cur
        s_out, cm_out = nxt
        k0 = pl.multiple_of((kb + 1) * TK_A, TK_A)
        m_new, alpha, pv, cm = [], [], [], []
        for c in range(n_pairs):
            m_old = m_sc[c]
            m_new.append(jnp.maximum(m_old, cm_in[c]))
            alpha.append(jnp.exp2(m_old - m_new[c]))
            pv.append(None)
            cm.append(None)
        for j in range(TK_A // SUB_A):
            rows = slice(j * SUB_A, (j + 1) * SUB_A)
            k = k_ref[pl.ds(k0 + j * SUB_A, SUB_A), :]
            vt = vt_ref[kb, :, rows]
            for c in range(n_pairs):
                s = _dot_nt(k, qs_sc[c])
                s_out[c, rows, :] = s
                part = jnp.max(s, axis=0, keepdims=True)
                cm[c] = part if cm[c] is None else jnp.maximum(cm[c], part)
            for c in range(n_pairs):
                p = jnp.exp2(s_in[c, rows, :] - m_new[c]).astype(BF16)
                part = _dot(vt, p)
                pv[c] = part if pv[c] is None else pv[c] + part
        for c in range(n_pairs):
            cm_out[c] = cm[c]
            acc_sc[c] = alpha[c] * acc_sc[c] + pv[c]
            m_sc[c] = m_new[c]

    def body(kb, carry):
        for parity in range(2):
            @pl.when(kb % 2 == parity)
            def _():
                fused(kb, bufs[parity], bufs[1 - parity])
        return carry

    lax.fori_loop(0, n_k - 1, body, 0)
    consume(n_k - 1, *bufs[(n_k - 1) % 2])
    for c in range(n_pairs):
        o = acc_sc[c, 0:HEAD_DIM, :] / acc_sc[c, HEAD_DIM:HEAD_DIM + 1, :]
        pair = jnp.concatenate([o[:, 0:tq], o[:, tq:]], axis=0)
        o_ref[:, c * LANES:(c + 1) * LANES] = pair.T.astype(o_ref.dtype)


def _attn_a_bounded_kernel(q_ref, k_ref, vt_ref, o_ref, qt_sc, acc_sc):
    n_pairs = q_ref.shape[1] // LANES
    units = [(c, h) for h in range(q_ref.shape[0] // QH_A) for c in range(n_pairs)]
    low = lax.broadcasted_iota(jnp.int32, (QH_A, LANES), 1) < HEAD_DIM
    for u, (c, h) in enumerate(units):
        qc = q_ref[h * QH_A:(h + 1) * QH_A, c * LANES:(c + 1) * LANES]
        zero = jnp.zeros_like(qc)
        qs = jnp.concatenate([jnp.where(low, qc, zero), jnp.where(low, zero, qc)], axis=0)
        qt_sc[u] = qs.T
    n_sub = TK_A // SUB_A
    stream = [(t, u) for t in range(vt_ref.shape[0] * n_sub) for u in range(len(units))]

    def scores(i):
        t, u = stream[i]
        return _dot(k_ref[t * SUB_A:(t + 1) * SUB_A, :], qt_sc[u])

    depth = n_pairs
    ahead = [scores(i) for i in range(depth)]
    pv = [None] * len(units)
    den = [None] * len(units)
    for i, (t, u) in enumerate(stream):
        if i + depth < len(stream):
            ahead.append(scores(i + depth))
        cols = slice((t % n_sub) * SUB_A, (t % n_sub + 1) * SUB_A)
        p = jnp.exp2(ahead.pop(0))
        part = _dot(vt_ref[t // n_sub, 0:HEAD_DIM, cols], p.astype(BF16))
        pv[u] = part if pv[u] is None else pv[u] + part
        rows8 = jnp.sum(p.reshape(SUB_A // 8, 8, p.shape[1]), axis=0)
        den[u] = rows8 if den[u] is None else den[u] + rows8
    for u in range(len(units)):
        acc_sc[u, 0:HEAD_DIM, :] = pv[u]
        acc_sc[u, HEAD_DIM:HEAD_DIM + 8, :] = den[u]
    for u, (c, h) in enumerate(units):
        l = jnp.sum(acc_sc[u, HEAD_DIM:HEAD_DIM + 8, :], axis=0, keepdims=True)
        o = acc_sc[u, 0:HEAD_DIM, :] / l
        pair = jnp.concatenate([o[:, 0:QH_A], o[:, QH_A:]], axis=0)
        o_ref[h * QH_A:(h + 1) * QH_A, c * LANES:(c + 1) * LANES] = pair.T.astype(o_ref.dtype)


def _attn_a_bounded_call(qa, ka, vta):
    B, S, _ = qa.shape
    qcols = A_Q // N_KV_A
    n_units = (qcols // LANES) * (TQ_BOUNDED // QH_A)
    qspec = pl.BlockSpec((None, TQ_BOUNDED, qcols), lambda b, g, i: (b, i, g))
    kspec = pl.BlockSpec((None, None, S, LANES), lambda b, g, i: (b, g, 0, 0))
    vspec = pl.BlockSpec((None, None, S // TK_A, VT_ROWS, TK_A), lambda b, g, i: (b, g, 0, 0, 0))
    return pl.pallas_call(
        _attn_a_bounded_kernel,
        grid=(B, N_KV_A, S // TQ_BOUNDED),
        in_specs=[qspec, kspec, vspec],
        out_specs=qspec,
        out_shape=jax.ShapeDtypeStruct((B, S, A_Q), BF16),
        scratch_shapes=[pltpu.VMEM((n_units, LANES, 2 * QH_A), BF16),
                        pltpu.VMEM((n_units, VT_ROWS, 2 * QH_A), F32)],
        compiler_params=pltpu.CompilerParams(
            dimension_semantics=("arbitrary", "arbitrary", "arbitrary"),
            vmem_limit_bytes=VMEM_LIMIT),
        name="attn_a_bounded",
    )(qa, ka, vta)


def _attn_a_call(qa, ka, vta):
    B, S, _ = qa.shape
    qcols = A_Q // N_KV_A
    pairs = N_HEADS_A // N_KV_A // 2
    qspec = pl.BlockSpec((None, TQ_A, qcols), lambda b, g, i: (b, i, g))
    kspec = pl.BlockSpec((None, None, S, LANES), lambda b, g, i: (b, g, 0, 0))
    vspec = pl.BlockSpec((None, None, S // TK_A, VT_ROWS, TK_A), lambda b, g, i: (b, g, 0, 0, 0))
    return pl.pallas_call(
        _attn_a_kernel,
        grid=(B, N_KV_A, S // TQ_A),
        in_specs=[qspec, kspec, vspec],
        out_specs=qspec,
        out_shape=jax.ShapeDtypeStruct((B, S, A_Q), BF16),
        scratch_shapes=[pltpu.VMEM((pairs, 2 * TQ_A, LANES), BF16),
                        pltpu.VMEM((pairs, 1, 2 * TQ_A), F32),
                        pltpu.VMEM((pairs, VT_ROWS, 2 * TQ_A), F32),
                        pltpu.VMEM((pairs, TK_A, 2 * TQ_A), F32),
                        pltpu.VMEM((pairs, TK_A, 2 * TQ_A), F32),
                        pltpu.VMEM((pairs, 1, 2 * TQ_A), F32),
                        pltpu.VMEM((pairs, 1, 2 * TQ_A), F32)],
        compiler_params=pltpu.CompilerParams(
            dimension_semantics=("arbitrary", "arbitrary", "arbitrary"),
            vmem_limit_bytes=VMEM_LIMIT),
        name="attn_a",
    )(qa, ka, vta)


def _attn_b_kernel(q_ref, kp_ref, kc_ref, kn_ref, vp_ref, vc_ref, vn_ref,
                   o_ref, st_ref, kbuf, vbuf, *, seq_len):
    j = pl.program_id(2)
    tq = q_ref.shape[0]
    kbuf[0:SPAN_B, :] = kp_ref[...]
    kbuf[SPAN_B:SPAN_B + tq, :] = kc_ref[...]
    kbuf[SPAN_B + tq:, :] = kn_ref[...]
    vbuf[0:SPAN_B, :] = vp_ref[...]
    vbuf[SPAN_B:SPAN_B + tq, :] = vc_ref[...]
    vbuf[SPAN_B + tq:, :] = vn_ref[...]

    n_keys = SUB_B + 2 * SPAN_B
    c_idx = lax.broadcasted_iota(jnp.int32, (n_keys, 2 * SUB_B), 0)
    a_idx = lax.broadcasted_iota(jnp.int32, (n_keys, 2 * SUB_B), 1) % SUB_B
    band_bias = jnp.where(jnp.abs(c_idx - SPAN_B - a_idx) <= SPAN_B, 0.0, NEG_INF).astype(F32)
    low = lax.broadcasted_iota(jnp.int32, (SUB_B, LANES), 1) < HEAD_DIM
    ones = jnp.ones((VT_ROWS - HEAD_DIM, n_keys), BF16)
    pad = jnp.zeros((LANES - 2 * N_HEADS_B, SUB_B), F32)

    n_sub = tq // SUB_B
    n_pairs = N_HEADS_B // 2
    biases = {}

    def bias_of(i):
        if i not in biases:
            first_key = j * tq + i * SUB_B - SPAN_B
            bias = band_bias
            if i == 0:
                bias = jnp.where(c_idx >= -first_key, bias, NEG_INF)
            if i == n_sub - 1:
                bias = jnp.where(c_idx < seq_len - first_key, bias, NEG_INF)
            biases[i] = bias
        return biases[i]

    def scores(i, hp):
        q0, cols = i * SUB_B, slice(hp * LANES, (hp + 1) * LANES)
        qp = q_ref[q0:q0 + SUB_B, cols]
        zero = jnp.zeros_like(qp)
        qs = jnp.concatenate([jnp.where(low, qp, zero), jnp.where(low, zero, qp)], axis=0)
        return _dot(kbuf[q0:q0 + n_keys, cols], qs.T) + bias_of(i)

    def finish(i, hp, s):
        q0, cols = i * SUB_B, slice(hp * LANES, (hp + 1) * LANES)
        m = jnp.max(s, axis=0, keepdims=True)
        p = jnp.exp2(s - m).astype(BF16)
        vt = jnp.concatenate([vbuf[q0:q0 + n_keys, cols].T, ones], axis=0)
        o_all = _dot(vt, p)
        o_t = jnp.concatenate([o_all[0:HEAD_DIM, 0:SUB_B],
                               o_all[HEAD_DIM:LANES, SUB_B:]], axis=0)
        o_ref[q0:q0 + SUB_B, cols] = o_t.T.astype(o_ref.dtype)
        l = o_all[LANES:LANES + 1, :]
        return [m[:, 0:SUB_B], m[:, SUB_B:]], [l[:, 0:SUB_B], l[:, SUB_B:]]

    stream = [(i, hp) for i in range(n_sub) for hp in range(n_pairs)]
    depth = 6
    ahead = [scores(*stream[u]) for u in range(depth)]
    ms, ls = [], []
    for u, (i, hp) in enumerate(stream):
        if u + depth < len(stream):
            ahead.append(scores(*stream[u + depth]))
        m2, l2 = finish(i, hp, ahead.pop(0))
        ms += m2
        ls += l2
        if hp == n_pairs - 1:
            st_t = jnp.concatenate(ms + ls + [pad], axis=0)
            st_ref[i * SUB_B:(i + 1) * SUB_B, :] = st_t.T
            ms, ls = [], []


def _attn_b_call(qb, kb, vb):
    B, d, L, C = qb.shape
    nh = L // SPAN_B
    tq = min(TQ_B, L)
    per = tq // SPAN_B
    cur = pl.BlockSpec((None, None, tq, C), lambda b, r, j: (b, r, j, 0))
    prev = pl.BlockSpec((None, None, SPAN_B, C),
                        lambda b, r, j: (b, r, jnp.maximum(j * per - 1, 0), 0))
    nxt = pl.BlockSpec((None, None, SPAN_B, C),
                       lambda b, r, j: (b, r, jnp.minimum((j + 1) * per, nh - 1), 0))
    st_spec = pl.BlockSpec((None, None, tq, LANES), lambda b, r, j: (b, r, j, 0))
    return pl.pallas_call(
        functools.partial(_attn_b_kernel, seq_len=L),
        grid=(B, d, L // tq),
        in_specs=[cur, prev, cur, nxt, prev, cur, nxt],
        out_specs=[cur, st_spec],
        out_shape=[jax.ShapeDtypeStruct((B, d, L, C), BF16),
                   jax.ShapeDtypeStruct((B, d, L, LANES), F32)],
        scratch_shapes=[pltpu.VMEM((tq + 2 * SPAN_B, C), BF16),
                        pltpu.VMEM((tq + 2 * SPAN_B, C), BF16)],
        compiler_params=pltpu.CompilerParams(
            dimension_semantics=("arbitrary", "arbitrary", "arbitrary"),
            vmem_limit_bytes=VMEM_LIMIT),
        name=f"attn_b_d{d}",
    )(qb, kb, kb, kb, vb, vb, vb)


def _natural_order(ref, scratch):
    d, _, cols = ref.shape
    if d == 1:
        return ref[0].astype(F32)
    for r in range(d):
        for c in range(cols // LANES):
            scratch[c, pl.ds(r, TM // d, stride=d), :] = ref[r, :, c * LANES:(c + 1) * LANES].astype(F32)
    return jnp.concatenate([scratch[c] for c in range(cols // LANES)], axis=1)


def _mix_prologue(x_ref, ha_ref, o_refs, s_refs, ex_ref, wo_ref, mixg_ref, pre_ref, scratches):
    of2_sc, of3_sc, sf2_sc, sf3_sc = scratches
    stats = (_natural_order(s_refs[0], None), _natural_order(s_refs[1], sf2_sc),
             _natural_order(s_refs[2], sf3_sc))
    parts = (_natural_order(o_refs[0], None), _natural_order(o_refs[1], of2_sc),
             _natural_order(o_refs[2], of3_sc))
    is_max = lax.broadcasted_iota(jnp.int32, (TM, LANES), 1) < N_HEADS_B
    m_all = jnp.maximum(jnp.maximum(stats[0], stats[1]), stats[2])
    es = [jnp.exp2(s - m_all) for s in stats]
    den = None
    for e, s in zip(es, stats):
        term = e * pltpu.roll(s, LANES - N_HEADS_B, 1)
        den = term if den is None else den + term
    heads_b = None
    for e, o in zip(es, parts):
        w = jnp.where(is_max, e / den, 0.0)
        hi = w.astype(BF16)
        lo = (w - hi.astype(F32)).astype(BF16)
        term = (_dot(hi, ex_ref[...]) + _dot(lo, ex_ref[...])) * o
        heads_b = term if heads_b is None else heads_b + term

    mixed = _dot(ha_ref[...], wo_ref[0:A_Q, :]) + _dot(heads_b.astype(BF16), wo_ref[A_Q:, :])
    x2 = x_ref[...] + _rms(mixed, mixg_ref[...])
    return x2, _rms(x2, pre_ref[...]).astype(BF16)


def _out_ffn_kernel(*refs):
    first, nxt = refs[0:8], refs[8:16]
    ex_ref, wo_ref, mixg_ref, pre_ref, post_ref, wg_ref, wu_ref, wd_ref, y_ref = refs[16:25]
    x2_sc, h2_sc, scratches = refs[25], refs[26], refs[27:31]

    def prologue(group):
        return _mix_prologue(group[0], group[1], group[2:5], group[5:8], ex_ref, wo_ref,
                             mixg_ref, pre_ref, scratches)

    @pl.when(pl.program_id(0) == 0)
    def _():
        x2, h2 = prologue(first)
        x2_sc[...] = x2
        h2_sc[...] = h2

    ahead = []

    def next_tile():
        ahead.append(prologue(nxt))
        return ahead[0][1]

    f = _swiglu(h2_sc[...], wg_ref, wu_ref, wd_ref, next_tile)
    y_ref[...] = x2_sc[...] + 0.5 * _rms(f, post_ref[...])
    x2_sc[...], h2_sc[...] = ahead[0]


def _out_ffn_call(x1, heads_a, parts, expand, wo, mix_g, pre_g, post_g, wg, wu, wd):
    n = x1.shape[0]
    n_tiles = n // TM
    per_batch = parts[0][0].shape[2] // TM
    arrays = [x1, heads_a] + [o for o, _ in parts] + [s for _, s in parts]

    def specs(tile_of_step, **kw):
        out = []
        for a in arrays:
            if a.ndim == 2:
                out.append(pl.BlockSpec((TM, a.shape[1]), lambda i: (tile_of_step(i), 0), **kw))
            else:
                _, d, _, c = a.shape
                out.append(pl.BlockSpec(
                    (None, d, TM // d, c),
                    lambda i: (tile_of_step(i) // per_batch, 0, tile_of_step(i) % per_batch, 0), **kw))
        return out

    first = specs(lambda i: 0, pipeline_mode=pl.Buffered(1))
    nxt = specs(lambda i: jnp.minimum(i + 1, n_tiles - 1))
    return pl.pallas_call(
        _out_ffn_kernel,
        grid=(n_tiles,),
        in_specs=first + nxt + [
            _resident(expand.shape), _resident(wo.shape), _resident((1, D_MODEL)),
            _resident((1, D_MODEL)), _resident((1, D_MODEL)), _resident(wg.shape),
            _resident(wu.shape), _resident(wd.shape)],
        out_specs=pl.BlockSpec((TM, D_MODEL), lambda i: (i, 0)),
        out_shape=jax.ShapeDtypeStruct((n, D_MODEL), F32),
        scratch_shapes=[pltpu.VMEM((TM, D_MODEL), F32), pltpu.VMEM((TM, D_MODEL), BF16),
                        pltpu.VMEM((B_QKV // LANES, TM, LANES), F32),
                        pltpu.VMEM((B_QKV // LANES, TM, LANES), F32),
                        pltpu.VMEM((1, TM, LANES), F32), pltpu.VMEM((1, TM, LANES), F32)],
        compiler_params=pltpu.CompilerParams(
            dimension_semantics=("arbitrary",), vmem_limit_bytes=VMEM_LIMIT_OUT_FFN),
        name="out_ffn2",
    )(*arrays, *arrays, expand, wo, mix_g, pre_g, post_g, wg, wu, wd)


def _rope_tables(seq):
    pos = jnp.arange(seq, dtype=jnp.int32)
    row = (pos // GRID_W).astype(F32)[:, None]
    col = (pos % GRID_W).astype(F32)[:, None]
    dim_a = HEAD_DIM // 2
    fa = ROPE_THETA ** (-jnp.arange(0, dim_a, 2, dtype=F32) / dim_a)
    fb = ROPE_THETA ** (-jnp.arange(0, HEAD_DIM, 2, dtype=F32) / HEAD_DIM)
    ar, ac = row * fa[None, :], col * fa[None, :]
    ab = pos.astype(F32)[:, None] * fb[None, :]
    cos_a = jnp.concatenate([jnp.cos(ar), jnp.cos(ar), jnp.cos(ac), jnp.cos(ac)], axis=-1)
    sin_a = jnp.concatenate([-jnp.sin(ar), jnp.sin(ar), -jnp.sin(ac), jnp.sin(ac)], axis=-1)
    cos_b = jnp.concatenate([jnp.cos(ab), jnp.cos(ab)], axis=-1)
    sin_b = jnp.concatenate([-jnp.sin(ab), jnp.sin(ab)], axis=-1)
    two = lambda t: jnp.tile(t, (1, LANES // HEAD_DIM))
    return two(cos_a), two(sin_a), two(cos_b), two(sin_b)


def _layer(x, ffn1_pre_g, ffn1_post_g, ffn1_w_gate, ffn1_w_up, ffn1_w_down,
           mix_pre_g, mix_post_g, w_qkv, q_norm_g, k_norm_g, w_out,
           ffn2_pre_g, ffn2_post_g, ffn2_w_gate, ffn2_w_up, ffn2_w_down, tables, seg, expand):
    B, S, D = x.shape
    vec = lambda g: g.reshape(1, -1).astype(F32)
    bf = lambda w: w.astype(BF16)

    x1 = _ffn_call(x.reshape(B * S, D), vec(ffn1_pre_g), vec(ffn1_post_g),
                   bf(ffn1_w_gate), bf(ffn1_w_up), bf(ffn1_w_down))
    heads_per_tile = 2 * LANES // HEAD_DIM
    kv_gain = jnp.concatenate([jnp.tile(vec(k_norm_g), (1, N_KV_A)), jnp.ones((1, A_KV), F32)], axis=1)
    qa, ka, vta, qkv_b = _qkv_call(x1.reshape(B, S, D), vec(mix_pre_g), bf(w_qkv), seg,
                                   jnp.tile(vec(q_norm_g), (1, heads_per_tile)), kv_gain, *tables)
    score_bound = ((HEAD_DIM * SCALE * LOG2E) * jnp.max(jnp.abs(q_norm_g))
                   * jnp.max(jnp.abs(k_norm_g)))
    heads_a = lax.cond(score_bound <= MAX_SCORE_BOUND,
                       lambda: _attn_a_bounded_call(qa, ka, vta),
                       lambda: _attn_a_call(qa, ka, vta))
    parts = [_attn_b_call(*qkv) for qkv in qkv_b]
    y = _out_ffn_call(x1, heads_a.reshape(B * S, A_Q), parts, expand,
                      bf(w_out), vec(mix_post_g), vec(ffn2_pre_g), vec(ffn2_post_g),
                      bf(ffn2_w_gate), bf(ffn2_w_up), bf(ffn2_w_down))
    return y.reshape(B, S, D)


def kernel(x, ffn1_pre_g, ffn1_post_g, ffn1_w_gate, ffn1_w_up, ffn1_w_down, mix_pre_g, mix_post_g, w_qkv, q_norm_g, k_norm_g, w_out, ffn2_pre_g, ffn2_post_g, ffn2_w_gate, ffn2_w_up, ffn2_w_down):
    assert all(w // 2 // d == SPAN_B for w, d in DILATED_CONFIGS) and TK_A % TM == 0
    S = x.shape[1]
    tables = _rope_tables(S)
    head_of_lane = jnp.arange(2 * LANES) // HEAD_DIM
    seg = (head_of_lane[:, None] == head_of_lane[None, :]).astype(BF16) / HEAD_DIM
    expand = (jnp.arange(LANES)[:, None] == jnp.arange(B_QKV)[None, :] // HEAD_DIM).astype(BF16)
    params = (ffn1_pre_g, ffn1_post_g, ffn1_w_gate, ffn1_w_up, ffn1_w_down, mix_pre_g, mix_post_g,
              w_qkv, q_norm_g, k_norm_g, w_out, ffn2_pre_g, ffn2_post_g, ffn2_w_gate, ffn2_w_up,
              ffn2_w_down)
    for l in range(ffn1_pre_g.shape[0]):
        x = _layer(x, *(p[l] for p in params), tables, seg, expand)
    return x
```

```python
import functools

import jax
import jax.numpy as jnp
from jax import lax
from jax.experimental import pallas as pl
from jax.experimental.pallas import tpu as pltpu

D_MODEL = 1024
HEAD_DIM = 64
N_HEADS_A = 8
N_KV_A = 2
N_HEADS_B = 8
DILATED_CONFIGS = ((128, 1), (512, 4), (2048, 16))
DILATIONS = tuple(d for _, d in DILATED_CONFIGS)
GRID_W = 64
ROPE_THETA = 10000.0
D_FF = 2816
EPS = 1e-6
NEG_INF = -1e30

A_Q = N_HEADS_A * HEAD_DIM
A_KV = N_KV_A * HEAD_DIM
B_QKV = N_HEADS_B * HEAD_DIM
QKV_COLS = A_Q + 2 * A_KV + 3 * B_QKV
SCALE = HEAD_DIM ** -0.5
LOG2E = 1.4426950408889634

LANES = 128
FF_CHUNKS = ((0, 1024), (1024, 2048), (2048, 2816))
TM = 512
TQ_A = 256
TK_A = 1024
TQ_BOUNDED = 512
QH_A = 256
SUB_A = 256
MAX_SCORE_BOUND = 48.0
VT_ROWS = HEAD_DIM + 16
TQ_B = 1024
SUB_B = 128
SPAN_B = 64
VMEM_LIMIT = 52 * 1024 * 1024

BF16 = jnp.bfloat16
F32 = jnp.float32


def _dot(a, b):
    return jnp.dot(a, b, preferred_element_type=F32)


def _dot_nt(a, b):
    return lax.dot_general(a, b, (((1,), (1,)), ((), ())), preferred_element_type=F32)


def _rms(x, g):
    ms = jnp.mean(x * x, axis=-1, keepdims=True)
    return x * lax.rsqrt(ms + EPS) * g


def _swiglu_half_step(x, pre_g, post_g, wg_ref, wu_ref, wd_ref):
    h = _rms(x, pre_g).astype(BF16)
    f = None
    for lo, hi in FF_CHUNKS:
        g = _dot(h, wg_ref[:, lo:hi])
        u = _dot(h, wu_ref[:, lo:hi])
        a = (g / (1.0 + jnp.exp(-g)) * u).astype(BF16)
        part = _dot(a, wd_ref[lo:hi, :])
        f = part if f is None else f + part
    return x + 0.5 * _rms(f, post_g)


def _ffn_kernel(x_ref, pre_ref, post_ref, wg_ref, wu_ref, wd_ref, o_ref):
    o_ref[...] = _swiglu_half_step(x_ref[...], pre_ref[...], post_ref[...],
                                   wg_ref, wu_ref, wd_ref)


def _resident(shape):
    nd = len(shape)
    return pl.BlockSpec(shape, lambda *_: (0,) * nd, pipeline_mode=pl.Buffered(1))


def _ffn_call(x2d, pre_g, post_g, wg, wu, wd):
    n = x2d.shape[0]
    row = pl.BlockSpec((TM, D_MODEL), lambda i: (i, 0))
    return pl.pallas_call(
        _ffn_kernel,
        grid=(n // TM,),
        in_specs=[row, _resident((1, D_MODEL)), _resident((1, D_MODEL)),
                  _resident(wg.shape), _resident(wu.shape), _resident(wd.shape)],
        out_specs=row,
        out_shape=jax.ShapeDtypeStruct(x2d.shape, F32),
        compiler_params=pltpu.CompilerParams(
            dimension_semantics=("arbitrary",), vmem_limit_bytes=VMEM_LIMIT),
        name="ffn1",
    )(x2d, pre_g, post_g, wg, wu, wd)


def _rope(c, cos, sin_signed, half):
    lane = lax.broadcasted_iota(jnp.int32, c.shape, 1)
    first = (lane % (2 * half)) < half
    partner = jnp.where(first, pltpu.roll(c, LANES - half, 1), pltpu.roll(c, half, 1))
    return c * cos + partner * sin_signed


def _head_rms(c, seg_ref, g):
    sq = c * c
    hi = sq.astype(BF16)
    lo = (sq - hi.astype(F32)).astype(BF16)
    ms = _dot(hi, seg_ref[...]) + _dot(lo, seg_ref[...])
    return c * lax.rsqrt(ms + EPS) * g


def _dup_halves(c):
    lane = lax.broadcasted_iota(jnp.int32, c.shape, 1)
    low = lane < HEAD_DIM
    r = pltpu.roll(c, HEAD_DIM, 1)
    return jnp.where(low, c, r), jnp.where(low, r, c)


def _qkv_kernel(x_ref, pre_ref, w_ref, seg_ref, qg_ref, kg_ref,
                cosa_ref, sina_ref, cosb_ref, sinb_ref,
                qa_ref, ka_ref, va_ref, *rest):
    b_refs, (qf_sc, kf_sc, vf_sc) = rest[:-3], rest[-3:]
    h = _rms(x_ref[...], pre_ref[...]).astype(BF16)
    cosa, sina = cosa_ref[...], sina_ref[...]
    cosb, sinb = cosb_ref[...], sinb_ref[...]
    wide = 2 * LANES

    qa = _dot(h, w_ref[:, 0:A_Q])
    for t in range(A_Q // wide):
        qn = _head_rms(qa[:, t * wide:(t + 1) * wide], seg_ref, qg_ref[...])
        for c in range(wide // LANES):
            q = _rope(qn[:, c * LANES:(c + 1) * LANES], cosa, sina, HEAD_DIM // 4) * (SCALE * LOG2E)
            col = t * wide + c * LANES
            qa_ref[:, col:col + LANES] = q.astype(BF16)

    kv = _dot(h, w_ref[:, A_Q:A_Q + 2 * A_KV])
    k = _rope(_head_rms(kv, seg_ref, kg_ref[...])[:, 0:A_KV], cosa, sina, HEAD_DIM // 4)
    k0, k1 = _dup_halves(k)
    ka_ref[0] = k0.astype(BF16)
    ka_ref[1] = k1.astype(BF16)
    vt = kv[:, A_KV:].T.astype(BF16)
    ones = jnp.ones((VT_ROWS - HEAD_DIM, TM), BF16)
    for g in range(N_KV_A):
        va_ref[g, 0:HEAD_DIM, :] = vt[g * HEAD_DIM:(g + 1) * HEAD_DIM, :]
        va_ref[g, HEAD_DIM:, :] = ones

    base = A_Q + 2 * A_KV
    qb = _dot(h, w_ref[:, base:base + B_QKV])
    kb = _dot(h, w_ref[:, base + B_QKV:base + 2 * B_QKV])
    vb = _dot(h, w_ref[:, base + 2 * B_QKV:base + 3 * B_QKV])
    for c in range(B_QKV // LANES):
        sl = slice(c * LANES, (c + 1) * LANES)
        qf_sc[c] = _rope(qb[:, sl], cosb, sinb, HEAD_DIM // 2) * (SCALE * LOG2E)
        kf_sc[c] = _rope(kb[:, sl], cosb, sinb, HEAD_DIM // 2)
        vf_sc[c] = vb[:, sl]
    for j, src in enumerate((qf_sc, kf_sc, vf_sc)):
        for i, d in enumerate(DILATIONS):
            out = b_refs[3 * i + j]
            for r in range(d):
                rows = slice(None) if d == 1 else pl.ds(r, TM // d, stride=d)
                for c in range(B_QKV // LANES):
                    out[r, :, c * LANES:(c + 1) * LANES] = src[c, rows, :].astype(BF16)


def _qkv_call(x1, pre_g, w, seg, qg, kg, cosa, sina, cosb, sinb):
    B, S, _ = x1.shape
    tab = pl.BlockSpec((TM, LANES), lambda i, b: (i, 0))
    wide = pl.BlockSpec((None, TM, A_Q), lambda i, b: (b, i, 0))
    dup = pl.BlockSpec((None, N_KV_A, TM, LANES), lambda i, b: (b, 0, i, 0))
    per = TK_A // TM
    vt = pl.BlockSpec((None, N_KV_A, None, VT_ROWS, TM), lambda i, b: (b, 0, i // per, 0, i % per))
    wide_shape = jax.ShapeDtypeStruct((B, S, A_Q), BF16)
    dup_shape = jax.ShapeDtypeStruct((B, N_KV_A, S, LANES), BF16)
    vt_shape = jax.ShapeDtypeStruct((B, N_KV_A, S // TK_A, VT_ROWS, TK_A), BF16)
    b_specs, b_shapes = [], []
    for d in DILATIONS:
        b_specs += [pl.BlockSpec((None, d, TM // d, B_QKV), lambda i, b: (b, 0, i, 0))] * 3
        b_shapes += [jax.ShapeDtypeStruct((B, d, S // d, B_QKV), BF16)] * 3
    outs = pl.pallas_call(
        _qkv_kernel,
        grid=(S // TM, B),
        in_specs=[pl.BlockSpec((None, TM, D_MODEL), lambda i, b: (b, i, 0)),
                  _resident((1, D_MODEL)), _resident(w.shape), _resident(seg.shape),
                  _resident(qg.shape), _resident(kg.shape), tab, tab, tab, tab],
        out_specs=[wide, dup, vt] + b_specs,
        out_shape=[wide_shape, dup_shape, vt_shape] + b_shapes,
        scratch_shapes=[pltpu.VMEM((B_QKV // LANES, TM, LANES), F32)] * 3,
        compiler_params=pltpu.CompilerParams(
            dimension_semantics=("arbitrary", "arbitrary"), vmem_limit_bytes=VMEM_LIMIT),
        name="qkv",
    )(x1, pre_g, w, seg, qg, kg, cosa, sina, cosb, sinb)
    qa, ka, vta = outs[:3]
    qkv_b = [tuple(outs[3 + 3 * i:6 + 3 * i]) for i in range(len(DILATIONS))]
    return qa, ka, vta, qkv_b


def _attn_a_kernel(q_ref, k_ref, vt_ref, o_ref, qs_sc, m_sc, acc_sc, sa_sc, sb_sc, cma_sc, cmb_sc):
    tq = q_ref.shape[0]
    n_pairs = qs_sc.shape[0]
    low = lax.broadcasted_iota(jnp.int32, (tq, LANES), 1) < HEAD_DIM
    for c in range(n_pairs):
        qc = q_ref[:, c * LANES:(c + 1) * LANES]
        zero = jnp.zeros_like(qc)
        qs_sc[c, 0:tq, :] = jnp.where(low, qc, zero)
        qs_sc[c, tq:, :] = jnp.where(low, zero, qc)

    m_sc[...] = jnp.full(m_sc.shape, NEG_INF, F32)
    acc_sc[...] = jnp.zeros(acc_sc.shape, F32)

    n_k = vt_ref.shape[0]

    def scores(kb, s_out, cm_out):
        k = k_ref[pl.ds(pl.multiple_of(kb * TK_A, TK_A), TK_A), :]
        for c in range(n_pairs):
            s = _dot_nt(k, qs_sc[c])
            s_out[c] = s
            cm_out[c] = jnp.max(s, axis=0, keepdims=True)

    def consume(kb, s_in, cm_in):
        vt = vt_ref[kb]
        for c in range(n_pairs):
            m_old = m_sc[c]
            m_new = jnp.maximum(m_old, cm_in[c])
            alpha = jnp.exp2(m_old - m_new)
            p = jnp.exp2(s_in[c] - m_new).astype(BF16)
            acc_sc[c] = alpha * acc_sc[c] + _dot(vt, p)
            m_sc[c] = m_new

    bufs = ((sa_sc, cma_sc), (sb_sc, cmb_sc))
    scores(0, *bufs[0])

    def fused(kb, cur, nxt):
        s_in, cm_in = cur
        s_out, cm_out = nxt
        k0 = pl.multiple_of((kb + 1) * TK_A, TK_A)
        m_new, alpha, pv, cm = [], [], [], []
        for c in range(n_pairs):
            m_old = m_sc[c]
            m_new.append(jnp.maximum(m_old, cm_in[c]))
            alpha.append(jnp.exp2(m_old - m_new[c]))
            pv.append(None)
            cm.append(None)
        for j in range(TK_A // SUB_A):
            rows = slice(j * SUB_A, (j + 1) * SUB_A)
            k = k_ref[pl.ds(k0 + j * SUB_A, SUB_A), :]
            vt = vt_ref[kb, :, rows]
            for c in range(n_pairs):
                s = _dot_nt(k, qs_sc[c])
                s_out[c, rows, :] = s
                part = jnp.max(s, axis=0, keepdims=True)
                cm[c] = part if cm[c] is None else jnp.maximum(cm[c], part)
            for c in range(n_pairs):
                p = jnp.exp2(s_in[c, rows, :] - m_new[c]).astype(BF16)
                part = _dot(vt, p)
                pv[c] = part if pv[c] is None else pv[c] + part
        for c in range(n_pairs):
            cm_out[c] = cm[c]
            acc_sc[c] = alpha[c] * acc_sc[c] + pv[c]
            m_sc[c] = m_new[c]

    def body(kb, carry):
        for parity in range(2):
            @pl.when(kb % 2 == parity)
            def _():
                fused(kb, bufs[parity], bufs[1 - parity])
        return carry

    lax.fori_loop(0, n_k - 1, body, 0)
    consume(n_k - 1, *bufs[(n_k - 1) % 2])
    for c in range(n_pairs):
        o = acc_sc[c, 0:HEAD_DIM, :] / acc_sc[c, HEAD_DIM:HEAD_DIM + 1, :]
        pair = jnp.concatenate([o[:, 0:tq], o[:, tq:]], axis=0)
        o_ref[:, c * LANES:(c + 1) * LANES] = pair.T.astype(o_ref.dtype)


def _attn_a_bounded_kernel(q_ref, k_ref, vt_ref, o_ref, qt_sc, acc_sc):
    n_pairs = q_ref.shape[1] // LANES
    units = [(c, h) for h in range(q_ref.shape[0] // QH_A) for c in range(n_pairs)]
    low = lax.broadcasted_iota(jnp.int32, (QH_A, LANES), 1) < HEAD_DIM
    for u, (c, h) in enumerate(units):
        qc = q_ref[h * QH_A:(h + 1) * QH_A, c * LANES:(c + 1) * LANES]
        zero = jnp.zeros_like(qc)
        qs = jnp.concatenate([jnp.where(low, qc, zero), jnp.where(low, zero, qc)], axis=0)
        qt_sc[u] = qs.T
    n_sub = TK_A // SUB_A
    stream = [(t, u) for t in range(vt_ref.shape[0] * n_sub) for u in range(len(units))]

    def scores(i):
        t, u = stream[i]
        return _dot(k_ref[t * SUB_A:(t + 1) * SUB_A, :], qt_sc[u])

    depth = n_pairs
    ahead = [scores(i) for i in range(depth)]
    pv = [None] * len(units)
    den = [None] * len(units)
    for i, (t, u) in enumerate(stream):
        if i + depth < len(stream):
            ahead.append(scores(i + depth))
        cols = slice((t % n_sub) * SUB_A, (t % n_sub + 1) * SUB_A)
        p = jnp.exp2(ahead.pop(0))
        part = _dot(vt_ref[t // n_sub, 0:HEAD_DIM, cols], p.astype(BF16))
        pv[u] = part if pv[u] is None else pv[u] + part
        rows8 = jnp.sum(p.reshape(SUB_A // 8, 8, p.shape[1]), axis=0)
        den[u] = rows8 if den[u] is None else den[u] + rows8
    for u in range(len(units)):
        acc_sc[u, 0:HEAD_DIM, :] = pv[u]
        acc_sc[u, HEAD_DIM:HEAD_DIM + 8, :] = den[u]
    for u, (c, h) in enumerate(units):
        l = jnp.sum(acc_sc[u, HEAD_DIM:HEAD_DIM + 8, :], axis=0, keepdims=True)
        o = acc_sc[u, 0:HEAD_DIM, :] / l
        pair = jnp.concatenate([o[:, 0:QH_A], o[:, QH_A:]], axis=0)
        o_ref[h * QH_A:(h + 1) * QH_A, c * LANES:(c + 1) * LANES] = pair.T.astype(o_ref.dtype)


def _attn_a_bounded_call(qa, ka, vta):
    B, S, _ = qa.shape
    qcols = A_Q // N_KV_A
    n_units = (qcols // LANES) * (TQ_BOUNDED // QH_A)
    qspec = pl.BlockSpec((None, TQ_BOUNDED, qcols), lambda b, g, i: (b, i, g))
    kspec = pl.BlockSpec((None, None, S, LANES), lambda b, g, i: (b, g, 0, 0))
    vspec = pl.BlockSpec((None, None, S // TK_A, VT_ROWS, TK_A), lambda b, g, i: (b, g, 0, 0, 0))
    return pl.pallas_call(
        _attn_a_bounded_kernel,
        grid=(B, N_KV_A, S // TQ_BOUNDED),
        in_specs=[qspec, kspec, vspec],
        out_specs=qspec,
        out_shape=jax.ShapeDtypeStruct((B, S, A_Q), BF16),
        scratch_shapes=[pltpu.VMEM((n_units, LANES, 2 * QH_A), BF16),
                        pltpu.VMEM((n_units, VT_ROWS, 2 * QH_A), F32)],
        compiler_params=pltpu.CompilerParams(
            dimension_semantics=("arbitrary", "arbitrary", "arbitrary"),
            vmem_limit_bytes=VMEM_LIMIT),
        name="attn_a_bounded",
    )(qa, ka, vta)


def _attn_a_call(qa, ka, vta):
    B, S, _ = qa.shape
    qcols = A_Q // N_KV_A
    pairs = N_HEADS_A // N_KV_A // 2
    qspec = pl.BlockSpec((None, TQ_A, qcols), lambda b, g, i: (b, i, g))
    kspec = pl.BlockSpec((None, None, S, LANES), lambda b, g, i: (b, g, 0, 0))
    vspec = pl.BlockSpec((None, None, S // TK_A, VT_ROWS, TK_A), lambda b, g, i: (b, g, 0, 0, 0))
    return pl.pallas_call(
        _attn_a_kernel,
        grid=(B, N_KV_A, S // TQ_A),
        in_specs=[qspec, kspec, vspec],
        out_specs=qspec,
        out_shape=jax.ShapeDtypeStruct((B, S, A_Q), BF16),
        scratch_shapes=[pltpu.VMEM((pairs, 2 * TQ_A, LANES), BF16),
                        pltpu.VMEM((pairs, 1, 2 * TQ_A), F32),
                        pltpu.VMEM((pairs, VT_ROWS, 2 * TQ_A), F32),
                        pltpu.VMEM((pairs, TK_A, 2 * TQ_A), F32),
                        pltpu.VMEM((pairs, TK_A, 2 * TQ_A), F32),
                        pltpu.VMEM((pairs, 1, 2 * TQ_A), F32),
                        pltpu.VMEM((pairs, 1, 2 * TQ_A), F32)],
        compiler_params=pltpu.CompilerParams(
            dimension_semantics=("arbitrary", "arbitrary", "arbitrary"),
            vmem_limit_bytes=VMEM_LIMIT),
        name="attn_a",
    )(qa, ka, vta)


def _attn_b_kernel(q_ref, kp_ref, kc_ref, kn_ref, vp_ref, vc_ref, vn_ref,
                   o_ref, st_ref, kbuf, vbuf, *, seq_len):
    j = pl.program_id(2)
    tq = q_ref.shape[0]
    kbuf[0:SPAN_B, :] = kp_ref[...]
    kbuf[SPAN_B:SPAN_B + tq, :] = kc_ref[...]
    kbuf[SPAN_B + tq:, :] = kn_ref[...]
    vbuf[0:SPAN_B, :] = vp_ref[...]
    vbuf[SPAN_B:SPAN_B + tq, :] = vc_ref[...]
    vbuf[SPAN_B + tq:, :] = vn_ref[...]

    n_keys = SUB_B + 2 * SPAN_B
    c_idx = lax.broadcasted_iota(jnp.int32, (n_keys, 2 * SUB_B), 0)
    a_idx = lax.broadcasted_iota(jnp.int32, (n_keys, 2 * SUB_B), 1) % SUB_B
    band_bias = jnp.where(jnp.abs(c_idx - SPAN_B - a_idx) <= SPAN_B, 0.0, NEG_INF).astype(F32)
    low = lax.broadcasted_iota(jnp.int32, (SUB_B, LANES), 1) < HEAD_DIM
    ones = jnp.ones((VT_ROWS - HEAD_DIM, n_keys), BF16)
    pad = jnp.zeros((LANES - 2 * N_HEADS_B, SUB_B), F32)

    n_sub = tq // SUB_B
    n_pairs = N_HEADS_B // 2
    biases = {}

    def bias_of(i):
        if i not in biases:
            first_key = j * tq + i * SUB_B - SPAN_B
            bias = band_bias
            if i == 0:
                bias = jnp.where(c_idx >= -first_key, bias, NEG_INF)
            if i == n_sub - 1:
                bias = jnp.where(c_idx < seq_len - first_key, bias, NEG_INF)
            biases[i] = bias
        return biases[i]

    def scores(i, hp):
        q0, cols = i * SUB_B, slice(hp * LANES, (hp + 1) * LANES)
        qp = q_ref[q0:q0 + SUB_B, cols]
        zero = jnp.zeros_like(qp)
        qs = jnp.concatenate([jnp.where(low, qp, zero), jnp.where(low, zero, qp)], axis=0)
        return _dot(kbuf[q0:q0 + n_keys, cols], qs.T) + bias_of(i)

    def finish(i, hp, s):
        q0, cols = i * SUB_B, slice(hp * LANES, (hp + 1) * LANES)
        m = jnp.max(s, axis=0, keepdims=True)
        p = jnp.exp2(s - m).astype(BF16)
        vt = jnp.concatenate([vbuf[q0:q0 + n_keys, cols].T, ones], axis=0)
        o_all = _dot(vt, p)
        o_t = jnp.concatenate([o_all[0:HEAD_DIM, 0:SUB_B],
                               o_all[HEAD_DIM:LANES, SUB_B:]], axis=0)
        o_ref[q0:q0 + SUB_B, cols] = o_t.T.astype(o_ref.dtype)
        l = o_all[LANES:LANES + 1, :]
        return [m[:, 0:SUB_B], m[:, SUB_B:]], [l[:, 0:SUB_B], l[:, SUB_B:]]

    stream = [(i, hp) for i in range(n_sub) for hp in range(n_pairs)]
    depth = 6
    ahead = [scores(*stream[u]) for u in range(depth)]
    ms, ls = [], []
    for u, (i, hp) in enumerate(stream):
        if u + depth < len(stream):
            ahead.append(scores(*stream[u + depth]))
        m2, l2 = finish(i, hp, ahead.pop(0))
        ms += m2
        ls += l2
        if hp == n_pairs - 1:
            st_t = jnp.concatenate(ms + ls + [pad], axis=0)
            st_ref[i * SUB_B:(i + 1) * SUB_B, :] = st_t.T
            ms, ls = [], []


def _attn_b_call(qb, kb, vb):
    B, d, L, C = qb.shape
    nh = L // SPAN_B
    tq = min(TQ_B, L)
    per = tq // SPAN_B
    cur = pl.BlockSpec((None, None, tq, C), lambda b, r, j: (b, r, j, 0))
    prev = pl.BlockSpec((None, None, SPAN_B, C),
                        lambda b, r, j: (b, r, jnp.maximum(j * per - 1, 0), 0))
    nxt = pl.BlockSpec((None, None, SPAN_B, C),
                       lambda b, r, j: (b, r, jnp.minimum((j + 1) * per, nh - 1), 0))
    st_spec = pl.BlockSpec((None, None, tq, LANES), lambda b, r, j: (b, r, j, 0))
    return pl.pallas_call(
        functools.partial(_attn_b_kernel, seq_len=L),
        grid=(B, d, L // tq),
        in_specs=[cur, prev, cur, nxt, prev, cur, nxt],
        out_specs=[cur, st_spec],
        out_shape=[jax.ShapeDtypeStruct((B, d, L, C), BF16),
                   jax.ShapeDtypeStruct((B, d, L, LANES), F32)],
        scratch_shapes=[pltpu.VMEM((tq + 2 * SPAN_B, C), BF16),
                        pltpu.VMEM((tq + 2 * SPAN_B, C), BF16)],
        compiler_params=pltpu.CompilerParams(
            dimension_semantics=("arbitrary", "arbitrary", "arbitrary"),
            vmem_limit_bytes=VMEM_LIMIT),
        name=f"attn_b_d{d}",
    )(qb, kb, kb, kb, vb, vb, vb)


def _natural_order(ref, scratch):
    d, _, cols = ref.shape
    if d == 1:
        return ref[0].astype(F32)
    for r in range(d):
        for c in range(cols // LANES):
            scratch[c, pl.ds(r, TM // d, stride=d), :] = ref[r, :, c * LANES:(c + 1) * LANES].astype(F32)
    return jnp.concatenate([scratch[c] for c in range(cols // LANES)], axis=1)


def _out_ffn_kernel(x_ref, ha_ref, o1_ref, o2_ref, o3_ref, s1_ref, s2_ref, s3_ref, ex_ref,
                    wo_ref, mixg_ref, pre_ref, post_ref, wg_ref, wu_ref, wd_ref, y_ref,
                    of2_sc, of3_sc, sf2_sc, sf3_sc):
    stats = (_natural_order(s1_ref, None), _natural_order(s2_ref, sf2_sc),
             _natural_order(s3_ref, sf3_sc))
    parts = (_natural_order(o1_ref, None), _natural_order(o2_ref, of2_sc),
             _natural_order(o3_ref, of3_sc))
    is_max = lax.broadcasted_iota(jnp.int32, (TM, LANES), 1) < N_HEADS_B
    m_all = jnp.maximum(jnp.maximum(stats[0], stats[1]), stats[2])
    es = [jnp.exp2(s - m_all) for s in stats]
    den = None
    for e, s in zip(es, stats):
        term = e * pltpu.roll(s, LANES - N_HEADS_B, 1)
        den = term if den is None else den + term
    heads_b = None
    for e, o in zip(es, parts):
        w = jnp.where(is_max, e / den, 0.0)
        hi = w.astype(BF16)
        lo = (w - hi.astype(F32)).astype(BF16)
        term = (_dot(hi, ex_ref[...]) + _dot(lo, ex_ref[...])) * o
        heads_b = term if heads_b is None else heads_b + term

    mixed = _dot(ha_ref[...], wo_ref[0:A_Q, :]) + _dot(heads_b.astype(BF16), wo_ref[A_Q:, :])
    x2 = x_ref[...] + _rms(mixed, mixg_ref[...])
    y_ref[...] = _swiglu_half_step(x2, pre_ref[...], post_ref[...], wg_ref, wu_ref, wd_ref)


def _out_ffn_call(x1, heads_a, parts, expand, wo, mix_g, pre_g, post_g, wg, wu, wd):
    n = x1.shape[0]
    per_batch = parts[0][0].shape[2] // TM
    row = lambda c: pl.BlockSpec((TM, c), lambda i: (i, 0))

    def strided(a):
        _, d, _, c = a.shape
        return pl.BlockSpec((None, d, TM // d, c), lambda i: (i // per_batch, 0, i % per_batch, 0))

    (o1, s1), (o2, s2), (o3, s3) = parts
    return pl.pallas_call(
        _out_ffn_kernel,
        grid=(n // TM,),
        in_specs=[row(D_MODEL), row(A_Q), strided(o1), strided(o2), strided(o3),
                  strided(s1), strided(s2), strided(s3), _resident(expand.shape),
                  _resident(wo.shape), _resident((1, D_MODEL)), _resident((1, D_MODEL)),
                  _resident((1, D_MODEL)), _resident(wg.shape), _resident(wu.shape),
                  _resident(wd.shape)],
        out_specs=row(D_MODEL),
        out_shape=jax.ShapeDtypeStruct((n, D_MODEL), F32),
        scratch_shapes=[pltpu.VMEM((B_QKV // LANES, TM, LANES), F32),
                        pltpu.VMEM((B_QKV // LANES, TM, LANES), F32),
                        pltpu.VMEM((1, TM, LANES), F32), pltpu.VMEM((1, TM, LANES), F32)],
        compiler_params=pltpu.CompilerParams(
            dimension_semantics=("arbitrary",), vmem_limit_bytes=VMEM_LIMIT),
        name="out_ffn2",
    )(x1, heads_a, o1, o2, o3, s1, s2, s3, expand, wo, mix_g, pre_g, post_g, wg, wu, wd)


def _rope_tables(seq):
    pos = jnp.arange(seq, dtype=jnp.int32)[:, None]
    within = (jnp.arange(LANES, dtype=jnp.int32) % HEAD_DIM)[None, :]
    dim_a = HEAD_DIM // 2
    blk, sub = within // (dim_a // 2), within % (dim_a // 2)
    pos_a = jnp.where(blk < 2, pos // GRID_W, pos % GRID_W).astype(F32)
    ang_a = pos_a * ROPE_THETA ** (-(2 * sub).astype(F32) / dim_a)
    sign_a = jnp.where(blk % 2 == 0, -1.0, 1.0).astype(F32)
    half = HEAD_DIM // 2
    ang_b = pos.astype(F32) * ROPE_THETA ** (-(2 * (within % half)).astype(F32) / HEAD_DIM)
    sign_b = jnp.where(within < half, -1.0, 1.0).astype(F32)
    return jnp.cos(ang_a), sign_a * jnp.sin(ang_a), jnp.cos(ang_b), sign_b * jnp.sin(ang_b)


def _layer(x, ffn1_pre_g, ffn1_post_g, ffn1_w_gate, ffn1_w_up, ffn1_w_down,
           mix_pre_g, mix_post_g, w_qkv, q_norm_g, k_norm_g, w_out,
           ffn2_pre_g, ffn2_post_g, ffn2_w_gate, ffn2_w_up, ffn2_w_down, tables, seg, expand):
    B, S, D = x.shape
    vec = lambda g: g.reshape(1, -1).astype(F32)
    bf = lambda w: w.astype(BF16)

    x1 = _ffn_call(x.reshape(B * S, D), vec(ffn1_pre_g), vec(ffn1_post_g),
                   bf(ffn1_w_gate), bf(ffn1_w_up), bf(ffn1_w_down))
    heads_per_tile = 2 * LANES // HEAD_DIM
    kv_gain = jnp.concatenate([jnp.tile(vec(k_norm_g), (1, N_KV_A)), jnp.ones((1, A_KV), F32)], axis=1)
    qa, ka, vta, qkv_b = _qkv_call(x1.reshape(B, S, D), vec(mix_pre_g), bf(w_qkv), seg,
                                   jnp.tile(vec(q_norm_g), (1, heads_per_tile)), kv_gain, *tables)
    score_bound = ((HEAD_DIM * SCALE * LOG2E) * jnp.max(jnp.abs(q_norm_g))
                   * jnp.max(jnp.abs(k_norm_g)))
    heads_a = lax.cond(score_bound <= MAX_SCORE_BOUND,
                       lambda: _attn_a_bounded_call(qa, ka, vta),
                       lambda: _attn_a_call(qa, ka, vta))
    parts = [_attn_b_call(*qkv) for qkv in qkv_b]
    y = _out_ffn_call(x1, heads_a.reshape(B * S, A_Q), parts, expand,
                      bf(w_out), vec(mix_post_g), vec(ffn2_pre_g), vec(ffn2_post_g),
                      bf(ffn2_w_gate), bf(ffn2_w_up), bf(ffn2_w_down))
    return y.reshape(B, S, D)


def kernel(x, ffn1_pre_g, ffn1_post_g, ffn1_w_gate, ffn1_w_up, ffn1_w_down, mix_pre_g, mix_post_g, w_qkv, q_norm_g, k_norm_g, w_out, ffn2_pre_g, ffn2_post_g, ffn2_w_gate, ffn2_w_up, ffn2_w_down):
    assert all(w // 2 // d == SPAN_B for w, d in DILATED_CONFIGS) and TK_A % TM == 0
    S = x.shape[1]
    tables = _rope_tables(S)
    head_of_lane = jnp.arange(2 * LANES) // HEAD_DIM
    seg = (head_of_lane[:, None] == head_of_lane[None, :]).astype(BF16) / HEAD_DIM
    expand = (jnp.arange(LANES)[:, None] == jnp.arange(B_QKV)[None, :] // HEAD_DIM).astype(BF16)
    params = (ffn1_pre_g, ffn1_post_g, ffn1_w_gate, ffn1_w_up, ffn1_w_down, mix_pre_g, mix_post_g,
              w_qkv, q_norm_g, k_norm_g, w_out, ffn2_pre_g, ffn2_post_g, ffn2_w_gate, ffn2_w_up,
              ffn2_w_down)
    for l in range(ffn1_pre_g.shape[0]):
        x = _layer(x, *(p[l] for p in params), tables, seg, expand)
    return x
```

```python
import functools

import jax
import jax.numpy as jnp
import numpy as np
from jax import lax
from jax.experimental import pallas as pl
from jax.experimental.pallas import tpu as pltpu

D_MODEL = 1024
HEAD_DIM = 64
N_HEADS_A = 8
N_KV_A = 2
N_HEADS_B = 8
DILATED_CONFIGS = ((128, 1), (512, 4), (2048, 16))
DILATIONS = tuple(d for _, d in DILATED_CONFIGS)
GRID_W = 64
ROPE_THETA = 10000.0
D_FF = 2816
EPS = 1e-6
NEG_INF = -1e30

A_Q = N_HEADS_A * HEAD_DIM
A_KV = N_KV_A * HEAD_DIM
B_QKV = N_HEADS_B * HEAD_DIM
QKV_COLS = A_Q + 2 * A_KV + 3 * B_QKV
SCALE = HEAD_DIM ** -0.5
LOG2E = 1.4426950408889634

LANES = 128
FF_CHUNKS = ((0, 1024), (1024, 2048), (2048, 2816))
TM = 512
TQ_A = 256
TK_A = 1024
TQ_BOUNDED = 512
QH_A = 256
SUB_A = 256
MAX_SCORE_BOUND = 48.0
VT_ROWS = HEAD_DIM + 16
TQ_B = 1024
SUB_B = 128
SPAN_B = 64
VMEM_LIMIT = 52 * 1024 * 1024

BF16 = jnp.bfloat16
F32 = jnp.float32


def _dot(a, b):
    return jnp.dot(a, b, preferred_element_type=F32)


def _dot_nt(a, b):
    return lax.dot_general(a, b, (((1,), (1,)), ((), ())), preferred_element_type=F32)


def _rms(x, g):
    ms = jnp.mean(x * x, axis=-1, keepdims=True)
    return x * lax.rsqrt(ms + EPS) * g


def _swiglu_half_step(x, pre_g, post_g, wg_ref, wu_ref, wd_ref):
    h = _rms(x, pre_g).astype(BF16)
    f = None
    for lo, hi in FF_CHUNKS:
        g = _dot(h, wg_ref[:, lo:hi])
        u = _dot(h, wu_ref[:, lo:hi])
        a = (g / (1.0 + jnp.exp(-g)) * u).astype(BF16)
        part = _dot(a, wd_ref[lo:hi, :])
        f = part if f is None else f + part
    return x + 0.5 * _rms(f, post_g)


def _ffn_kernel(x_ref, pre_ref, post_ref, wg_ref, wu_ref, wd_ref, o_ref):
    o_ref[...] = _swiglu_half_step(x_ref[...], pre_ref[...], post_ref[...],
                                   wg_ref, wu_ref, wd_ref)


def _resident(shape):
    nd = len(shape)
    return pl.BlockSpec(shape, lambda *_: (0,) * nd, pipeline_mode=pl.Buffered(1))


def _ffn_call(x2d, pre_g, post_g, wg, wu, wd):
    n = x2d.shape[0]
    row = pl.BlockSpec((TM, D_MODEL), lambda i: (i, 0))
    return pl.pallas_call(
        _ffn_kernel,
        grid=(n // TM,),
        in_specs=[row, _resident((1, D_MODEL)), _resident((1, D_MODEL)),
                  _resident(wg.shape), _resident(wu.shape), _resident(wd.shape)],
        out_specs=row,
        out_shape=jax.ShapeDtypeStruct(x2d.shape, F32),
        compiler_params=pltpu.CompilerParams(
            dimension_semantics=("arbitrary",), vmem_limit_bytes=VMEM_LIMIT),
        name="ffn1",
    )(x2d, pre_g, post_g, wg, wu, wd)


def _rope(c, cos, sin_signed, half):
    lane = lax.broadcasted_iota(jnp.int32, c.shape, 1)
    first = (lane % (2 * half)) < half
    partner = jnp.where(first, pltpu.roll(c, LANES - half, 1), pltpu.roll(c, half, 1))
    return c * cos + partner * sin_signed


def _head_rms(c, seg_ref, g):
    sq = c * c
    hi = sq.astype(BF16)
    lo = (sq - hi.astype(F32)).astype(BF16)
    ms = _dot(hi, seg_ref[...]) + _dot(lo, seg_ref[...])
    return c * lax.rsqrt(ms + EPS) * g


def _dup_halves(c):
    lane = lax.broadcasted_iota(jnp.int32, c.shape, 1)
    low = lane < HEAD_DIM
    r = pltpu.roll(c, HEAD_DIM, 1)
    return jnp.where(low, c, r), jnp.where(low, r, c)


def _qkv_kernel(x_ref, pre_ref, w_ref, seg_ref, qg_ref, kg_ref,
                cosa_ref, sina_ref, cosb_ref, sinb_ref,
                qa_ref, ka_ref, va_ref, *rest):
    b_refs, (qf_sc, kf_sc, vf_sc) = rest[:-3], rest[-3:]
    h = _rms(x_ref[...], pre_ref[...]).astype(BF16)
    cosa, sina = cosa_ref[...], sina_ref[...]
    cosb, sinb = cosb_ref[...], sinb_ref[...]
    wide = 2 * LANES

    qa = _dot(h, w_ref[:, 0:A_Q])
    for t in range(A_Q // wide):
        qn = _head_rms(qa[:, t * wide:(t + 1) * wide], seg_ref, qg_ref[...])
        for c in range(wide // LANES):
            q = _rope(qn[:, c * LANES:(c + 1) * LANES], cosa, sina, HEAD_DIM // 4) * (SCALE * LOG2E)
            col = t * wide + c * LANES
            qa_ref[:, col:col + LANES] = q.astype(BF16)

    kv = _dot(h, w_ref[:, A_Q:A_Q + 2 * A_KV])
    k = _rope(_head_rms(kv, seg_ref, kg_ref[...])[:, 0:A_KV], cosa, sina, HEAD_DIM // 4)
    k0, k1 = _dup_halves(k)
    ka_ref[0] = k0.astype(BF16)
    ka_ref[1] = k1.astype(BF16)
    vt = kv[:, A_KV:].T.astype(BF16)
    ones = jnp.ones((VT_ROWS - HEAD_DIM, TM), BF16)
    for g in range(N_KV_A):
        va_ref[g, 0:HEAD_DIM, :] = vt[g * HEAD_DIM:(g + 1) * HEAD_DIM, :]
        va_ref[g, HEAD_DIM:, :] = ones

    base = A_Q + 2 * A_KV
    qb = _dot(h, w_ref[:, base:base + B_QKV])
    kb = _dot(h, w_ref[:, base + B_QKV:base + 2 * B_QKV])
    vb = _dot(h, w_ref[:, base + 2 * B_QKV:base + 3 * B_QKV])
    for c in range(B_QKV // LANES):
        sl = slice(c * LANES, (c + 1) * LANES)
        qf_sc[c] = _rope(qb[:, sl], cosb, sinb, HEAD_DIM // 2) * (SCALE * LOG2E)
        kf_sc[c] = _rope(kb[:, sl], cosb, sinb, HEAD_DIM // 2)
        vf_sc[c] = vb[:, sl]
    for j, src in enumerate((qf_sc, kf_sc, vf_sc)):
        for i, d in enumerate(DILATIONS):
            out = b_refs[3 * i + j]
            for r in range(d):
                rows = slice(None) if d == 1 else pl.ds(r, TM // d, stride=d)
                for c in range(B_QKV // LANES):
                    out[r, :, c * LANES:(c + 1) * LANES] = src[c, rows, :].astype(BF16)


def _qkv_call(x1, pre_g, w, seg, qg, kg, cosa, sina, cosb, sinb):
    B, S, _ = x1.shape
    tab = pl.BlockSpec((TM, LANES), lambda i, b: (i, 0))
    wide = pl.BlockSpec((None, TM, A_Q), lambda i, b: (b, i, 0))
    dup = pl.BlockSpec((None, N_KV_A, TM, LANES), lambda i, b: (b, 0, i, 0))
    per = TK_A // TM
    vt = pl.BlockSpec((None, N_KV_A, None, VT_ROWS, TM), lambda i, b: (b, 0, i // per, 0, i % per))
    wide_shape = jax.ShapeDtypeStruct((B, S, A_Q), BF16)
    dup_shape = jax.ShapeDtypeStruct((B, N_KV_A, S, LANES), BF16)
    vt_shape = jax.ShapeDtypeStruct((B, N_KV_A, S // TK_A, VT_ROWS, TK_A), BF16)
    b_specs, b_shapes = [], []
    for d in DILATIONS:
        b_specs += [pl.BlockSpec((None, d, TM // d, B_QKV), lambda i, b: (b, 0, i, 0))] * 3
        b_shapes += [jax.ShapeDtypeStruct((B, d, S // d, B_QKV), BF16)] * 3
    outs = pl.pallas_call(
        _qkv_kernel,
        grid=(S // TM, B),
        in_specs=[pl.BlockSpec((None, TM, D_MODEL), lambda i, b: (b, i, 0)),
                  _resident((1, D_MODEL)), _resident(w.shape), _resident(seg.shape),
                  _resident(qg.shape), _resident(kg.shape), tab, tab, tab, tab],
        out_specs=[wide, dup, vt] + b_specs,
        out_shape=[wide_shape, dup_shape, vt_shape] + b_shapes,
        scratch_shapes=[pltpu.VMEM((B_QKV // LANES, TM, LANES), F32)] * 3,
        compiler_params=pltpu.CompilerParams(
            dimension_semantics=("arbitrary", "arbitrary"), vmem_limit_bytes=VMEM_LIMIT),
        name="qkv",
    )(x1, pre_g, w, seg, qg, kg, cosa, sina, cosb, sinb)
    qa, ka, vta = outs[:3]
    qkv_b = [tuple(outs[3 + 3 * i:6 + 3 * i]) for i in range(len(DILATIONS))]
    return qa, ka, vta, qkv_b


def _attn_a_kernel(q_ref, k_ref, vt_ref, o_ref, qs_sc, m_sc, acc_sc, sa_sc, sb_sc, cma_sc, cmb_sc):
    tq = q_ref.shape[0]
    n_pairs = qs_sc.shape[0]
    low = lax.broadcasted_iota(jnp.int32, (tq, LANES), 1) < HEAD_DIM
    for c in range(n_pairs):
        qc = q_ref[:, c * LANES:(c + 1) * LANES]
        zero = jnp.zeros_like(qc)
        qs_sc[c, 0:tq, :] = jnp.where(low, qc, zero)
        qs_sc[c, tq:, :] = jnp.where(low, zero, qc)

    m_sc[...] = jnp.full(m_sc.shape, NEG_INF, F32)
    acc_sc[...] = jnp.zeros(acc_sc.shape, F32)

    n_k = vt_ref.shape[0]

    def scores(kb, s_out, cm_out):
        k = k_ref[pl.ds(pl.multiple_of(kb * TK_A, TK_A), TK_A), :]
        for c in range(n_pairs):
            s = _dot_nt(k, qs_sc[c])
            s_out[c] = s
            cm_out[c] = jnp.max(s, axis=0, keepdims=True)

    def consume(kb, s_in, cm_in):
        vt = vt_ref[kb]
        for c in range(n_pairs):
            m_old = m_sc[c]
            m_new = jnp.maximum(m_old, cm_in[c])
            alpha = jnp.exp2(m_old - m_new)
            p = jnp.exp2(s_in[c] - m_new).astype(BF16)
            acc_sc[c] = alpha * acc_sc[c] + _dot(vt, p)
            m_sc[c] = m_new

    bufs = ((sa_sc, cma_sc), (sb_sc, cmb_sc))
    scores(0, *bufs[0])

    def fused(kb, cur, nxt):
        s_in, cm_in = cur
        s_out, cm_out = nxt
        k0 = pl.multiple_of((kb + 1) * TK_A, TK_A)
        m_new, alpha, pv, cm = [], [], [], []
        for c in range(n_pairs):
            m_old = m_sc[c]
            m_new.append(jnp.maximum(m_old, cm_in[c]))
            alpha.append(jnp.exp2(m_old - m_new[c]))
            pv.append(None)
            cm.append(None)
        for j in range(TK_A // SUB_A):
            rows = slice(j * SUB_A, (j + 1) * SUB_A)
            k = k_ref[pl.ds(k0 + j * SUB_A, SUB_A), :]
            vt = vt_ref[kb, :, rows]
            for c in range(n_pairs):
                s = _dot_nt(k, qs_sc[c])
                s_out[c, rows, :] = s
                part = jnp.max(s, axis=0, keepdims=True)
                cm[c] = part if cm[c] is None else jnp.maximum(cm[c], part)
            for c in range(n_pairs):
                p = jnp.exp2(s_in[c, rows, :] - m_new[c]).astype(BF16)
                part = _dot(vt, p)
                pv[c] = part if pv[c] is None else pv[c] + part
        for c in range(n_pairs):
            cm_out[c] = cm[c]
            acc_sc[c] = alpha[c] * acc_sc[c] + pv[c]
            m_sc[c] = m_new[c]

    def body(kb, carry):
        for parity in range(2):
            @pl.when(kb % 2 == parity)
            def _():
                fused(kb, bufs[parity], bufs[1 - parity])
        return carry

    lax.fori_loop(0, n_k - 1, body, 0)
    consume(n_k - 1, *bufs[(n_k - 1) % 2])
    for c in range(n_pairs):
        o = acc_sc[c, 0:HEAD_DIM, :] / acc_sc[c, HEAD_DIM:HEAD_DIM + 1, :]
        pair = jnp.concatenate([o[:, 0:tq], o[:, tq:]], axis=0)
        o_ref[:, c * LANES:(c + 1) * LANES] = pair.T.astype(o_ref.dtype)


def _attn_a_bounded_kernel(q_ref, k_ref, vt_ref, o_ref, qt_sc, acc_sc):
    n_pairs = q_ref.shape[1] // LANES
    units = [(c, h) for h in range(q_ref.shape[0] // QH_A) for c in range(n_pairs)]
    low = lax.broadcasted_iota(jnp.int32, (QH_A, LANES), 1) < HEAD_DIM
    for u, (c, h) in enumerate(units):
        qc = q_ref[h * QH_A:(h + 1) * QH_A, c * LANES:(c + 1) * LANES]
        zero = jnp.zeros_like(qc)
        qs = jnp.concatenate([jnp.where(low, qc, zero), jnp.where(low, zero, qc)], axis=0)
        qt_sc[u] = qs.T
    n_sub = TK_A // SUB_A
    stream = [(t, u) for t in range(vt_ref.shape[0] * n_sub) for u in range(len(units))]

    def scores(i):
        t, u = stream[i]
        return _dot(k_ref[t * SUB_A:(t + 1) * SUB_A, :], qt_sc[u])

    depth = n_pairs
    ahead = [scores(i) for i in range(depth)]
    pv = [None] * len(units)
    den = [None] * len(units)
    for i, (t, u) in enumerate(stream):
        if i + depth < len(stream):
            ahead.append(scores(i + depth))
        cols = slice((t % n_sub) * SUB_A, (t % n_sub + 1) * SUB_A)
        p = jnp.exp2(ahead.pop(0))
        part = _dot(vt_ref[t // n_sub, 0:HEAD_DIM, cols], p.astype(BF16))
        pv[u] = part if pv[u] is None else pv[u] + part
        rows8 = jnp.sum(p.reshape(SUB_A // 8, 8, p.shape[1]), axis=0)
        den[u] = rows8 if den[u] is None else den[u] + rows8
    for u in range(len(units)):
        acc_sc[u, 0:HEAD_DIM, :] = pv[u]
        acc_sc[u, HEAD_DIM:HEAD_DIM + 8, :] = den[u]
    for u, (c, h) in enumerate(units):
        l = jnp.sum(acc_sc[u, HEAD_DIM:HEAD_DIM + 8, :], axis=0, keepdims=True)
        o = acc_sc[u, 0:HEAD_DIM, :] / l
        pair = jnp.concatenate([o[:, 0:QH_A], o[:, QH_A:]], axis=0)
        o_ref[h * QH_A:(h + 1) * QH_A, c * LANES:(c + 1) * LANES] = pair.T.astype(o_ref.dtype)


def _attn_a_bounded_call(qa, ka, vta):
    B, S, _ = qa.shape
    qcols = A_Q // N_KV_A
    n_units = (qcols // LANES) * (TQ_BOUNDED // QH_A)
    qspec = pl.BlockSpec((None, TQ_BOUNDED, qcols), lambda b, g, i: (b, i, g))
    kspec = pl.BlockSpec((None, None, S, LANES), lambda b, g, i: (b, g, 0, 0))
    vspec = pl.BlockSpec((None, None, S // TK_A, VT_ROWS, TK_A), lambda b, g, i: (b, g, 0, 0, 0))
    return pl.pallas_call(
        _attn_a_bounded_kernel,
        grid=(B, N_KV_A, S // TQ_BOUNDED),
        in_specs=[qspec, kspec, vspec],
        out_specs=qspec,
        out_shape=jax.ShapeDtypeStruct((B, S, A_Q), BF16),
        scratch_shapes=[pltpu.VMEM((n_units, LANES, 2 * QH_A), BF16),
                        pltpu.VMEM((n_units, VT_ROWS, 2 * QH_A), F32)],
        compiler_params=pltpu.CompilerParams(
            dimension_semantics=("arbitrary", "arbitrary", "arbitrary"),
            vmem_limit_bytes=VMEM_LIMIT),
        name="attn_a_bounded",
    )(qa, ka, vta)


def _attn_a_call(qa, ka, vta):
    B, S, _ = qa.shape
    qcols = A_Q // N_KV_A
    pairs = N_HEADS_A // N_KV_A // 2
    qspec = pl.BlockSpec((None, TQ_A, qcols), lambda b, g, i: (b, i, g))
    kspec = pl.BlockSpec((None, None, S, LANES), lambda b, g, i: (b, g, 0, 0))
    vspec = pl.BlockSpec((None, None, S // TK_A, VT_ROWS, TK_A), lambda b, g, i: (b, g, 0, 0, 0))
    return pl.pallas_call(
        _attn_a_kernel,
        grid=(B, N_KV_A, S // TQ_A),
        in_specs=[qspec, kspec, vspec],
        out_specs=qspec,
        out_shape=jax.ShapeDtypeStruct((B, S, A_Q), BF16),
        scratch_shapes=[pltpu.VMEM((pairs, 2 * TQ_A, LANES), BF16),
                        pltpu.VMEM((pairs, 1, 2 * TQ_A), F32),
                        pltpu.VMEM((pairs, VT_ROWS, 2 * TQ_A), F32),
                        pltpu.VMEM((pairs, TK_A, 2 * TQ_A), F32),
                        pltpu.VMEM((pairs, TK_A, 2 * TQ_A), F32),
                        pltpu.VMEM((pairs, 1, 2 * TQ_A), F32),
                        pltpu.VMEM((pairs, 1, 2 * TQ_A), F32)],
        compiler_params=pltpu.CompilerParams(
            dimension_semantics=("arbitrary", "arbitrary", "arbitrary"),
            vmem_limit_bytes=VMEM_LIMIT),
        name="attn_a",
    )(qa, ka, vta)


def _attn_b_kernel(q_ref, kp_ref, kc_ref, kn_ref, vp_ref, vc_ref, vn_ref,
                   o_ref, st_ref, kbuf, vbuf, *, seq_len):
    j = pl.program_id(2)
    tq = q_ref.shape[0]
    kbuf[0:SPAN_B, :] = kp_ref[...]
    kbuf[SPAN_B:SPAN_B + tq, :] = kc_ref[...]
    kbuf[SPAN_B + tq:, :] = kn_ref[...]
    vbuf[0:SPAN_B, :] = vp_ref[...]
    vbuf[SPAN_B:SPAN_B + tq, :] = vc_ref[...]
    vbuf[SPAN_B + tq:, :] = vn_ref[...]

    n_keys = SUB_B + 2 * SPAN_B
    c_idx = lax.broadcasted_iota(jnp.int32, (n_keys, 2 * SUB_B), 0)
    a_idx = lax.broadcasted_iota(jnp.int32, (n_keys, 2 * SUB_B), 1) % SUB_B
    band_bias = jnp.where(jnp.abs(c_idx - SPAN_B - a_idx) <= SPAN_B, 0.0, NEG_INF).astype(F32)
    low = lax.broadcasted_iota(jnp.int32, (SUB_B, LANES), 1) < HEAD_DIM
    ones = jnp.ones((VT_ROWS - HEAD_DIM, n_keys), BF16)
    pad = jnp.zeros((LANES - 2 * N_HEADS_B, SUB_B), F32)

    n_sub = tq // SUB_B
    n_pairs = N_HEADS_B // 2
    biases = {}

    def bias_of(i):
        if i not in biases:
            first_key = j * tq + i * SUB_B - SPAN_B
            bias = band_bias
            if i == 0:
                bias = jnp.where(c_idx >= -first_key, bias, NEG_INF)
            if i == n_sub - 1:
                bias = jnp.where(c_idx < seq_len - first_key, bias, NEG_INF)
            biases[i] = bias
        return biases[i]

    def scores(i, hp):
        q0, cols = i * SUB_B, slice(hp * LANES, (hp + 1) * LANES)
        qp = q_ref[q0:q0 + SUB_B, cols]
        zero = jnp.zeros_like(qp)
        qs = jnp.concatenate([jnp.where(low, qp, zero), jnp.where(low, zero, qp)], axis=0)
        return _dot(kbuf[q0:q0 + n_keys, cols], qs.T) + bias_of(i)

    def finish(i, hp, s):
        q0, cols = i * SUB_B, slice(hp * LANES, (hp + 1) * LANES)
        m = jnp.max(s, axis=0, keepdims=True)
        p = jnp.exp2(s - m).astype(BF16)
        vt = jnp.concatenate([vbuf[q0:q0 + n_keys, cols].T, ones], axis=0)
        o_all = _dot(vt, p)
        o_t = jnp.concatenate([o_all[0:HEAD_DIM, 0:SUB_B],
                               o_all[HEAD_DIM:LANES, SUB_B:]], axis=0)
        o_ref[q0:q0 + SUB_B, cols] = o_t.T.astype(o_ref.dtype)
        l = o_all[LANES:LANES + 1, :]
        return [m[:, 0:SUB_B], m[:, SUB_B:]], [l[:, 0:SUB_B], l[:, SUB_B:]]

    stream = [(i, hp) for i in range(n_sub) for hp in range(n_pairs)]
    depth = 6
    ahead = [scores(*stream[u]) for u in range(depth)]
    ms, ls = [], []
    for u, (i, hp) in enumerate(stream):
        if u + depth < len(stream):
            ahead.append(scores(*stream[u + depth]))
        m2, l2 = finish(i, hp, ahead.pop(0))
        ms += m2
        ls += l2
        if hp == n_pairs - 1:
            st_t = jnp.concatenate(ms + ls + [pad], axis=0)
            st_ref[i * SUB_B:(i + 1) * SUB_B, :] = st_t.T
            ms, ls = [], []


def _attn_b_call(qb, kb, vb):
    B, d, L, C = qb.shape
    nh = L // SPAN_B
    tq = min(TQ_B, L)
    per = tq // SPAN_B
    cur = pl.BlockSpec((None, None, tq, C), lambda b, r, j: (b, r, j, 0))
    prev = pl.BlockSpec((None, None, SPAN_B, C),
                        lambda b, r, j: (b, r, jnp.maximum(j * per - 1, 0), 0))
    nxt = pl.BlockSpec((None, None, SPAN_B, C),
                       lambda b, r, j: (b, r, jnp.minimum((j + 1) * per, nh - 1), 0))
    st_spec = pl.BlockSpec((None, None, tq, LANES), lambda b, r, j: (b, r, j, 0))
    return pl.pallas_call(
        functools.partial(_attn_b_kernel, seq_len=L),
        grid=(B, d, L // tq),
        in_specs=[cur, prev, cur, nxt, prev, cur, nxt],
        out_specs=[cur, st_spec],
        out_shape=[jax.ShapeDtypeStruct((B, d, L, C), BF16),
                   jax.ShapeDtypeStruct((B, d, L, LANES), F32)],
        scratch_shapes=[pltpu.VMEM((tq + 2 * SPAN_B, C), BF16),
                        pltpu.VMEM((tq + 2 * SPAN_B, C), BF16)],
        compiler_params=pltpu.CompilerParams(
            dimension_semantics=("arbitrary", "arbitrary", "arbitrary"),
            vmem_limit_bytes=VMEM_LIMIT),
        name=f"attn_b_d{d}",
    )(qb, kb, kb, kb, vb, vb, vb)


def _natural_order(ref, scratch):
    d, _, cols = ref.shape
    if d == 1:
        return ref[0].astype(F32)
    for r in range(d):
        for c in range(cols // LANES):
            scratch[c, pl.ds(r, TM // d, stride=d), :] = ref[r, :, c * LANES:(c + 1) * LANES].astype(F32)
    return jnp.concatenate([scratch[c] for c in range(cols // LANES)], axis=1)


def _out_ffn_kernel(x_ref, ha_ref, o1_ref, o2_ref, o3_ref, s1_ref, s2_ref, s3_ref, ex_ref,
                    wo_ref, mixg_ref, pre_ref, post_ref, wg_ref, wu_ref, wd_ref, y_ref,
                    of2_sc, of3_sc, sf2_sc, sf3_sc):
    stats = (_natural_order(s1_ref, None), _natural_order(s2_ref, sf2_sc),
             _natural_order(s3_ref, sf3_sc))
    parts = (_natural_order(o1_ref, None), _natural_order(o2_ref, of2_sc),
             _natural_order(o3_ref, of3_sc))
    is_max = lax.broadcasted_iota(jnp.int32, (TM, LANES), 1) < N_HEADS_B
    m_all = jnp.maximum(jnp.maximum(stats[0], stats[1]), stats[2])
    es = [jnp.exp2(s - m_all) for s in stats]
    den = None
    for e, s in zip(es, stats):
        term = e * pltpu.roll(s, LANES - N_HEADS_B, 1)
        den = term if den is None else den + term
    heads_b = None
    for e, o in zip(es, parts):
        w = jnp.where(is_max, e / den, 0.0)
        hi = w.astype(BF16)
        lo = (w - hi.astype(F32)).astype(BF16)
        term = (_dot(hi, ex_ref[...]) + _dot(lo, ex_ref[...])) * o
        heads_b = term if heads_b is None else heads_b + term

    mixed = _dot(ha_ref[...], wo_ref[0:A_Q, :]) + _dot(heads_b.astype(BF16), wo_ref[A_Q:, :])
    x2 = x_ref[...] + _rms(mixed, mixg_ref[...])
    y_ref[...] = _swiglu_half_step(x2, pre_ref[...], post_ref[...], wg_ref, wu_ref, wd_ref)


def _out_ffn_call(x1, heads_a, parts, expand, wo, mix_g, pre_g, post_g, wg, wu, wd):
    n = x1.shape[0]
    per_batch = parts[0][0].shape[2] // TM
    row = lambda c: pl.BlockSpec((TM, c), lambda i: (i, 0))

    def strided(a):
        _, d, _, c = a.shape
        return pl.BlockSpec((None, d, TM // d, c), lambda i: (i // per_batch, 0, i % per_batch, 0))

    (o1, s1), (o2, s2), (o3, s3) = parts
    return pl.pallas_call(
        _out_ffn_kernel,
        grid=(n // TM,),
        in_specs=[row(D_MODEL), row(A_Q), strided(o1), strided(o2), strided(o3),
                  strided(s1), strided(s2), strided(s3), _resident(expand.shape),
                  _resident(wo.shape), _resident((1, D_MODEL)), _resident((1, D_MODEL)),
                  _resident((1, D_MODEL)), _resident(wg.shape), _resident(wu.shape),
                  _resident(wd.shape)],
        out_specs=row(D_MODEL),
        out_shape=jax.ShapeDtypeStruct((n, D_MODEL), F32),
        scratch_shapes=[pltpu.VMEM((B_QKV // LANES, TM, LANES), F32),
                        pltpu.VMEM((B_QKV // LANES, TM, LANES), F32),
                        pltpu.VMEM((1, TM, LANES), F32), pltpu.VMEM((1, TM, LANES), F32)],
        compiler_params=pltpu.CompilerParams(
            dimension_semantics=("arbitrary",), vmem_limit_bytes=VMEM_LIMIT),
        name="out_ffn2",
    )(x1, heads_a, o1, o2, o3, s1, s2, s3, expand, wo, mix_g, pre_g, post_g, wg, wu, wd)


def _rope_tables(seq):
    pos = np.arange(seq)
    row, col = (pos // GRID_W)[:, None], (pos % GRID_W)[:, None]
    dim_a = HEAD_DIM // 2
    fa = ROPE_THETA ** (-np.arange(0, dim_a, 2, dtype=np.float64) / dim_a)
    fb = ROPE_THETA ** (-np.arange(0, HEAD_DIM, 2, dtype=np.float64) / HEAD_DIM)
    ar, ac, ab = row * fa[None, :], col * fa[None, :], pos[:, None] * fb[None, :]
    cos_a = np.concatenate([np.cos(ar), np.cos(ar), np.cos(ac), np.cos(ac)], axis=-1)
    sin_a = np.concatenate([-np.sin(ar), np.sin(ar), -np.sin(ac), np.sin(ac)], axis=-1)
    cos_b = np.concatenate([np.cos(ab), np.cos(ab)], axis=-1)
    sin_b = np.concatenate([-np.sin(ab), np.sin(ab)], axis=-1)
    two = lambda t: jnp.asarray(np.tile(t, (1, LANES // HEAD_DIM)).astype(np.float32))
    return two(cos_a), two(sin_a), two(cos_b), two(sin_b)


def _layer(x, ffn1_pre_g, ffn1_post_g, ffn1_w_gate, ffn1_w_up, ffn1_w_down,
           mix_pre_g, mix_post_g, w_qkv, q_norm_g, k_norm_g, w_out,
           ffn2_pre_g, ffn2_post_g, ffn2_w_gate, ffn2_w_up, ffn2_w_down, tables, seg, expand):
    B, S, D = x.shape
    vec = lambda g: g.reshape(1, -1).astype(F32)
    bf = lambda w: w.astype(BF16)

    x1 = _ffn_call(x.reshape(B * S, D), vec(ffn1_pre_g), vec(ffn1_post_g),
                   bf(ffn1_w_gate), bf(ffn1_w_up), bf(ffn1_w_down))
    heads_per_tile = 2 * LANES // HEAD_DIM
    kv_gain = jnp.concatenate([jnp.tile(vec(k_norm_g), (1, N_KV_A)), jnp.ones((1, A_KV), F32)], axis=1)
    qa, ka, vta, qkv_b = _qkv_call(x1.reshape(B, S, D), vec(mix_pre_g), bf(w_qkv), seg,
                                   jnp.tile(vec(q_norm_g), (1, heads_per_tile)), kv_gain, *tables)
    score_bound = ((HEAD_DIM * SCALE * LOG2E) * jnp.max(jnp.abs(q_norm_g))
                   * jnp.max(jnp.abs(k_norm_g)))
    heads_a = lax.cond(score_bound <= MAX_SCORE_BOUND,
                       lambda: _attn_a_bounded_call(qa, ka, vta),
                       lambda: _attn_a_call(qa, ka, vta))
    parts = [_attn_b_call(*qkv) for qkv in qkv_b]
    y = _out_ffn_call(x1, heads_a.reshape(B * S, A_Q), parts, expand,
                      bf(w_out), vec(mix_post_g), vec(ffn2_pre_g), vec(ffn2_post_g),
                      bf(ffn2_w_gate), bf(ffn2_w_up), bf(ffn2_w_down))
    return y.reshape(B, S, D)


def kernel(x, ffn1_pre_g, ffn1_post_g, ffn1_w_gate, ffn1_w_up, ffn1_w_down, mix_pre_g, mix_post_g, w_qkv, q_norm_g, k_norm_g, w_out, ffn2_pre_g, ffn2_post_g, ffn2_w_gate, ffn2_w_up, ffn2_w_down):
    assert all(w // 2 // d == SPAN_B for w, d in DILATED_CONFIGS) and TK_A % TM == 0
    S = x.shape[1]
    tables = _rope_tables(S)
    head_of_lane = jnp.arange(2 * LANES) // HEAD_DIM
    seg = (head_of_lane[:, None] == head_of_lane[None, :]).astype(BF16) / HEAD_DIM
    expand = (jnp.arange(LANES)[:, None] == jnp.arange(B_QKV)[None, :] // HEAD_DIM).astype(BF16)
    params = (ffn1_pre_g, ffn1_post_g, ffn1_w_gate, ffn1_w_up, ffn1_w_down, mix_pre_g, mix_post_g,
              w_qkv, q_norm_g, k_norm_g, w_out, ffn2_pre_g, ffn2_post_g, ffn2_w_gate, ffn2_w_up,
              ffn2_w_down)
    for l in range(ffn1_pre_g.shape[0]):
        x = _layer(x, *(p[l] for p in params), tables, seg, expand)
    return x
```

```python
import functools

import jax
import jax.numpy as jnp
import numpy as np
from jax import lax
from jax.experimental import pallas as pl
from jax.experimental.pallas import tpu as pltpu

D_MODEL = 1024
HEAD_DIM = 64
N_HEADS_A = 8
N_KV_A = 2
N_HEADS_B = 8
DILATED_CONFIGS = ((128, 1), (512, 4), (2048, 16))
DILATIONS = tuple(d for _, d in DILATED_CONFIGS)
GRID_W = 64
ROPE_THETA = 10000.0
D_FF = 2816
EPS = 1e-6
NEG_INF = -1e30

A_Q = N_HEADS_A * HEAD_DIM
A_KV = N_KV_A * HEAD_DIM
B_QKV = N_HEADS_B * HEAD_DIM
QKV_COLS = A_Q + 2 * A_KV + 3 * B_QKV
SCALE = HEAD_DIM ** -0.5
LOG2E = 1.4426950408889634

LANES = 128
SUBLANES = 8
FF_CHUNKS = ((0, 1024), (1024, 2048), (2048, 2816))
TM = 512
TQ_A = 256
TK_A = 1024
TQ_BOUNDED = 512
QH_A = 256
SUB_A = 256
MAX_SCORE_BOUND = 48.0
VT_ROWS = HEAD_DIM + 16
TQ_B = 1024
SUB_B = 128
SPAN_B = 64
VMEM_LIMIT = 52 * 1024 * 1024

BF16 = jnp.bfloat16
F32 = jnp.float32


def _dot(a, b):
    return jnp.dot(a, b, preferred_element_type=F32)


def _dot_nt(a, b):
    return lax.dot_general(a, b, (((1,), (1,)), ((), ())), preferred_element_type=F32)


def _rms(x, g):
    ms = jnp.mean(x * x, axis=-1, keepdims=True)
    return x * lax.rsqrt(ms + EPS) * g


def _swiglu_half_step(x, pre_g, post_g, wg_ref, wu_ref, wd_ref):
    h = _rms(x, pre_g).astype(BF16)
    f = None
    for lo, hi in FF_CHUNKS:
        g = _dot(h, wg_ref[:, lo:hi])
        u = _dot(h, wu_ref[:, lo:hi])
        a = (g / (1.0 + jnp.exp(-g)) * u).astype(BF16)
        part = _dot(a, wd_ref[lo:hi, :])
        f = part if f is None else f + part
    return x + 0.5 * _rms(f, post_g)


def _ffn_kernel(x_ref, pre_ref, post_ref, wg_ref, wu_ref, wd_ref, o_ref):
    o_ref[...] = _swiglu_half_step(x_ref[...], pre_ref[...], post_ref[...],
                                   wg_ref, wu_ref, wd_ref)


def _resident(shape):
    nd = len(shape)
    return pl.BlockSpec(shape, lambda *_: (0,) * nd, pipeline_mode=pl.Buffered(1))


def _ffn_call(x2d, pre_g, post_g, wg, wu, wd):
    n = x2d.shape[0]
    row = pl.BlockSpec((TM, D_MODEL), lambda i: (i, 0))
    return pl.pallas_call(
        _ffn_kernel,
        grid=(n // TM,),
        in_specs=[row, _resident((1, D_MODEL)), _resident((1, D_MODEL)),
                  _resident(wg.shape), _resident(wu.shape), _resident(wd.shape)],
        out_specs=row,
        out_shape=jax.ShapeDtypeStruct(x2d.shape, F32),
        compiler_params=pltpu.CompilerParams(
            dimension_semantics=("arbitrary",), vmem_limit_bytes=VMEM_LIMIT),
        name="ffn1",
    )(x2d, pre_g, post_g, wg, wu, wd)


def _rope(c, cos, sin_signed, half):
    lane = lax.broadcasted_iota(jnp.int32, c.shape, 1)
    first = (lane % (2 * half)) < half
    partner = jnp.where(first, pltpu.roll(c, LANES - half, 1), pltpu.roll(c, half, 1))
    return c * cos + partner * sin_signed


def _head_rms(c, seg_ref, g):
    sq = c * c
    hi = sq.astype(BF16)
    lo = (sq - hi.astype(F32)).astype(BF16)
    ms = _dot(hi, seg_ref[...]) + _dot(lo, seg_ref[...])
    return c * lax.rsqrt(ms + EPS) * g


def _dup_halves(c):
    lane = lax.broadcasted_iota(jnp.int32, c.shape, 1)
    low = lane < HEAD_DIM
    r = pltpu.roll(c, HEAD_DIM, 1)
    return jnp.where(low, c, r), jnp.where(low, r, c)


def _qkv_kernel(x_ref, pre_ref, w_ref, seg_ref, qg_ref, kg_ref,
                cosa_ref, sina_ref, cosb_ref, sinb_ref,
                qa_ref, ka_ref, va_ref, *rest):
    b_refs, (qf_sc, kf_sc, vf_sc) = rest[:-3], rest[-3:]
    h = _rms(x_ref[...], pre_ref[...]).astype(BF16)
    cosa, sina = cosa_ref[...], sina_ref[...]
    cosb, sinb = cosb_ref[...], sinb_ref[...]
    wide = 2 * LANES

    qa = _dot(h, w_ref[:, 0:A_Q])
    for t in range(A_Q // wide):
        qn = _head_rms(qa[:, t * wide:(t + 1) * wide], seg_ref, qg_ref[...])
        for c in range(wide // LANES):
            q = _rope(qn[:, c * LANES:(c + 1) * LANES], cosa, sina, HEAD_DIM // 4) * (SCALE * LOG2E)
            col = t * wide + c * LANES
            qa_ref[:, col:col + LANES] = q.astype(BF16)

    kv = _dot(h, w_ref[:, A_Q:A_Q + 2 * A_KV])
    k = _rope(_head_rms(kv, seg_ref, kg_ref[...])[:, 0:A_KV], cosa, sina, HEAD_DIM // 4)
    k0, k1 = _dup_halves(k)
    ka_ref[0] = k0.astype(BF16)
    ka_ref[1] = k1.astype(BF16)
    vt = kv[:, A_KV:].T.astype(BF16)
    ones = jnp.ones((VT_ROWS - HEAD_DIM, TM), BF16)
    for g in range(N_KV_A):
        va_ref[g, 0:HEAD_DIM, :] = vt[g * HEAD_DIM:(g + 1) * HEAD_DIM, :]
        va_ref[g, HEAD_DIM:, :] = ones

    base = A_Q + 2 * A_KV
    qb = _dot(h, w_ref[:, base:base + B_QKV])
    kb = _dot(h, w_ref[:, base + B_QKV:base + 2 * B_QKV])
    vb = _dot(h, w_ref[:, base + 2 * B_QKV:base + 3 * B_QKV])
    for c in range(B_QKV // LANES):
        sl = slice(c * LANES, (c + 1) * LANES)
        qf_sc[c] = _rope(qb[:, sl], cosb, sinb, HEAD_DIM // 2) * (SCALE * LOG2E)
        kf_sc[c] = _rope(kb[:, sl], cosb, sinb, HEAD_DIM // 2)
        vf_sc[c] = vb[:, sl]
    for j, src in enumerate((qf_sc, kf_sc, vf_sc)):
        for i, d in enumerate(DILATIONS):
            out = b_refs[3 * i + j]
            for r in range(d):
                rows = slice(None) if d == 1 else pl.ds(r, TM // d, stride=d)
                for c in range(B_QKV // LANES):
                    out[r, :, c * LANES:(c + 1) * LANES] = src[c, rows, :].astype(BF16)


def _qkv_call(x1, pre_g, w, seg, qg, kg, cosa, sina, cosb, sinb):
    B, S, _ = x1.shape
    tab = pl.BlockSpec((TM, LANES), lambda i, b: (i, 0))
    wide = pl.BlockSpec((None, TM, A_Q), lambda i, b: (b, i, 0))
    dup = pl.BlockSpec((None, N_KV_A, TM, LANES), lambda i, b: (b, 0, i, 0))
    per = TK_A // TM
    vt = pl.BlockSpec((None, N_KV_A, None, VT_ROWS, TM), lambda i, b: (b, 0, i // per, 0, i % per))
    wide_shape = jax.ShapeDtypeStruct((B, S, A_Q), BF16)
    dup_shape = jax.ShapeDtypeStruct((B, N_KV_A, S, LANES), BF16)
    vt_shape = jax.ShapeDtypeStruct((B, N_KV_A, S // TK_A, VT_ROWS, TK_A), BF16)
    b_specs, b_shapes = [], []
    for d in DILATIONS:
        b_specs += [pl.BlockSpec((None, d, TM // d, B_QKV), lambda i, b: (b, 0, i, 0))] * 3
        b_shapes += [jax.ShapeDtypeStruct((B, d, S // d, B_QKV), BF16)] * 3
    outs = pl.pallas_call(
        _qkv_kernel,
        grid=(S // TM, B),
        in_specs=[pl.BlockSpec((None, TM, D_MODEL), lambda i, b: (b, i, 0)),
                  _resident((1, D_MODEL)), _resident(w.shape), _resident(seg.shape),
                  _resident(qg.shape), _resident(kg.shape), tab, tab, tab, tab],
        out_specs=[wide, dup, vt] + b_specs,
        out_shape=[wide_shape, dup_shape, vt_shape] + b_shapes,
        scratch_shapes=[pltpu.VMEM((B_QKV // LANES, TM, LANES), F32)] * 3,
        compiler_params=pltpu.CompilerParams(
            dimension_semantics=("arbitrary", "arbitrary"), vmem_limit_bytes=VMEM_LIMIT),
        name="qkv",
    )(x1, pre_g, w, seg, qg, kg, cosa, sina, cosb, sinb)
    qa, ka, vta = outs[:3]
    qkv_b = [tuple(outs[3 + 3 * i:6 + 3 * i]) for i in range(len(DILATIONS))]
    return qa, ka, vta, qkv_b


def _attn_a_kernel(q_ref, k_ref, vt_ref, o_ref, qs_sc, m_sc, acc_sc, sa_sc, sb_sc, cma_sc, cmb_sc):
    tq = q_ref.shape[0]
    n_pairs = qs_sc.shape[0]
    low = lax.broadcasted_iota(jnp.int32, (tq, LANES), 1) < HEAD_DIM
    for c in range(n_pairs):
        qc = q_ref[:, c * LANES:(c + 1) * LANES]
        zero = jnp.zeros_like(qc)
        qs_sc[c, 0:tq, :] = jnp.where(low, qc, zero)
        qs_sc[c, tq:, :] = jnp.where(low, zero, qc)

    m_sc[...] = jnp.full(m_sc.shape, NEG_INF, F32)
    acc_sc[...] = jnp.zeros(acc_sc.shape, F32)

    n_k = vt_ref.shape[0]

    def scores(kb, s_out, cm_out):
        k = k_ref[pl.ds(pl.multiple_of(kb * TK_A, TK_A), TK_A), :]
        for c in range(n_pairs):
            s = _dot_nt(k, qs_sc[c])
            s_out[c] = s
            cm_out[c] = jnp.max(s, axis=0, keepdims=True)

    def consume(kb, s_in, cm_in):
        vt = vt_ref[kb]
        for c in range(n_pairs):
            m_old = m_sc[c]
            m_new = jnp.maximum(m_old, cm_in[c])
            alpha = jnp.exp2(m_old - m_new)
            p = jnp.exp2(s_in[c] - m_new).astype(BF16)
            acc_sc[c] = alpha * acc_sc[c] + _dot(vt, p)
            m_sc[c] = m_new

    bufs = ((sa_sc, cma_sc), (sb_sc, cmb_sc))
    scores(0, *bufs[0])

    def fused(kb, cur, nxt):
        s_in, cm_in = cur
        s_out, cm_out = nxt
        k0 = pl.multiple_of((kb + 1) * TK_A, TK_A)
        m_new, alpha, pv, cm = [], [], [], []
        for c in range(n_pairs):
            m_old = m_sc[c]
            m_new.append(jnp.maximum(m_old, cm_in[c]))
            alpha.append(jnp.exp2(m_old - m_new[c]))
            pv.append(None)
            cm.append(None)
        for j in range(TK_A // SUB_A):
            rows = slice(j * SUB_A, (j + 1) * SUB_A)
            k = k_ref[pl.ds(k0 + j * SUB_A, SUB_A), :]
            vt = vt_ref[kb, :, rows]
            for c in range(n_pairs):
                s = _dot_nt(k, qs_sc[c])
                s_out[c, rows, :] = s
                part = jnp.max(s, axis=0, keepdims=True)
                cm[c] = part if cm[c] is None else jnp.maximum(cm[c], part)
            for c in range(n_pairs):
                p = jnp.exp2(s_in[c, rows, :] - m_new[c]).astype(BF16)
                part = _dot(vt, p)
                pv[c] = part if pv[c] is None else pv[c] + part
        for c in range(n_pairs):
            cm_out[c] = cm[c]
            acc_sc[c] = alpha[c] * acc_sc[c] + pv[c]
            m_sc[c] = m_new[c]

    def body(kb, carry):
        for parity in range(2):
            @pl.when(kb % 2 == parity)
            def _():
                fused(kb, bufs[parity], bufs[1 - parity])
        return carry

    lax.fori_loop(0, n_k - 1, body, 0)
    consume(n_k - 1, *bufs[(n_k - 1) % 2])
    for c in range(n_pairs):
        o = acc_sc[c, 0:HEAD_DIM, :] / acc_sc[c, HEAD_DIM:HEAD_DIM + 1, :]
        pair = jnp.concatenate([o[:, 0:tq], o[:, tq:]], axis=0)
        o_ref[:, c * LANES:(c + 1) * LANES] = pair.T.astype(o_ref.dtype)


def _attn_a_bounded_kernel(q_ref, k_ref, vt_ref, o_ref, qt_sc, acc_sc):
    n_pairs = q_ref.shape[1] // LANES
    units = [(c, h) for h in range(q_ref.shape[0] // QH_A) for c in range(n_pairs)]
    low = lax.broadcasted_iota(jnp.int32, (QH_A, LANES), 1) < HEAD_DIM
    for u, (c, h) in enumerate(units):
        qc = q_ref[h * QH_A:(h + 1) * QH_A, c * LANES:(c + 1) * LANES]
        zero = jnp.zeros_like(qc)
        qs = jnp.concatenate([jnp.where(low, qc, zero), jnp.where(low, zero, qc)], axis=0)
        qt_sc[u] = qs.T
    n_sub = TK_A // SUB_A
    stream = [(t, u) for t in range(vt_ref.shape[0] * n_sub) for u in range(len(units))]

    def scores(i):
        t, u = stream[i]
        return _dot(k_ref[t * SUB_A:(t + 1) * SUB_A, :], qt_sc[u])

    depth = n_pairs
    ahead = [scores(i) for i in range(depth)]
    pv = [None] * len(units)
    den = [None] * len(units)
    for i, (t, u) in enumerate(stream):
        if i + depth < len(stream):
            ahead.append(scores(i + depth))
        cols = slice((t % n_sub) * SUB_A, (t % n_sub + 1) * SUB_A)
        p = jnp.exp2(ahead.pop(0))
        part = _dot(vt_ref[t // n_sub, 0:HEAD_DIM, cols], p.astype(BF16))
        pv[u] = part if pv[u] is None else pv[u] + part
        sums = jnp.sum(p.reshape(SUB_A // SUBLANES, SUBLANES, p.shape[1]), axis=0)
        den[u] = sums if den[u] is None else den[u] + sums
    for u in range(len(units)):
        acc_sc[u, 0:HEAD_DIM, :] = pv[u]
        acc_sc[u, HEAD_DIM:HEAD_DIM + SUBLANES, :] = den[u]
    for u, (c, h) in enumerate(units):
        l = jnp.sum(acc_sc[u, HEAD_DIM:HEAD_DIM + SUBLANES, :], axis=0, keepdims=True)
        o = acc_sc[u, 0:HEAD_DIM, :] / l
        pair = jnp.concatenate([o[:, 0:QH_A], o[:, QH_A:]], axis=0)
        o_ref[h * QH_A:(h + 1) * QH_A, c * LANES:(c + 1) * LANES] = pair.T.astype(o_ref.dtype)


def _attn_a_bounded_call(qa, ka, vta):
    B, S, _ = qa.shape
    qcols = A_Q // N_KV_A
    n_units = (qcols // LANES) * (TQ_BOUNDED // QH_A)
    qspec = pl.BlockSpec((None, TQ_BOUNDED, qcols), lambda b, g, i: (b, i, g))
    kspec = pl.BlockSpec((None, None, S, LANES), lambda b, g, i: (b, g, 0, 0))
    vspec = pl.BlockSpec((None, None, S // TK_A, VT_ROWS, TK_A), lambda b, g, i: (b, g, 0, 0, 0))
    return pl.pallas_call(
        _attn_a_bounded_kernel,
        grid=(B, N_KV_A, S // TQ_BOUNDED),
        in_specs=[qspec, kspec, vspec],
        out_specs=qspec,
        out_shape=jax.ShapeDtypeStruct((B, S, A_Q), BF16),
        scratch_shapes=[pltpu.VMEM((n_units, LANES, 2 * QH_A), BF16),
                        pltpu.VMEM((n_units, VT_ROWS, 2 * QH_A), F32)],
        compiler_params=pltpu.CompilerParams(
            dimension_semantics=("arbitrary", "arbitrary", "arbitrary"),
            vmem_limit_bytes=VMEM_LIMIT),
        name="attn_a_bounded",
    )(qa, ka, vta)


def _attn_a_call(qa, ka, vta):
    B, S, _ = qa.shape
    qcols = A_Q // N_KV_A
    pairs = N_HEADS_A // N_KV_A // 2
    qspec = pl.BlockSpec((None, TQ_A, qcols), lambda b, g, i: (b, i, g))
    kspec = pl.BlockSpec((None, None, S, LANES), lambda b, g, i: (b, g, 0, 0))
    vspec = pl.BlockSpec((None, None, S // TK_A, VT_ROWS, TK_A), lambda b, g, i: (b, g, 0, 0, 0))
    return pl.pallas_call(
        _attn_a_kernel,
        grid=(B, N_KV_A, S // TQ_A),
        in_specs=[qspec, kspec, vspec],
        out_specs=qspec,
        out_shape=jax.ShapeDtypeStruct((B, S, A_Q), BF16),
        scratch_shapes=[pltpu.VMEM((pairs, 2 * TQ_A, LANES), BF16),
                        pltpu.VMEM((pairs, 1, 2 * TQ_A), F32),
                        pltpu.VMEM((pairs, VT_ROWS, 2 * TQ_A), F32),
                        pltpu.VMEM((pairs, TK_A, 2 * TQ_A), F32),
                        pltpu.VMEM((pairs, TK_A, 2 * TQ_A), F32),
                        pltpu.VMEM((pairs, 1, 2 * TQ_A), F32),
                        pltpu.VMEM((pairs, 1, 2 * TQ_A), F32)],
        compiler_params=pltpu.CompilerParams(
            dimension_semantics=("arbitrary", "arbitrary", "arbitrary"),
            vmem_limit_bytes=VMEM_LIMIT),
        name="attn_a",
    )(qa, ka, vta)


def _attn_b_kernel(q_ref, kp_ref, kc_ref, kn_ref, vp_ref, vc_ref, vn_ref,
                   o_ref, st_ref, kbuf, vbuf, *, seq_len):
    j = pl.program_id(2)
    tq = q_ref.shape[0]
    kbuf[0:SPAN_B, :] = kp_ref[...]
    kbuf[SPAN_B:SPAN_B + tq, :] = kc_ref[...]
    kbuf[SPAN_B + tq:, :] = kn_ref[...]
    vbuf[0:SPAN_B, :] = vp_ref[...]
    vbuf[SPAN_B:SPAN_B + tq, :] = vc_ref[...]
    vbuf[SPAN_B + tq:, :] = vn_ref[...]

    n_keys = SUB_B + 2 * SPAN_B
    c_idx = lax.broadcasted_iota(jnp.int32, (n_keys, 2 * SUB_B), 0)
    a_idx = lax.broadcasted_iota(jnp.int32, (n_keys, 2 * SUB_B), 1) % SUB_B
    band_bias = jnp.where(jnp.abs(c_idx - SPAN_B - a_idx) <= SPAN_B, 0.0, NEG_INF).astype(F32)
    low = lax.broadcasted_iota(jnp.int32, (SUB_B, LANES), 1) < HEAD_DIM
    ones = jnp.ones((VT_ROWS - HEAD_DIM, n_keys), BF16)
    pad = jnp.zeros((LANES - 2 * N_HEADS_B, SUB_B), F32)

    n_sub = tq // SUB_B
    n_pairs = N_HEADS_B // 2
    biases = {}

    def bias_of(i):
        if i not in biases:
            first_key = j * tq + i * SUB_B - SPAN_B
            bias = band_bias
            if i == 0:
                bias = jnp.where(c_idx >= -first_key, bias, NEG_INF)
            if i == n_sub - 1:
                bias = jnp.where(c_idx < seq_len - first_key, bias, NEG_INF)
            biases[i] = bias
        return biases[i]

    def scores(i, hp):
        q0, cols = i * SUB_B, slice(hp * LANES, (hp + 1) * LANES)
        qp = q_ref[q0:q0 + SUB_B, cols]
        zero = jnp.zeros_like(qp)
        qs = jnp.concatenate([jnp.where(low, qp, zero), jnp.where(low, zero, qp)], axis=0)
        return _dot(kbuf[q0:q0 + n_keys, cols], qs.T) + bias_of(i)

    def finish(i, hp, s):
        q0, cols = i * SUB_B, slice(hp * LANES, (hp + 1) * LANES)
        m = jnp.max(s, axis=0, keepdims=True)
        p = jnp.exp2(s - m).astype(BF16)
        vt = jnp.concatenate([vbuf[q0:q0 + n_keys, cols].T, ones], axis=0)
        o_all = _dot(vt, p)
        o_t = jnp.concatenate([o_all[0:HEAD_DIM, 0:SUB_B],
                               o_all[HEAD_DIM:LANES, SUB_B:]], axis=0)
        o_ref[q0:q0 + SUB_B, cols] = o_t.T.astype(o_ref.dtype)
        l = o_all[LANES:LANES + 1, :]
        return [m[:, 0:SUB_B], m[:, SUB_B:]], [l[:, 0:SUB_B], l[:, SUB_B:]]

    stream = [(i, hp) for i in range(n_sub) for hp in range(n_pairs)]
    depth = 8
    ahead = [scores(*stream[u]) for u in range(depth)]
    ms, ls = [], []
    for u, (i, hp) in enumerate(stream):
        if u + depth < len(stream):
            ahead.append(scores(*stream[u + depth]))
        m2, l2 = finish(i, hp, ahead.pop(0))
        ms += m2
        ls += l2
        if hp == n_pairs - 1:
            st_t = jnp.concatenate(ms + ls + [pad], axis=0)
            st_ref[i * SUB_B:(i + 1) * SUB_B, :] = st_t.T
            ms, ls = [], []


def _attn_b_call(qb, kb, vb):
    B, d, L, C = qb.shape
    nh = L // SPAN_B
    tq = min(TQ_B, L)
    per = tq // SPAN_B
    cur = pl.BlockSpec((None, None, tq, C), lambda b, r, j: (b, r, j, 0))
    prev = pl.BlockSpec((None, None, SPAN_B, C),
                        lambda b, r, j: (b, r, jnp.maximum(j * per - 1, 0), 0))
    nxt = pl.BlockSpec((None, None, SPAN_B, C),
                       lambda b, r, j: (b, r, jnp.minimum((j + 1) * per, nh - 1), 0))
    st_spec = pl.BlockSpec((None, None, tq, LANES), lambda b, r, j: (b, r, j, 0))
    return pl.pallas_call(
        functools.partial(_attn_b_kernel, seq_len=L),
        grid=(B, d, L // tq),
        in_specs=[cur, prev, cur, nxt, prev, cur, nxt],
        out_specs=[cur, st_spec],
        out_shape=[jax.ShapeDtypeStruct((B, d, L, C), BF16),
                   jax.ShapeDtypeStruct((B, d, L, LANES), F32)],
        scratch_shapes=[pltpu.VMEM((tq + 2 * SPAN_B, C), BF16),
                        pltpu.VMEM((tq + 2 * SPAN_B, C), BF16)],
        compiler_params=pltpu.CompilerParams(
            dimension_semantics=("arbitrary", "arbitrary", "arbitrary"),
            vmem_limit_bytes=VMEM_LIMIT),
        name=f"attn_b_d{d}",
    )(qb, kb, kb, kb, vb, vb, vb)


def _natural_order(ref, scratch):
    d, _, cols = ref.shape
    if d == 1:
        return ref[0].astype(F32)
    for r in range(d):
        for c in range(cols // LANES):
            scratch[c, pl.ds(r, TM // d, stride=d), :] = ref[r, :, c * LANES:(c + 1) * LANES].astype(F32)
    return jnp.concatenate([scratch[c] for c in range(cols // LANES)], axis=1)


def _out_ffn_kernel(x_ref, ha_ref, o1_ref, o2_ref, o3_ref, s1_ref, s2_ref, s3_ref, ex_ref,
                    wo_ref, mixg_ref, pre_ref, post_ref, wg_ref, wu_ref, wd_ref, y_ref,
                    of2_sc, of3_sc, sf2_sc, sf3_sc):
    stats = (_natural_order(s1_ref, None), _natural_order(s2_ref, sf2_sc),
             _natural_order(s3_ref, sf3_sc))
    parts = (_natural_order(o1_ref, None), _natural_order(o2_ref, of2_sc),
             _natural_order(o3_ref, of3_sc))
    is_max = lax.broadcasted_iota(jnp.int32, (TM, LANES), 1) < N_HEADS_B
    m_all = jnp.maximum(jnp.maximum(stats[0], stats[1]), stats[2])
    es = [jnp.exp2(s - m_all) for s in stats]
    den = None
    for e, s in zip(es, stats):
        term = e * pltpu.roll(s, LANES - N_HEADS_B, 1)
        den = term if den is None else den + term
    heads_b = None
    for e, o in zip(es, parts):
        w = jnp.where(is_max, e / den, 0.0)
        hi = w.astype(BF16)
        lo = (w - hi.astype(F32)).astype(BF16)
        term = (_dot(hi, ex_ref[...]) + _dot(lo, ex_ref[...])) * o
        heads_b = term if heads_b is None else heads_b + term

    mixed = _dot(ha_ref[...], wo_ref[0:A_Q, :]) + _dot(heads_b.astype(BF16), wo_ref[A_Q:, :])
    x2 = x_ref[...] + _rms(mixed, mixg_ref[...])
    y_ref[...] = _swiglu_half_step(x2, pre_ref[...], post_ref[...], wg_ref, wu_ref, wd_ref)


def _out_ffn_call(x1, heads_a, parts, expand, wo, mix_g, pre_g, post_g, wg, wu, wd):
    n = x1.shape[0]
    per_batch = parts[0][0].shape[2] // TM
    row = lambda c: pl.BlockSpec((TM, c), lambda i: (i, 0))

    def strided(a):
        _, d, _, c = a.shape
        return pl.BlockSpec((None, d, TM // d, c), lambda i: (i // per_batch, 0, i % per_batch, 0))

    (o1, s1), (o2, s2), (o3, s3) = parts
    return pl.pallas_call(
        _out_ffn_kernel,
        grid=(n // TM,),
        in_specs=[row(D_MODEL), row(A_Q), strided(o1), strided(o2), strided(o3),
                  strided(s1), strided(s2), strided(s3), _resident(expand.shape),
                  _resident(wo.shape), _resident((1, D_MODEL)), _resident((1, D_MODEL)),
                  _resident((1, D_MODEL)), _resident(wg.shape), _resident(wu.shape),
                  _resident(wd.shape)],
        out_specs=row(D_MODEL),
        out_shape=jax.ShapeDtypeStruct((n, D_MODEL), F32),
        scratch_shapes=[pltpu.VMEM((B_QKV // LANES, TM, LANES), F32),
                        pltpu.VMEM((B_QKV // LANES, TM, LANES), F32),
                        pltpu.VMEM((1, TM, LANES), F32), pltpu.VMEM((1, TM, LANES), F32)],
        compiler_params=pltpu.CompilerParams(
            dimension_semantics=("arbitrary",), vmem_limit_bytes=VMEM_LIMIT),
        name="out_ffn2",
    )(x1, heads_a, o1, o2, o3, s1, s2, s3, expand, wo, mix_g, pre_g, post_g, wg, wu, wd)


def _rope_tables(seq):
    pos = np.arange(seq)
    row, col = (pos // GRID_W)[:, None], (pos % GRID_W)[:, None]
    dim_a = HEAD_DIM // 2
    fa = ROPE_THETA ** (-np.arange(0, dim_a, 2, dtype=np.float64) / dim_a)
    fb = ROPE_THETA ** (-np.arange(0, HEAD_DIM, 2, dtype=np.float64) / HEAD_DIM)
    ar, ac, ab = row * fa[None, :], col * fa[None, :], pos[:, None] * fb[None, :]
    cos_a = np.concatenate([np.cos(ar), np.cos(ar), np.cos(ac), np.cos(ac)], axis=-1)
    sin_a = np.concatenate([-np.sin(ar), np.sin(ar), -np.sin(ac), np.sin(ac)], axis=-1)
    cos_b = np.concatenate([np.cos(ab), np.cos(ab)], axis=-1)
    sin_b = np.concatenate([-np.sin(ab), np.sin(ab)], axis=-1)
    two = lambda t: jnp.asarray(np.tile(t, (1, LANES // HEAD_DIM)).astype(np.float32))
    return two(cos_a), two(sin_a), two(cos_b), two(sin_b)


def _layer(x, ffn1_pre_g, ffn1_post_g, ffn1_w_gate, ffn1_w_up, ffn1_w_down,
           mix_pre_g, mix_post_g, w_qkv, q_norm_g, k_norm_g, w_out,
           ffn2_pre_g, ffn2_post_g, ffn2_w_gate, ffn2_w_up, ffn2_w_down, tables, seg, expand):
    B, S, D = x.shape
    vec = lambda g: g.reshape(1, -1).astype(F32)
    bf = lambda w: w.astype(BF16)

    x1 = _ffn_call(x.reshape(B * S, D), vec(ffn1_pre_g), vec(ffn1_post_g),
                   bf(ffn1_w_gate), bf(ffn1_w_up), bf(ffn1_w_down))
    heads_per_tile = 2 * LANES // HEAD_DIM
    kv_gain = jnp.concatenate([jnp.tile(vec(k_norm_g), (1, N_KV_A)), jnp.ones((1, A_KV), F32)], axis=1)
    qa, ka, vta, qkv_b = _qkv_call(x1.reshape(B, S, D), vec(mix_pre_g), bf(w_qkv), seg,
                                   jnp.tile(vec(q_norm_g), (1, heads_per_tile)), kv_gain, *tables)
    score_bound = ((HEAD_DIM * SCALE * LOG2E) * jnp.max(jnp.abs(q_norm_g))
                   * jnp.max(jnp.abs(k_norm_g)))
    heads_a = lax.cond(score_bound <= MAX_SCORE_BOUND,
                       lambda: _attn_a_bounded_call(qa, ka, vta),
                       lambda: _attn_a_call(qa, ka, vta))
    parts = [_attn_b_call(*qkv) for qkv in qkv_b]
    y = _out_ffn_call(x1, heads_a.reshape(B * S, A_Q), parts, expand,
                      bf(w_out), vec(mix_post_g), vec(ffn2_pre_g), vec(ffn2_post_g),
                      bf(ffn2_w_gate), bf(ffn2_w_up), bf(ffn2_w_down))
    return y.reshape(B, S, D)


def kernel(x, ffn1_pre_g, ffn1_post_g, ffn1_w_gate, ffn1_w_up, ffn1_w_down, mix_pre_g, mix_post_g, w_qkv, q_norm_g, k_norm_g, w_out, ffn2_pre_g, ffn2_post_g, ffn2_w_gate, ffn2_w_up, ffn2_w_down):
    assert all(w // 2 // d == SPAN_B for w, d in DILATED_CONFIGS) and TK_A % TM == 0
    S = x.shape[1]
    tables = _rope_tables(S)
    head_of_lane = jnp.arange(2 * LANES) // HEAD_DIM
    seg = (head_of_lane[:, None] == head_of_lane[None, :]).astype(BF16) / HEAD_DIM
    expand = (jnp.arange(LANES)[:, None] == jnp.arange(B_QKV)[None, :] // HEAD_DIM).astype(BF16)
    params = (ffn1_pre_g, ffn1_post_g, ffn1_w_gate, ffn1_w_up, ffn1_w_down, mix_pre_g, mix_post_g,
              w_qkv, q_norm_g, k_norm_g, w_out, ffn2_pre_g, ffn2_post_g, ffn2_w_gate, ffn2_w_up,
              ffn2_w_down)
    for l in range(ffn1_pre_g.shape[0]):
        x = _layer(x, *(p[l] for p in params), tables, seg, expand)
    return x
```

```python
import functools

import jax
import jax.numpy as jnp
import numpy as np
from jax import lax
from jax.experimental import pallas as pl
from jax.experimental.pallas import tpu as pltpu

D_MODEL = 1024
HEAD_DIM = 64
N_HEADS_A = 8
N_KV_A = 2
N_HEADS_B = 8
DILATED_CONFIGS = ((128, 1), (512, 4), (2048, 16))
DILATIONS = tuple(d for _, d in DILATED_CONFIGS)
GRID_W = 64
ROPE_THETA = 10000.0
D_FF = 2816
EPS = 1e-6
NEG_INF = -1e30

A_Q = N_HEADS_A * HEAD_DIM
A_KV = N_KV_A * HEAD_DIM
B_QKV = N_HEADS_B * HEAD_DIM
QKV_COLS = A_Q + 2 * A_KV + 3 * B_QKV
SCALE = HEAD_DIM ** -0.5
LOG2E = 1.4426950408889634

LANES = 128
SUBLANES = 8
FF_CHUNKS = ((0, 1024), (1024, 2048), (2048, 2816))
TM = 512
TM_FFN1 = 1024
TQ_A = 256
TK_A = 1024
TQ_BOUNDED = 512
QH_A = 256
SUB_A = 256
MAX_SCORE_BOUND = 48.0
VT_ROWS = HEAD_DIM + 16
TQ_B = 1024
SUB_B = 128
SPAN_B = 64
VMEM_LIMIT = 52 * 1024 * 1024

BF16 = jnp.bfloat16
F32 = jnp.float32


def _dot(a, b):
    return jnp.dot(a, b, preferred_element_type=F32)


def _dot_nt(a, b):
    return lax.dot_general(a, b, (((1,), (1,)), ((), ())), preferred_element_type=F32)


def _rms(x, g):
    ms = jnp.mean(x * x, axis=-1, keepdims=True)
    return x * lax.rsqrt(ms + EPS) * g


def _swiglu_half_step(x, pre_g, post_g, wg_ref, wu_ref, wd_ref):
    h = _rms(x, pre_g).astype(BF16)
    f = None
    for lo, hi in FF_CHUNKS:
        g = _dot(h, wg_ref[:, lo:hi])
        u = _dot(h, wu_ref[:, lo:hi])
        a = (g / (1.0 + jnp.exp(-g)) * u).astype(BF16)
        part = _dot(a, wd_ref[lo:hi, :])
        f = part if f is None else f + part
    return x + 0.5 * _rms(f, post_g)


def _ffn_kernel(x_ref, pre_ref, post_ref, wg_ref, wu_ref, wd_ref, o_ref):
    o_ref[...] = _swiglu_half_step(x_ref[...], pre_ref[...], post_ref[...],
                                   wg_ref, wu_ref, wd_ref)


def _resident(shape):
    nd = len(shape)
    return pl.BlockSpec(shape, lambda *_: (0,) * nd, pipeline_mode=pl.Buffered(1))


def _ffn_call(x2d, pre_g, post_g, wg, wu, wd):
    n = x2d.shape[0]
    row = pl.BlockSpec((TM_FFN1, D_MODEL), lambda i: (i, 0))
    return pl.pallas_call(
        _ffn_kernel,
        grid=(n // TM_FFN1,),
        in_specs=[row, _resident((1, D_MODEL)), _resident((1, D_MODEL)),
                  _resident(wg.shape), _resident(wu.shape), _resident(wd.shape)],
        out_specs=row,
        out_shape=jax.ShapeDtypeStruct(x2d.shape, F32),
        compiler_params=pltpu.CompilerParams(
            dimension_semantics=("arbitrary",), vmem_limit_bytes=VMEM_LIMIT),
        name="ffn1",
    )(x2d, pre_g, post_g, wg, wu, wd)


def _rope(c, cos, sin_signed, half):
    lane = lax.broadcasted_iota(jnp.int32, c.shape, 1)
    first = (lane % (2 * half)) < half
    partner = jnp.where(first, pltpu.roll(c, LANES - half, 1), pltpu.roll(c, half, 1))
    return c * cos + partner * sin_signed


def _head_rms(c, seg_ref, g):
    sq = c * c
    hi = sq.astype(BF16)
    lo = (sq - hi.astype(F32)).astype(BF16)
    ms = _dot(hi, seg_ref[...]) + _dot(lo, seg_ref[...])
    return c * lax.rsqrt(ms + EPS) * g


def _dup_halves(c):
    lane = lax.broadcasted_iota(jnp.int32, c.shape, 1)
    low = lane < HEAD_DIM
    r = pltpu.roll(c, HEAD_DIM, 1)
    return jnp.where(low, c, r), jnp.where(low, r, c)


def _qkv_kernel(x_ref, pre_ref, w_ref, seg_ref, qg_ref, kg_ref,
                cosa_ref, sina_ref, cosb_ref, sinb_ref,
                qa_ref, ka_ref, va_ref, *rest):
    b_refs, (qf_sc, kf_sc, vf_sc) = rest[:-3], rest[-3:]
    h = _rms(x_ref[...], pre_ref[...]).astype(BF16)
    cosa, sina = cosa_ref[...], sina_ref[...]
    cosb, sinb = cosb_ref[...], sinb_ref[...]
    wide = 2 * LANES

    qa = _dot(h, w_ref[:, 0:A_Q])
    for t in range(A_Q // wide):
        qn = _head_rms(qa[:, t * wide:(t + 1) * wide], seg_ref, qg_ref[...])
        for c in range(wide // LANES):
            q = _rope(qn[:, c * LANES:(c + 1) * LANES], cosa, sina, HEAD_DIM // 4) * (SCALE * LOG2E)
            col = t * wide + c * LANES
            qa_ref[:, col:col + LANES] = q.astype(BF16)

    kv = _dot(h, w_ref[:, A_Q:A_Q + 2 * A_KV])
    k = _rope(_head_rms(kv, seg_ref, kg_ref[...])[:, 0:A_KV], cosa, sina, HEAD_DIM // 4)
    k0, k1 = _dup_halves(k)
    ka_ref[0] = k0.astype(BF16)
    ka_ref[1] = k1.astype(BF16)
    vt = kv[:, A_KV:].T.astype(BF16)
    ones = jnp.ones((VT_ROWS - HEAD_DIM, TM), BF16)
    for g in range(N_KV_A):
        va_ref[g, 0:HEAD_DIM, :] = vt[g * HEAD_DIM:(g + 1) * HEAD_DIM, :]
        va_ref[g, HEAD_DIM:, :] = ones

    base = A_Q + 2 * A_KV
    qb = _dot(h, w_ref[:, base:base + B_QKV])
    kb = _dot(h, w_ref[:, base + B_QKV:base + 2 * B_QKV])
    vb = _dot(h, w_ref[:, base + 2 * B_QKV:base + 3 * B_QKV])
    for c in range(B_QKV // LANES):
        sl = slice(c * LANES, (c + 1) * LANES)
        qf_sc[c] = _rope(qb[:, sl], cosb, sinb, HEAD_DIM // 2) * (SCALE * LOG2E)
        kf_sc[c] = _rope(kb[:, sl], cosb, sinb, HEAD_DIM // 2)
        vf_sc[c] = vb[:, sl]
    for j, src in enumerate((qf_sc, kf_sc, vf_sc)):
        for i, d in enumerate(DILATIONS):
            out = b_refs[3 * i + j]
            for r in range(d):
                rows = slice(None) if d == 1 else pl.ds(r, TM // d, stride=d)
                for c in range(B_QKV // LANES):
                    out[r, :, c * LANES:(c + 1) * LANES] = src[c, rows, :].astype(BF16)


def _qkv_call(x1, pre_g, w, seg, qg, kg, cosa, sina, cosb, sinb):
    B, S, _ = x1.shape
    tab = pl.BlockSpec((TM, LANES), lambda i, b: (i, 0))
    wide = pl.BlockSpec((None, TM, A_Q), lambda i, b: (b, i, 0))
    dup = pl.BlockSpec((None, N_KV_A, TM, LANES), lambda i, b: (b, 0, i, 0))
    per = TK_A // TM
    vt = pl.BlockSpec((None, N_KV_A, None, VT_ROWS, TM), lambda i, b: (b, 0, i // per, 0, i % per))
    wide_shape = jax.ShapeDtypeStruct((B, S, A_Q), BF16)
    dup_shape = jax.ShapeDtypeStruct((B, N_KV_A, S, LANES), BF16)
    vt_shape = jax.ShapeDtypeStruct((B, N_KV_A, S // TK_A, VT_ROWS, TK_A), BF16)
    b_specs, b_shapes = [], []
    for d in DILATIONS:
        b_specs += [pl.BlockSpec((None, d, TM // d, B_QKV), lambda i, b: (b, 0, i, 0))] * 3
        b_shapes += [jax.ShapeDtypeStruct((B, d, S // d, B_QKV), BF16)] * 3
    outs = pl.pallas_call(
        _qkv_kernel,
        grid=(S // TM, B),
        in_specs=[pl.BlockSpec((None, TM, D_MODEL), lambda i, b: (b, i, 0)),
                  _resident((1, D_MODEL)), _resident(w.shape), _resident(seg.shape),
                  _resident(qg.shape), _resident(kg.shape), tab, tab, tab, tab],
        out_specs=[wide, dup, vt] + b_specs,
        out_shape=[wide_shape, dup_shape, vt_shape] + b_shapes,
        scratch_shapes=[pltpu.VMEM((B_QKV // LANES, TM, LANES), F32)] * 3,
        compiler_params=pltpu.CompilerParams(
            dimension_semantics=("arbitrary", "arbitrary"), vmem_limit_bytes=VMEM_LIMIT),
        name="qkv",
    )(x1, pre_g, w, seg, qg, kg, cosa, sina, cosb, sinb)
    qa, ka, vta = outs[:3]
    qkv_b = [tuple(outs[3 + 3 * i:6 + 3 * i]) for i in range(len(DILATIONS))]
    return qa, ka, vta, qkv_b


def _attn_a_kernel(q_ref, k_ref, vt_ref, o_ref, qs_sc, m_sc, acc_sc, sa_sc, sb_sc, cma_sc, cmb_sc):
    tq = q_ref.shape[0]
    n_pairs = qs_sc.shape[0]
    low = lax.broadcasted_iota(jnp.int32, (tq, LANES), 1) < HEAD_DIM
    for c in range(n_pairs):
        qc = q_ref[:, c * LANES:(c + 1) * LANES]
        zero = jnp.zeros_like(qc)
        qs_sc[c, 0:tq, :] = jnp.where(low, qc, zero)
        qs_sc[c, tq:, :] = jnp.where(low, zero, qc)

    m_sc[...] = jnp.full(m_sc.shape, NEG_INF, F32)
    acc_sc[...] = jnp.zeros(acc_sc.shape, F32)

    n_k = vt_ref.shape[0]

    def scores(kb, s_out, cm_out):
        k = k_ref[pl.ds(pl.multiple_of(kb * TK_A, TK_A), TK_A), :]
        for c in range(n_pairs):
            s = _dot_nt(k, qs_sc[c])
            s_out[c] = s
            cm_out[c] = jnp.max(s, axis=0, keepdims=True)

    def consume(kb, s_in, cm_in):
        vt = vt_ref[kb]
        for c in range(n_pairs):
            m_old = m_sc[c]
            m_new = jnp.maximum(m_old, cm_in[c])
            alpha = jnp.exp2(m_old - m_new)
            p = jnp.exp2(s_in[c] - m_new).astype(BF16)
            acc_sc[c] = alpha * acc_sc[c] + _dot(vt, p)
            m_sc[c] = m_new

    bufs = ((sa_sc, cma_sc), (sb_sc, cmb_sc))
    scores(0, *bufs[0])

    def fused(kb, cur, nxt):
        s_in, cm_in = cur
        s_out, cm_out = nxt
        k0 = pl.multiple_of((kb + 1) * TK_A, TK_A)
        m_new, alpha, pv, cm = [], [], [], []
        for c in range(n_pairs):
            m_old = m_sc[c]
            m_new.append(jnp.maximum(m_old, cm_in[c]))
            alpha.append(jnp.exp2(m_old - m_new[c]))
            pv.append(None)
            cm.append(None)
        for j in range(TK_A // SUB_A):
            rows = slice(j * SUB_A, (j + 1) * SUB_A)
            k = k_ref[pl.ds(k0 + j * SUB_A, SUB_A), :]
            vt = vt_ref[kb, :, rows]
            for c in range(n_pairs):
                s = _dot_nt(k, qs_sc[c])
                s_out[c, rows, :] = s
                part = jnp.max(s, axis=0, keepdims=True)
                cm[c] = part if cm[c] is None else jnp.maximum(cm[c], part)
            for c in range(n_pairs):
                p = jnp.exp2(s_in[c, rows, :] - m_new[c]).astype(BF16)
                part = _dot(vt, p)
                pv[c] = part if pv[c] is None else pv[c] + part
        for c in range(n_pairs):
            cm_out[c] = cm[c]
            acc_sc[c] = alpha[c] * acc_sc[c] + pv[c]
            m_sc[c] = m_new[c]

    def body(kb, carry):
        for parity in range(2):
            @pl.when(kb % 2 == parity)
            def _():
                fused(kb, bufs[parity], bufs[1 - parity])
        return carry

    lax.fori_loop(0, n_k - 1, body, 0)
    consume(n_k - 1, *bufs[(n_k - 1) % 2])
    for c in range(n_pairs):
        o = acc_sc[c, 0:HEAD_DIM, :] / acc_sc[c, HEAD_DIM:HEAD_DIM + 1, :]
        pair = jnp.concatenate([o[:, 0:tq], o[:, tq:]], axis=0)
        o_ref[:, c * LANES:(c + 1) * LANES] = pair.T.astype(o_ref.dtype)


def _attn_a_bounded_kernel(q_ref, k_ref, vt_ref, o_ref, qt_sc, acc_sc):
    n_pairs = q_ref.shape[1] // LANES
    units = [(c, h) for h in range(q_ref.shape[0] // QH_A) for c in range(n_pairs)]
    low = lax.broadcasted_iota(jnp.int32, (QH_A, LANES), 1) < HEAD_DIM
    for u, (c, h) in enumerate(units):
        qc = q_ref[h * QH_A:(h + 1) * QH_A, c * LANES:(c + 1) * LANES]
        zero = jnp.zeros_like(qc)
        qs = jnp.concatenate([jnp.where(low, qc, zero), jnp.where(low, zero, qc)], axis=0)
        qt_sc[u] = qs.T
    n_sub = TK_A // SUB_A
    stream = [(t, u) for t in range(vt_ref.shape[0] * n_sub) for u in range(len(units))]

    def scores(i):
        t, u = stream[i]
        return _dot(k_ref[t * SUB_A:(t + 1) * SUB_A, :], qt_sc[u])

    depth = n_pairs
    ahead = [scores(i) for i in range(depth)]
    pv = [None] * len(units)
    den = [None] * len(units)
    for i, (t, u) in enumerate(stream):
        if i + depth < len(stream):
            ahead.append(scores(i + depth))
        cols = slice((t % n_sub) * SUB_A, (t % n_sub + 1) * SUB_A)
        p = jnp.exp2(ahead.pop(0))
        part = _dot(vt_ref[t // n_sub, 0:HEAD_DIM, cols], p.astype(BF16))
        pv[u] = part if pv[u] is None else pv[u] + part
        sums = jnp.sum(p.reshape(SUB_A // SUBLANES, SUBLANES, p.shape[1]), axis=0)
        den[u] = sums if den[u] is None else den[u] + sums
    for u in range(len(units)):
        acc_sc[u, 0:HEAD_DIM, :] = pv[u]
        acc_sc[u, HEAD_DIM:HEAD_DIM + SUBLANES, :] = den[u]
    for u, (c, h) in enumerate(units):
        l = jnp.sum(acc_sc[u, HEAD_DIM:HEAD_DIM + SUBLANES, :], axis=0, keepdims=True)
        o = acc_sc[u, 0:HEAD_DIM, :] / l
        pair = jnp.concatenate([o[:, 0:QH_A], o[:, QH_A:]], axis=0)
        o_ref[h * QH_A:(h + 1) * QH_A, c * LANES:(c + 1) * LANES] = pair.T.astype(o_ref.dtype)


def _attn_a_bounded_call(qa, ka, vta):
    B, S, _ = qa.shape
    qcols = A_Q // N_KV_A
    n_units = (qcols // LANES) * (TQ_BOUNDED // QH_A)
    qspec = pl.BlockSpec((None, TQ_BOUNDED, qcols), lambda b, g, i: (b, i, g))
    kspec = pl.BlockSpec((None, None, S, LANES), lambda b, g, i: (b, g, 0, 0))
    vspec = pl.BlockSpec((None, None, S // TK_A, VT_ROWS, TK_A), lambda b, g, i: (b, g, 0, 0, 0))
    return pl.pallas_call(
        _attn_a_bounded_kernel,
        grid=(B, N_KV_A, S // TQ_BOUNDED),
        in_specs=[qspec, kspec, vspec],
        out_specs=qspec,
        out_shape=jax.ShapeDtypeStruct((B, S, A_Q), BF16),
        scratch_shapes=[pltpu.VMEM((n_units, LANES, 2 * QH_A), BF16),
                        pltpu.VMEM((n_units, VT_ROWS, 2 * QH_A), F32)],
        compiler_params=pltpu.CompilerParams(
            dimension_semantics=("arbitrary", "arbitrary", "arbitrary"),
            vmem_limit_bytes=VMEM_LIMIT),
        name="attn_a_bounded",
    )(qa, ka, vta)


def _attn_a_call(qa, ka, vta):
    B, S, _ = qa.shape
    qcols = A_Q // N_KV_A
    pairs = N_HEADS_A // N_KV_A // 2
    qspec = pl.BlockSpec((None, TQ_A, qcols), lambda b, g, i: (b, i, g))
    kspec = pl.BlockSpec((None, None, S, LANES), lambda b, g, i: (b, g, 0, 0))
    vspec = pl.BlockSpec((None, None, S // TK_A, VT_ROWS, TK_A), lambda b, g, i: (b, g, 0, 0, 0))
    return pl.pallas_call(
        _attn_a_kernel,
        grid=(B, N_KV_A, S // TQ_A),
        in_specs=[qspec, kspec, vspec],
        out_specs=qspec,
        out_shape=jax.ShapeDtypeStruct((B, S, A_Q), BF16),
        scratch_shapes=[pltpu.VMEM((pairs, 2 * TQ_A, LANES), BF16),
                        pltpu.VMEM((pairs, 1, 2 * TQ_A), F32),
                        pltpu.VMEM((pairs, VT_ROWS, 2 * TQ_A), F32),
                        pltpu.VMEM((pairs, TK_A, 2 * TQ_A), F32),
                        pltpu.VMEM((pairs, TK_A, 2 * TQ_A), F32),
                        pltpu.VMEM((pairs, 1, 2 * TQ_A), F32),
                        pltpu.VMEM((pairs, 1, 2 * TQ_A), F32)],
        compiler_params=pltpu.CompilerParams(
            dimension_semantics=("arbitrary", "arbitrary", "arbitrary"),
            vmem_limit_bytes=VMEM_LIMIT),
        name="attn_a",
    )(qa, ka, vta)


def _attn_b_kernel(q_ref, kp_ref, kc_ref, kn_ref, vp_ref, vc_ref, vn_ref,
                   o_ref, st_ref, kbuf, vbuf, *, seq_len):
    j = pl.program_id(2)
    tq = q_ref.shape[0]
    kbuf[0:SPAN_B, :] = kp_ref[...]
    kbuf[SPAN_B:SPAN_B + tq, :] = kc_ref[...]
    kbuf[SPAN_B + tq:, :] = kn_ref[...]
    vbuf[0:SPAN_B, :] = vp_ref[...]
    vbuf[SPAN_B:SPAN_B + tq, :] = vc_ref[...]
    vbuf[SPAN_B + tq:, :] = vn_ref[...]

    n_keys = SUB_B + 2 * SPAN_B
    c_idx = lax.broadcasted_iota(jnp.int32, (n_keys, 2 * SUB_B), 0)
    a_idx = lax.broadcasted_iota(jnp.int32, (n_keys, 2 * SUB_B), 1) % SUB_B
    band_bias = jnp.where(jnp.abs(c_idx - SPAN_B - a_idx) <= SPAN_B, 0.0, NEG_INF).astype(F32)
    low = lax.broadcasted_iota(jnp.int32, (SUB_B, LANES), 1) < HEAD_DIM
    ones = jnp.ones((VT_ROWS - HEAD_DIM, n_keys), BF16)
    pad = jnp.zeros((LANES - 2 * N_HEADS_B, SUB_B), F32)

    n_sub = tq // SUB_B
    n_pairs = N_HEADS_B // 2
    biases = {}

    def bias_of(i):
        if i not in biases:
            first_key = j * tq + i * SUB_B - SPAN_B
            bias = band_bias
            if i == 0:
                bias = jnp.where(c_idx >= -first_key, bias, NEG_INF)
            if i == n_sub - 1:
                bias = jnp.where(c_idx < seq_len - first_key, bias, NEG_INF)
            biases[i] = bias
        return biases[i]

    def scores(i, hp):
        q0, cols = i * SUB_B, slice(hp * LANES, (hp + 1) * LANES)
        qp = q_ref[q0:q0 + SUB_B, cols]
        zero = jnp.zeros_like(qp)
        qs = jnp.concatenate([jnp.where(low, qp, zero), jnp.where(low, zero, qp)], axis=0)
        return _dot(kbuf[q0:q0 + n_keys, cols], qs.T) + bias_of(i)

    def finish(i, hp, s):
        q0, cols = i * SUB_B, slice(hp * LANES, (hp + 1) * LANES)
        m = jnp.max(s, axis=0, keepdims=True)
        p = jnp.exp2(s - m).astype(BF16)
        vt = jnp.concatenate([vbuf[q0:q0 + n_keys, cols].T, ones], axis=0)
        o_all = _dot(vt, p)
        o_t = jnp.concatenate([o_all[0:HEAD_DIM, 0:SUB_B],
                               o_all[HEAD_DIM:LANES, SUB_B:]], axis=0)
        o_ref[q0:q0 + SUB_B, cols] = o_t.T.astype(o_ref.dtype)
        l = o_all[LANES:LANES + 1, :]
        return [m[:, 0:SUB_B], m[:, SUB_B:]], [l[:, 0:SUB_B], l[:, SUB_B:]]

    stream = [(i, hp) for i in range(n_sub) for hp in range(n_pairs)]
    depth = 8
    ahead = [scores(*stream[u]) for u in range(depth)]
    ms, ls = [], []
    for u, (i, hp) in enumerate(stream):
        if u + depth < len(stream):
            ahead.append(scores(*stream[u + depth]))
        m2, l2 = finish(i, hp, ahead.pop(0))
        ms += m2
        ls += l2
        if hp == n_pairs - 1:
            st_t = jnp.concatenate(ms + ls + [pad], axis=0)
            st_ref[i * SUB_B:(i + 1) * SUB_B, :] = st_t.T
            ms, ls = [], []


def _attn_b_call(qb, kb, vb):
    B, d, L, C = qb.shape
    nh = L // SPAN_B
    tq = min(TQ_B, L)
    per = tq // SPAN_B
    cur = pl.BlockSpec((None, None, tq, C), lambda b, r, j: (b, r, j, 0))
    prev = pl.BlockSpec((None, None, SPAN_B, C),
                        lambda b, r, j: (b, r, jnp.maximum(j * per - 1, 0), 0))
    nxt = pl.BlockSpec((None, None, SPAN_B, C),
                       lambda b, r, j: (b, r, jnp.minimum((j + 1) * per, nh - 1), 0))
    st_spec = pl.BlockSpec((None, None, tq, LANES), lambda b, r, j: (b, r, j, 0))
    return pl.pallas_call(
        functools.partial(_attn_b_kernel, seq_len=L),
        grid=(B, d, L // tq),
        in_specs=[cur, prev, cur, nxt, prev, cur, nxt],
        out_specs=[cur, st_spec],
        out_shape=[jax.ShapeDtypeStruct((B, d, L, C), BF16),
                   jax.ShapeDtypeStruct((B, d, L, LANES), F32)],
        scratch_shapes=[pltpu.VMEM((tq + 2 * SPAN_B, C), BF16),
                        pltpu.VMEM((tq + 2 * SPAN_B, C), BF16)],
        compiler_params=pltpu.CompilerParams(
            dimension_semantics=("arbitrary", "arbitrary", "arbitrary"),
            vmem_limit_bytes=VMEM_LIMIT),
        name=f"attn_b_d{d}",
    )(qb, kb, kb, kb, vb, vb, vb)


def _natural_order(ref, scratch):
    d, _, cols = ref.shape
    if d == 1:
        return ref[0].astype(F32)
    for r in range(d):
        for c in range(cols // LANES):
            scratch[c, pl.ds(r, TM // d, stride=d), :] = ref[r, :, c * LANES:(c + 1) * LANES].astype(F32)
    return jnp.concatenate([scratch[c] for c in range(cols // LANES)], axis=1)


def _out_ffn_kernel(x_ref, ha_ref, o1_ref, o2_ref, o3_ref, s1_ref, s2_ref, s3_ref, ex_ref,
                    wo_ref, mixg_ref, pre_ref, post_ref, wg_ref, wu_ref, wd_ref, y_ref,
                    of2_sc, of3_sc, sf2_sc, sf3_sc):
    stats = (_natural_order(s1_ref, None), _natural_order(s2_ref, sf2_sc),
             _natural_order(s3_ref, sf3_sc))
    parts = (_natural_order(o1_ref, None), _natural_order(o2_ref, of2_sc),
             _natural_order(o3_ref, of3_sc))
    is_max = lax.broadcasted_iota(jnp.int32, (TM, LANES), 1) < N_HEADS_B
    m_all = jnp.maximum(jnp.maximum(stats[0], stats[1]), stats[2])
    es = [jnp.exp2(s - m_all) for s in stats]
    den = None
    for e, s in zip(es, stats):
        term = e * pltpu.roll(s, LANES - N_HEADS_B, 1)
        den = term if den is None else den + term
    heads_b = None
    for e, o in zip(es, parts):
        w = jnp.where(is_max, e / den, 0.0)
        hi = w.astype(BF16)
        lo = (w - hi.astype(F32)).astype(BF16)
        term = (_dot(hi, ex_ref[...]) + _dot(lo, ex_ref[...])) * o
        heads_b = term if heads_b is None else heads_b + term

    mixed = _dot(ha_ref[...], wo_ref[0:A_Q, :]) + _dot(heads_b.astype(BF16), wo_ref[A_Q:, :])
    x2 = x_ref[...] + _rms(mixed, mixg_ref[...])
    y_ref[...] = _swiglu_half_step(x2, pre_ref[...], post_ref[...], wg_ref, wu_ref, wd_ref)


def _out_ffn_call(x1, heads_a, parts, expand, wo, mix_g, pre_g, post_g, wg, wu, wd):
    n = x1.shape[0]
    per_batch = parts[0][0].shape[2] // TM
    row = lambda c: pl.BlockSpec((TM, c), lambda i: (i, 0))

    def strided(a):
        _, d, _, c = a.shape
        return pl.BlockSpec((None, d, TM // d, c), lambda i: (i // per_batch, 0, i % per_batch, 0))

    (o1, s1), (o2, s2), (o3, s3) = parts
    return pl.pallas_call(
        _out_ffn_kernel,
        grid=(n // TM,),
        in_specs=[row(D_MODEL), row(A_Q), strided(o1), strided(o2), strided(o3),
                  strided(s1), strided(s2), strided(s3), _resident(expand.shape),
                  _resident(wo.shape), _resident((1, D_MODEL)), _resident((1, D_MODEL)),
                  _resident((1, D_MODEL)), _resident(wg.shape), _resident(wu.shape),
                  _resident(wd.shape)],
        out_specs=row(D_MODEL),
        out_shape=jax.ShapeDtypeStruct((n, D_MODEL), F32),
        scratch_shapes=[pltpu.VMEM((B_QKV // LANES, TM, LANES), F32),
                        pltpu.VMEM((B_QKV // LANES, TM, LANES), F32),
                        pltpu.VMEM((1, TM, LANES), F32), pltpu.VMEM((1, TM, LANES), F32)],
        compiler_params=pltpu.CompilerParams(
            dimension_semantics=("arbitrary",), vmem_limit_bytes=VMEM_LIMIT),
        name="out_ffn2",
    )(x1, heads_a, o1, o2, o3, s1, s2, s3, expand, wo, mix_g, pre_g, post_g, wg, wu, wd)


def _rope_tables(seq):
    pos = np.arange(seq)
    row, col = (pos // GRID_W)[:, None], (pos % GRID_W)[:, None]
    dim_a = HEAD_DIM // 2
    fa = ROPE_THETA ** (-np.arange(0, dim_a, 2, dtype=np.float64) / dim_a)
    fb = ROPE_THETA ** (-np.arange(0, HEAD_DIM, 2, dtype=np.float64) / HEAD_DIM)
    ar, ac, ab = row * fa[None, :], col * fa[None, :], pos[:, None] * fb[None, :]
    cos_a = np.concatenate([np.cos(ar), np.cos(ar), np.cos(ac), np.cos(ac)], axis=-1)
    sin_a = np.concatenate([-np.sin(ar), np.sin(ar), -np.sin(ac), np.sin(ac)], axis=-1)
    cos_b = np.concatenate([np.cos(ab), np.cos(ab)], axis=-1)
    sin_b = np.concatenate([-np.sin(ab), np.sin(ab)], axis=-1)
    two = lambda t: jnp.asarray(np.tile(t, (1, LANES // HEAD_DIM)).astype(np.float32))
    return two(cos_a), two(sin_a), two(cos_b), two(sin_b)


def _layer(x, ffn1_pre_g, ffn1_post_g, ffn1_w_gate, ffn1_w_up, ffn1_w_down,
           mix_pre_g, mix_post_g, w_qkv, q_norm_g, k_norm_g, w_out,
           ffn2_pre_g, ffn2_post_g, ffn2_w_gate, ffn2_w_up, ffn2_w_down, tables, seg, expand):
    B, S, D = x.shape
    vec = lambda g: g.reshape(1, -1).astype(F32)
    bf = lambda w: w.astype(BF16)

    x1 = _ffn_call(x.reshape(B * S, D), vec(ffn1_pre_g), vec(ffn1_post_g),
                   bf(ffn1_w_gate), bf(ffn1_w_up), bf(ffn1_w_down))
    heads_per_tile = 2 * LANES // HEAD_DIM
    kv_gain = jnp.concatenate([jnp.tile(vec(k_norm_g), (1, N_KV_A)), jnp.ones((1, A_KV), F32)], axis=1)
    qa, ka, vta, qkv_b = _qkv_call(x1.reshape(B, S, D), vec(mix_pre_g), bf(w_qkv), seg,
                                   jnp.tile(vec(q_norm_g), (1, heads_per_tile)), kv_gain, *tables)
    score_bound = ((HEAD_DIM * SCALE * LOG2E) * jnp.max(jnp.abs(q_norm_g))
                   * jnp.max(jnp.abs(k_norm_g)))
    heads_a = lax.cond(score_bound <= MAX_SCORE_BOUND,
                       lambda: _attn_a_bounded_call(qa, ka, vta),
                       lambda: _attn_a_call(qa, ka, vta))
    parts = [_attn_b_call(*qkv) for qkv in qkv_b]
    y = _out_ffn_call(x1, heads_a.reshape(B * S, A_Q), parts, expand,
                      bf(w_out), vec(mix_post_g), vec(ffn2_pre_g), vec(ffn2_post_g),
                      bf(ffn2_w_gate), bf(ffn2_w_up), bf(ffn2_w_down))
    return y.reshape(B, S, D)


def kernel(x, ffn1_pre_g, ffn1_post_g, ffn1_w_gate, ffn1_w_up, ffn1_w_down, mix_pre_g, mix_post_g, w_qkv, q_norm_g, k_norm_g, w_out, ffn2_pre_g, ffn2_post_g, ffn2_w_gate, ffn2_w_up, ffn2_w_down):
    assert all(w // 2 // d == SPAN_B for w, d in DILATED_CONFIGS) and TK_A % TM == 0
    S = x.shape[1]
    tables = _rope_tables(S)
    head_of_lane = jnp.arange(2 * LANES) // HEAD_DIM
    seg = (head_of_lane[:, None] == head_of_lane[None, :]).astype(BF16) / HEAD_DIM
    expand = (jnp.arange(LANES)[:, None] == jnp.arange(B_QKV)[None, :] // HEAD_DIM).astype(BF16)
    params = (ffn1_pre_g, ffn1_post_g, ffn1_w_gate, ffn1_w_up, ffn1_w_down, mix_pre_g, mix_post_g,
              w_qkv, q_norm_g, k_norm_g, w_out, ffn2_pre_g, ffn2_post_g, ffn2_w_gate, ffn2_w_up,
              ffn2_w_down)
    for l in range(ffn1_pre_g.shape[0]):
        x = _layer(x, *(p[l] for p in params), tables, seg, expand)
    return x
```
